```python
import jax, jax.numpy as jnp
from jax import lax
import numpy as np

D_MODEL = 4096
BATCH = 1
SEQ = 8192
DEPTH = 1
DEC_BATCH = 16
DEC_SEQ = 64
PAST_LEN = 1024

CHUNK = 64
D_A = D_MODEL // 2
GMLP_GROUPS = 8
GMLP_CHUNK = 128
N_HEADS = 32
N_KV_HEADS = 8
HEAD_DIM = 64
GQA = N_HEADS // N_KV_HEADS
WINDOW = 128
ROT_DIM = HEAD_DIM // 4
ROPE_THETA = 500000.0
N_GROUPS = 8
EXPERTS_PER_GROUP = 8
N_EXPERTS = N_GROUPS * EXPERTS_PER_GROUP
TOP_K = 2
D_EXPERT = D_MODEL // 8
MOE_BLOCK = 128
EPS = 1e-6

Q_DIM = N_HEADS * HEAD_DIM
KV_DIM = N_KV_HEADS * HEAD_DIM
IN_DIM = 2 * D_A + Q_DIM + 2 * KV_DIM + 2 * D_MODEL
IN_SPLITS = [D_A, 2 * D_A, 2 * D_A + Q_DIM, 2 * D_A + Q_DIM + KV_DIM, 2 * D_A + Q_DIM + 2 * KV_DIM]

kernel_name = 'hybrid_gmlp_swa_sink_hiermoe_step'


def rms_norm(x, gain):
    xf = x.astype(jnp.float32)
    y = xf * lax.rsqrt(jnp.mean(xf * xf, axis=-1, keepdims=True) + EPS)
    return (y * gain.astype(jnp.float32)).astype(x.dtype)


def partial_rope(x, positions):
    half = ROT_DIM // 2
    inv_freq = jnp.power(ROPE_THETA, -jnp.arange(half, dtype=jnp.float32) * 2.0 / ROT_DIM)
    ang = positions.astype(jnp.float32)[:, None] * inv_freq[None, :]
    cos = jnp.cos(ang)[None, :, None, :]
    sin = jnp.sin(ang)[None, :, None, :]
    xr = x[..., :ROT_DIM].astype(jnp.float32)
    x1, x2 = xr[..., :half], xr[..., half:]
    rot = jnp.concatenate([x1 * cos - x2 * sin, x2 * cos + x1 * sin], axis=-1).astype(x.dtype)
    return jnp.concatenate([rot, x[..., ROT_DIM:]], axis=-1)


def pre_mix(x, positions, norm1_g, w_in, gmlp_norm_g, q_norm_g, k_norm_g):
    B, L, _ = x.shape
    h = rms_norm(x, norm1_g)
    z = h @ w_in
    u, va, q, k, v, gl = jnp.split(z, IN_SPLITS, axis=-1)
    u = jax.nn.gelu(u)
    vn = rms_norm(jax.nn.gelu(va), gmlp_norm_g)
    q = partial_rope(rms_norm(q.reshape(B, L, N_HEADS, HEAD_DIM), q_norm_g), positions)
    k = partial_rope(rms_norm(k.reshape(B, L, N_KV_HEADS, HEAD_DIM), k_norm_g), positions)
    v = v.reshape(B, L, N_KV_HEADS, HEAD_DIM)
    return u, vn, q, k, v, gl


def gmlp_spatial(u, vn, w_s, b_s, lc):
    B, L, _ = u.shape
    vb = vn.reshape(B, L // lc, lc, GMLP_GROUPS, D_A // GMLP_GROUPS)
    wm = jnp.tril(w_s[:, :lc, :lc])
    s = jnp.einsum('gij,bnjgc->bnigc', wm, vb) + b_s[:, :lc].T[None, None, :, :, None]
    return u * s.reshape(B, L, D_A)


def attend(q, k, v, valid, sinks):
    B, N, Lq = q.shape[:3]
    qg = q.reshape(B, N, Lq, N_KV_HEADS, GQA, HEAD_DIM)
    s = jnp.einsum('bnqkgd,bnskd->bnkgqs', qg, k, preferred_element_type=jnp.float32) * (HEAD_DIM ** -0.5)
    s = jnp.where(valid[None, :, None, None], s, -jnp.inf)
    sink = sinks.astype(jnp.float32).reshape(N_KV_HEADS, GQA)[None, None, :, :, None, None]
    m = jnp.maximum(jnp.max(s, axis=-1, keepdims=True), sink)
    e = jnp.exp(s - m)
    p = e / (jnp.sum(e, axis=-1, keepdims=True) + jnp.exp(sink - m))
    o = jnp.einsum('bnkgqs,bnskd->bnqkgd', p.astype(v.dtype), v)
    return o.reshape(B, N * Lq, Q_DIM)


def swa_prompt(q, k, v, sinks):
    B, L = q.shape[:2]
    nc = L // CHUNK
    nb = WINDOW // CHUNK + 1
    pad = ((0, 0), (WINDOW, 0), (0, 0), (0, 0))
    kp = jnp.pad(k, pad).reshape(B, nc + nb - 1, CHUNK, N_KV_HEADS, HEAD_DIM)
    vp = jnp.pad(v, pad).reshape(B, nc + nb - 1, CHUNK, N_KV_HEADS, HEAD_DIM)
    kb = jnp.concatenate([kp[:, j:j + nc] for j in range(nb)], axis=2)
    vb = jnp.concatenate([vp[:, j:j + nc] for j in range(nb)], axis=2)
    key_pos = (jnp.arange(nc) * CHUNK)[:, None] - WINDOW + jnp.arange(nb * CHUNK)[None, :]
    valid = jnp.broadcast_to((key_pos >= 0)[:, None, :], (nc, CHUNK, nb * CHUNK))
    return attend(q.reshape(B, nc, CHUNK, N_HEADS, HEAD_DIM), kb, vb, valid, sinks)


def swa_sample(q, k, v, cache_k, cache_v, sinks):
    T = q.shape[1]
    kb = jnp.concatenate([cache_k, k], axis=1)[:, None]
    vb = jnp.concatenate([cache_v, v], axis=1)[:, None]
    valid = jnp.ones((1, T, kb.shape[2]), dtype=bool)
    return attend(q[:, None], kb, vb, valid, sinks)


def moe_experts(h, expert_idx, combine_w, w_gate_e, w_up_e, w_down_e, layer):
    N, D = h.shape
    S = N * TOP_K
    flat_e = expert_idx.reshape(S)
    order = jnp.argsort(flat_e)
    sorted_e = flat_e[order]
    counts = jnp.bincount(flat_e, length=N_EXPERTS)
    padded = ((counts + MOE_BLOCK - 1) // MOE_BLOCK) * MOE_BLOCK
    start = jnp.cumsum(counts) - counts
    pend = jnp.cumsum(padded)
    pstart = pend - padded
    dest = pstart[sorted_e] + jnp.arange(S) - start[sorted_e]
    n_blocks = -(-S // MOE_BLOCK) + N_EXPERTS
    rows = n_blocks * MOE_BLOCK
    row_token = jnp.full((rows,), N, dtype=jnp.int32).at[dest].set((order // TOP_K).astype(jnp.int32))
    h_pad = jnp.concatenate([h, jnp.zeros((1, D), h.dtype)], axis=0)
    xb = h_pad[row_token].reshape(n_blocks, MOE_BLOCK, D)
    block_e = jnp.minimum(jnp.searchsorted(pend, jnp.arange(n_blocks) * MOE_BLOCK, side='right'), N_EXPERTS - 1)

    def run_block(args):
        x_blk, e = args
        a = x_blk @ w_gate_e[layer, e]
        b = x_blk @ w_up_e[layer, e]
        return (jax.nn.silu(a) * b) @ w_down_e[layer, e]

    ys = lax.map(run_block, (xb, block_e)).reshape(rows, D)
    slot_out = jnp.zeros((S, D), ys.dtype).at[order].set(ys[dest])
    return jnp.einsum('nk,nkd->nd', combine_w.astype(h.dtype), slot_out.reshape(N, TOP_K, D))


def hier_moe(h, w_rg, b_rg, w_re, b_re, w_gate_e, w_up_e, w_down_e, layer):
    lg = (h @ w_rg).astype(jnp.float32) + b_rg.astype(jnp.float32)
    pg = jax.nn.softmax(lg, axis=-1)
    _, gsel = lax.top_k(lg, 1)
    le = jnp.einsum('nd,dge->nge', h, w_re).astype(jnp.float32) + b_re.astype(jnp.float32)
    le_g = jnp.take_along_axis(le, gsel[:, :, None], axis=1)[:, 0]
    pe = jax.nn.softmax(le_g, axis=-1)
    pk, ek = lax.top_k(pe, TOP_K)
    w = pk / jnp.sum(pk, axis=-1, keepdims=True) * jnp.take_along_axis(pg, gsel, axis=1)
    idx = gsel * EXPERTS_PER_GROUP + ek
    return moe_experts(h, idx, w, w_gate_e, w_up_e, w_down_e, layer)


def post_mix(x, o_a, o_b, gl, p_a, p_b, w_out, norm2_g, w_rg, b_rg, w_re, b_re, w_gate_e, w_up_e, w_down_e, layer):
    g = jax.nn.sigmoid(gl.astype(jnp.float32)).astype(x.dtype)
    g_a, g_b = g[..., :D_MODEL], g[..., D_MODEL:]
    merged = g_a * (o_a @ p_a) + g_b * (o_b @ p_b)
    x = x + merged @ w_out
    B, L, D = x.shape
    h2 = rms_norm(x, norm2_g).reshape(B * L, D)
    y = hier_moe(h2, w_rg, b_rg, w_re, b_re, w_gate_e, w_up_e, w_down_e, layer)
    return x + y.reshape(B, L, D)


def setup_inputs(seed: int = 0) -> dict:
    key = jax.random.key(seed)
    ks = jax.random.split(key, 23)
    f32 = jnp.float32

    def nrm(k, shape, scale):
        return jax.random.normal(k, shape, f32) * scale

    cache_rows = min(WINDOW, PAST_LEN)
    return {
        'x_prompt': nrm(ks[0], (BATCH, SEQ, D_MODEL), 1.0),
        'x_sample': nrm(ks[1], (DEC_BATCH, DEC_SEQ, D_MODEL), 1.0),
        'cache_k_win': nrm(ks[2], (DEPTH, DEC_BATCH, cache_rows, N_KV_HEADS, HEAD_DIM), 1.0),
        'cache_v_win': nrm(ks[3], (DEPTH, DEC_BATCH, cache_rows, N_KV_HEADS, HEAD_DIM), 1.0),
        'norm1_g': 1.0 + nrm(ks[4], (DEPTH, D_MODEL), 0.02),
        'w_in': nrm(ks[5], (DEPTH, D_MODEL, IN_DIM), D_MODEL ** -0.5),
        'gmlp_norm_g': 1.0 + nrm(ks[6], (DEPTH, D_A), 0.02),
        'w_s': nrm(ks[7], (DEPTH, GMLP_GROUPS, GMLP_CHUNK, GMLP_CHUNK), GMLP_CHUNK ** -0.5),
        'b_s': 1.0 + nrm(ks[8], (DEPTH, GMLP_GROUPS, GMLP_CHUNK), 0.1),
        'q_norm_g': 1.0 + nrm(ks[9], (DEPTH, HEAD_DIM), 0.02),
        'k_norm_g': 1.0 + nrm(ks[10], (DEPTH, HEAD_DIM), 0.02),
        'sinks': nrm(ks[11], (DEPTH, N_HEADS), 1.0),
        'p_a': nrm(ks[12], (DEPTH, D_A, D_MODEL), D_A ** -0.5),
        'p_b': nrm(ks[13], (DEPTH, Q_DIM, D_MODEL), Q_DIM ** -0.5),
        'w_out': nrm(ks[14], (DEPTH, D_MODEL, D_MODEL), D_MODEL ** -0.5),
        'norm2_g': 1.0 + nrm(ks[15], (DEPTH, D_MODEL), 0.02),
        'w_rg': nrm(ks[16], (DEPTH, D_MODEL, N_GROUPS), D_MODEL ** -0.5),
        'b_rg': nrm(ks[17], (DEPTH, N_GROUPS), 0.01),
        'w_re': nrm(ks[18], (DEPTH, D_MODEL, N_GROUPS, EXPERTS_PER_GROUP), D_MODEL ** -0.5),
        'b_re': nrm(ks[19], (DEPTH, N_GROUPS, EXPERTS_PER_GROUP), 0.01),
        'w_gate_e': nrm(ks[20], (DEPTH, N_EXPERTS, D_MODEL, D_EXPERT), D_MODEL ** -0.5),
        'w_up_e': nrm(ks[21], (DEPTH, N_EXPERTS, D_MODEL, D_EXPERT), D_MODEL ** -0.5),
        'w_down_e': nrm(ks[22], (DEPTH, N_EXPERTS, D_EXPERT, D_MODEL), D_EXPERT ** -0.5),
    }


def reference(x_prompt, x_sample, cache_k_win, cache_v_win, norm1_g, w_in, gmlp_norm_g, w_s, b_s,
              q_norm_g, k_norm_g, sinks, p_a, p_b, w_out, norm2_g, w_rg, b_rg, w_re, b_re,
              w_gate_e, w_up_e, w_down_e):
    S = x_prompt.shape[1]
    T = x_sample.shape[1]
    pos_p = jnp.arange(S, dtype=jnp.int32)
    pos_s = PAST_LEN + jnp.arange(T, dtype=jnp.int32)
    keep = min(WINDOW, S)
    xp, xs = x_prompt, x_sample
    k_win_p, v_win_p, k_new_s, v_new_s, gv_s = [], [], [], [], []
    for l in range(DEPTH):
        u, vn, q, k, v, gl = pre_mix(xp, pos_p, norm1_g[l], w_in[l], gmlp_norm_g[l], q_norm_g[l], k_norm_g[l])
        o_a = gmlp_spatial(u, vn, w_s[l], b_s[l], GMLP_CHUNK)
        o_b = swa_prompt(q, k, v, sinks[l])
        xp = post_mix(xp, o_a, o_b, gl, p_a[l], p_b[l], w_out[l], norm2_g[l], w_rg[l], b_rg[l],
                      w_re[l], b_re[l], w_gate_e, w_up_e, w_down_e, l)
        k_win_p.append(k[:, S - keep:])
        v_win_p.append(v[:, S - keep:])
        u, vn, q, k, v, gl = pre_mix(xs, pos_s, norm1_g[l], w_in[l], gmlp_norm_g[l], q_norm_g[l], k_norm_g[l])
        o_a = gmlp_spatial(u, vn, w_s[l], b_s[l], T)
        o_b = swa_sample(q, k, v, cache_k_win[l], cache_v_win[l], sinks[l])
        xs = post_mix(xs, o_a, o_b, gl, p_a[l], p_b[l], w_out[l], norm2_g[l], w_rg[l], b_rg[l],
                      w_re[l], b_re[l], w_gate_e, w_up_e, w_down_e, l)
        k_new_s.append(k)
        v_new_s.append(v)
        gv_s.append(vn)
    return (xp, xs, jnp.stack(k_win_p), jnp.stack(v_win_p), jnp.stack(k_new_s), jnp.stack(v_new_s), jnp.stack(gv_s))
```

```python
import functools

import jax
import jax.numpy as jnp
from jax import lax
from jax.experimental import pallas as pl
from jax.experimental.pallas import tpu as pltpu

F32 = jnp.float32
BF16 = jnp.bfloat16
I32 = jnp.int32

EPS = 1e-6
PAST_LEN = 1024
CHUNK = 64
ROPE_THETA = 500000.0
TOP_K = 2
LANES = 128

VMEM_LIMIT_BYTES = 56 * 1024 * 1024

ROW_TILE = 1024
COL_TILE = 512
NORM_ROWS = 256
ROUTE_ROWS = 256
MOE_ROWS = 320
COMBINE_ROWS = 128
DMA_LAG = 16


def _cparams(sem):
    return pltpu.CompilerParams(dimension_semantics=sem, vmem_limit_bytes=VMEM_LIMIT_BYTES)


def _cast_rows(src_ref, dst_ref, rows):
    n = src_ref.shape[0] // rows

    def body(r, c):
        sl = pl.ds(pl.multiple_of(r * rows, rows), rows)
        dst_ref[sl, :] = src_ref[sl, :].astype(dst_ref.dtype)
        return c

    lax.fori_loop(0, n, body, 0)


def _rms(x, gain):
    ms = jnp.mean(x * x, axis=-1, keepdims=True)
    return x * lax.rsqrt(ms + EPS) * gain


def _norm1_kernel(xp_ref, xs_ref, g_ref, h_ref, *, n_prompt_blocks):
    i = pl.program_id(0)

    @pl.when(i < n_prompt_blocks)
    def _():
        h_ref[...] = _rms(xp_ref[...], g_ref[...]).astype(h_ref.dtype)

    @pl.when(i >= n_prompt_blocks)
    def _():
        h_ref[...] = _rms(xs_ref[...], g_ref[...]).astype(h_ref.dtype)


def _norm1(xp, xs, gain):
    s, d = xp.shape
    t = xs.shape[0]
    br = min(NORM_ROWS, t)
    nbp, nbs = s // br, t // br
    return pl.pallas_call(
        functools.partial(_norm1_kernel, n_prompt_blocks=nbp),
        grid=(nbp + nbs,),
        in_specs=[
            pl.BlockSpec((br, d), lambda i: (jnp.minimum(i, nbp - 1), 0)),
            pl.BlockSpec((br, d), lambda i: (jnp.maximum(i - nbp, 0), 0)),
            pl.BlockSpec((1, d), lambda i: (0, 0)),
        ],
        out_specs=pl.BlockSpec((br, d), lambda i: (i, 0)),
        out_shape=jax.ShapeDtypeStruct((s + t, d), BF16),
        compiler_params=_cparams(("arbitrary",)),
        name="norm1",
    )(xp, xs, gain)


def _inproj_act_kernel(h_ref, w_ref, o_ref, wb_ref, *, act):
    @pl.when(pl.program_id(1) == 0)
    def _():
        _cast_rows(w_ref, wb_ref, 256)

    z = jnp.dot(h_ref[...], wb_ref[...], preferred_element_type=F32)
    o_ref[...] = act(z).astype(o_ref.dtype)


def _inproj_act(h, w, col0, ncols, act, out_dtype, name):
    m, d = h.shape
    bm, bn = min(ROW_TILE, m), min(COL_TILE, ncols)
    assert col0 % bn == 0 and ncols % bn == 0 and m % bm == 0
    off = col0 // bn
    return pl.pallas_call(
        functools.partial(_inproj_act_kernel, act=act),
        grid=(ncols // bn, m // bm),
        in_specs=[
            pl.BlockSpec((bm, d), lambda n, i: (i, 0)),
            pl.BlockSpec((d, bn), lambda n, i: (0, n + off)),
        ],
        out_specs=pl.BlockSpec((bm, bn), lambda n, i: (i, n)),
        out_shape=jax.ShapeDtypeStruct((m, ncols), out_dtype),
        scratch_shapes=[pltpu.VMEM((d, bn), BF16)],
        compiler_params=_cparams(("arbitrary", "arbitrary")),
        name=name,
    )(h, w)


def _inproj_qkv_kernel(h_ref, w_ref, gain_ref, cos_ref, sa_ref, sb_ref, o_ref, wb_ref, *,
                       n_norm_tiles, head_dim, rot_half):
    n = pl.program_id(0)

    @pl.when(pl.program_id(1) == 0)
    def _():
        _cast_rows(w_ref, wb_ref, 256)

    z = jnp.dot(h_ref[...], wb_ref[...], preferred_element_type=F32)
    bn = z.shape[1]

    @pl.when(n < n_norm_tiles)
    def _():
        r = lax.broadcasted_iota(I32, (bn, bn), 0) // head_dim
        c = lax.broadcasted_iota(I32, (bn, bn), 1) // head_dim
        seg = (r == c).astype(BF16)
        zz = z * z
        hi = zz.astype(BF16)
        lo = (zz - hi.astype(F32)).astype(BF16)
        ssq = (jnp.dot(hi, seg, preferred_element_type=F32)
               + jnp.dot(lo, seg, preferred_element_type=F32))
        y = z * lax.rsqrt(ssq * (1.0 / head_dim) + EPS) * gain_ref[...]
        reps = bn // cos_ref.shape[1]
        cosv = jnp.tile(cos_ref[...], (1, reps))
        sa = jnp.tile(sa_ref[...], (1, reps))
        sb = jnp.tile(sb_ref[...], (1, reps))
        y = y * cosv + pltpu.roll(y, bn - rot_half, 1) * sa + pltpu.roll(y, rot_half, 1) * sb
        o_ref[...] = y

    @pl.when(n >= n_norm_tiles)
    def _():
        o_ref[...] = z


def _inproj_qkv(h, w, col0, q_dim, kv_dim, gain_row, cos_t, sa_t, sb_t, head_dim, rot_half):
    m, d = h.shape
    ncols = q_dim + 2 * kv_dim
    bm, bn = min(ROW_TILE, m), kv_dim
    assert col0 % bn == 0 and q_dim % bn == 0 and bn % LANES == 0 and m % bm == 0
    off = col0 // bn
    n_norm_tiles = (q_dim + kv_dim) // bn
    tw = cos_t.shape[1]
    return pl.pallas_call(
        functools.partial(_inproj_qkv_kernel, n_norm_tiles=n_norm_tiles, head_dim=head_dim,
                          rot_half=rot_half),
        grid=(ncols // bn, m // bm),
        in_specs=[
            pl.BlockSpec((bm, d), lambda n, i: (i, 0)),
            pl.BlockSpec((d, bn), lambda n, i: (0, n + off)),
            pl.BlockSpec((1, bn), lambda n, i: (0, n)),
            pl.BlockSpec((bm, tw), lambda n, i: (i, 0)),
            pl.BlockSpec((bm, tw), lambda n, i: (i, 0)),
            pl.BlockSpec((bm, tw), lambda n, i: (i, 0)),
        ],
        out_specs=pl.BlockSpec((bm, bn), lambda n, i: (i, n)),
        out_shape=jax.ShapeDtypeStruct((m, ncols), F32),
        scratch_shapes=[pltpu.VMEM((d, bn), BF16)],
        compiler_params=_cparams(("arbitrary", "arbitrary")),
        name="inproj_qkv",
    )(h, w, gain_row, cos_t, sa_t, sb_t)


def _gmlp_kernel(u_ref, va_ref, wp_ref, ws_ref, bp_ref, bs_ref, gain_ref, o_ref, vn_ref, *,
                 n_prompt_blocks, groups, sample_len):
    i = pl.program_id(0)
    rows, d_a = u_ref.shape
    gw = d_a // groups
    ri = lax.broadcasted_iota(I32, (rows, rows), 0)
    ci = lax.broadcasted_iota(I32, (rows, rows), 1)

    def run(w_ref, b_ref, sub, emit_vn):
        vn = _rms(va_ref[...].astype(F32), gain_ref[...])
        if emit_vn:
            vn_ref[...] = vn
        vb = vn.astype(BF16)
        mask = (ci <= ri) & ((ri // sub) == (ci // sub))
        for g in range(groups):
            sl = slice(g * gw, (g + 1) * gw)
            wg = jnp.where(mask, w_ref[g], 0.0).astype(BF16)
            s = jnp.dot(wg, vb[:, sl], preferred_element_type=F32) + b_ref[:, g:g + 1]
            o_ref[:, sl] = (u_ref[:, sl].astype(F32) * s).astype(o_ref.dtype)

    @pl.when(i < n_prompt_blocks)
    def _():
        run(wp_ref, bp_ref, rows, False)

    @pl.when(i >= n_prompt_blocks)
    def _():
        run(ws_ref, bs_ref, sample_len, True)


def _gmlp(ua, w_prompt, w_sample, b_prompt, b_sample, gain, n_prompt_rows, sample_len):
    m, two_da = ua.shape
    d_a = two_da // 2
    groups, rows, _ = w_prompt.shape
    nbp = n_prompt_rows // rows
    nb = m // rows
    full = lambda i: (0, 0, 0)
    return pl.pallas_call(
        functools.partial(_gmlp_kernel, n_prompt_blocks=nbp, groups=groups, sample_len=sample_len),
        grid=(nb,),
        in_specs=[
            pl.BlockSpec((rows, d_a), lambda i: (i, 0)),
            pl.BlockSpec((rows, d_a), lambda i: (i, 1)),
            pl.BlockSpec((groups, rows, rows), full),
            pl.BlockSpec((groups, rows, rows), full),
            pl.BlockSpec((rows, groups), lambda i: (0, 0)),
            pl.BlockSpec((rows, groups), lambda i: (0, 0)),
            pl.BlockSpec((1, d_a), lambda i: (0, 0)),
        ],
        out_specs=[
            pl.BlockSpec((rows, d_a), lambda i: (i, 0)),
            pl.BlockSpec((rows, d_a), lambda i: (jnp.maximum(i - nbp, 0), 0)),
        ],
        out_shape=[
            jax.ShapeDtypeStruct((m, d_a), BF16),
            jax.ShapeDtypeStruct((m - n_prompt_rows, d_a), F32),
        ],
        compiler_params=_cparams(("arbitrary",)),
        name="gmlp_spatial",
    )(ua, ua, w_prompt, w_sample, b_prompt, b_sample, gain)


def _attn_kernel(sink_ref, q_ref, k0_ref, k1_ref, k2_ref, v0_ref, v1_ref, v2_ref, ck_ref, cv_ref,
                 o_ref, *, n_prompt_chunks, n_kv_heads, gqa, head_dim):
    c = pl.program_id(0)
    t = q_ref.shape[0]
    scale = head_dim ** -0.5

    def run(k_parts, v_parts, first_key_chunk):
        k = jnp.concatenate(k_parts, axis=0)
        v = jnp.concatenate(v_parts, axis=0)
        lk = k.shape[0]
        if first_key_chunk is not None:
            key_chunk = first_key_chunk + lax.broadcasted_iota(I32, (1, lk), 1) // t
            valid = key_chunk >= 0
        for j in range(n_kv_heads):
            kj = k[:, j * head_dim:(j + 1) * head_dim].astype(BF16)
            vj = v[:, j * head_dim:(j + 1) * head_dim].astype(BF16)
            heads = [j * gqa + g for g in range(gqa)]
            qs = jnp.concatenate(
                [q_ref[:, h * head_dim:(h + 1) * head_dim] for h in heads], axis=0).astype(BF16)
            s = lax.dot_general(qs, kj, (((1,), (1,)), ((), ())),
                                preferred_element_type=F32) * scale
            if first_key_chunk is not None:
                s = jnp.where(valid, s, -jnp.inf)
            sink = jnp.concatenate([jnp.full((t, 1), sink_ref[0, h], F32) for h in heads], axis=0)
            mx = jnp.maximum(jnp.max(s, axis=-1, keepdims=True), sink)
            e = jnp.exp(s - mx)
            p = e / (jnp.sum(e, axis=-1, keepdims=True) + jnp.exp(sink - mx))
            o = jnp.dot(p.astype(BF16), vj, preferred_element_type=F32)
            for g, h in enumerate(heads):
                o_ref[:, h * head_dim:(h + 1) * head_dim] = o[g * t:(g + 1) * t].astype(o_ref.dtype)

    @pl.when(c < n_prompt_chunks)
    def _():
        run([k0_ref[...], k1_ref[...], k2_ref[...]], [v0_ref[...], v1_ref[...], v2_ref[...]], c - 2)

    @pl.when(c >= n_prompt_chunks)
    def _():
        run([ck_ref[...], k2_ref[...]], [cv_ref[...], v2_ref[...]], None)


def _attention(qkv, cache_k, cache_v, sinks, n_prompt_rows, q_dim, kv_dim, head_dim, n_heads):
    m = qkv.shape[0]
    t = CHUNK
    nc = n_prompt_rows // t
    n_kv_heads = kv_dim // head_dim
    window = cache_k.shape[0] // ((m - n_prompt_rows) // t)
    kcol, vcol = q_dim // kv_dim, q_dim // kv_dim + 1

    def kv_spec(back, col):
        return pl.BlockSpec((t, kv_dim), lambda c: (jnp.maximum(c - back, 0), col))

    cache_spec = pl.BlockSpec((window, kv_dim), lambda c: (jnp.maximum(c - nc, 0), 0))
    return pl.pallas_call(
        functools.partial(_attn_kernel, n_prompt_chunks=nc, n_kv_heads=n_kv_heads,
                          gqa=n_heads // n_kv_heads, head_dim=head_dim),
        grid=(m // t,),
        in_specs=[
            pl.BlockSpec(memory_space=pltpu.SMEM),
            pl.BlockSpec((t, q_dim), lambda c: (c, 0)),
            kv_spec(2, kcol), kv_spec(1, kcol), kv_spec(0, kcol),
            kv_spec(2, vcol), kv_spec(1, vcol), kv_spec(0, vcol),
            cache_spec, cache_spec,
        ],
        out_specs=pl.BlockSpec((t, q_dim), lambda c: (c, 0)),
        out_shape=jax.ShapeDtypeStruct((m, q_dim), BF16),
        compiler_params=_cparams(("arbitrary",)),
        name="window_attention",
    )(sinks, qkv, qkv, qkv, qkv, qkv, qkv, qkv, cache_k, cache_v)


def _merge_kernel(oa_ref, ob_ref, pa_ref, pb_ref, ga_ref, gb_ref, o_ref, pab_ref, pbb_ref):
    @pl.when(pl.program_id(1) == 0)
    def _():
        _cast_rows(pa_ref, pab_ref, 256)
        _cast_rows(pb_ref, pbb_ref, 256)

    a = jnp.dot(oa_ref[...], pab_ref[...], preferred_element_type=F32)
    b = jnp.dot(ob_ref[...], pbb_ref[...], preferred_element_type=F32)
    o_ref[...] = (ga_ref[...].astype(F32) * a + gb_ref[...].astype(F32) * b).astype(o_ref.dtype)


def _merge(o_a, o_b, p_a, p_b, gates):
    m, d_a = o_a.shape
    q_dim = o_b.shape[1]
    d = p_a.shape[1]
    bm, bn = min(ROW_TILE, m), min(COL_TILE, d)
    goff = d // bn
    return pl.pallas_call(
        _merge_kernel,
        grid=(d // bn, m // bm),
        in_specs=[
            pl.BlockSpec((bm, d_a), lambda n, i: (i, 0)),
            pl.BlockSpec((bm, q_dim), lambda n, i: (i, 0)),
            pl.BlockSpec((d_a, bn), lambda n, i: (0, n)),
            pl.BlockSpec((q_dim, bn), lambda n, i: (0, n)),
            pl.BlockSpec((bm, bn), lambda n, i: (i, n)),
            pl.BlockSpec((bm, bn), lambda n, i: (i, n + goff)),
        ],
        out_specs=pl.BlockSpec((bm, bn), lambda n, i: (i, n)),
        out_shape=jax.ShapeDtypeStruct((m, d), BF16),
        scratch_shapes=[pltpu.VMEM((d_a, bn), BF16), pltpu.VMEM((q_dim, bn), BF16)],
        compiler_params=_cparams(("arbitrary", "arbitrary")),
        name="merge_proj",
    )(o_a, o_b, p_a, p_b, gates, gates)


def _outproj_kernel(t_ref, w_ref, xp_ref, xs_ref, o_ref, wb_ref, *, n_prompt_tiles):
    i = pl.program_id(1)

    @pl.when(i == 0)
    def _():
        _cast_rows(w_ref, wb_ref, 256)

    z = jnp.dot(t_ref[...], wb_ref[...], preferred_element_type=F32)

    @pl.when(i < n_prompt_tiles)
    def _():
        o_ref[...] = xp_ref[...] + z

    @pl.when(i >= n_prompt_tiles)
    def _():
        o_ref[...] = xs_ref[...] + z


def _outproj(tm, w_out, xp, xs):
    m, d = tm.shape
    s, t = xp.shape[0], xs.shape[0]
    bm, bn = min(ROW_TILE, t), min(COL_TILE, d)
    npt = s // bm
    return pl.pallas_call(
        functools.partial(_outproj_kernel, n_prompt_tiles=npt),
        grid=(d // bn, m // bm),
        in_specs=[
            pl.BlockSpec((bm, d), lambda n, i: (i, 0)),
            pl.BlockSpec((d, bn), lambda n, i: (0, n)),
            pl.BlockSpec((bm, bn), lambda n, i: (jnp.minimum(i, npt - 1), n)),
            pl.BlockSpec((bm, bn), lambda n, i: (jnp.maximum(i - npt, 0), n)),
        ],
        out_specs=pl.BlockSpec((bm, bn), lambda n, i: (i, n)),
        out_shape=jax.ShapeDtypeStruct((m, d), F32),
        scratch_shapes=[pltpu.VMEM((d, bn), BF16)],
        compiler_params=_cparams(("arbitrary", "arbitrary")),
        name="out_proj",
    )(tm, w_out, xp, xs)


def _router_kernel(x_ref, g_ref, wr_ref, br_ref, h_ref, route_ref, cnt_ref, whi_ref, wlo_ref,
                   carry_ref, *, n_groups, per_group):
    i = pl.program_id(0)

    @pl.when(i == 0)
    def _():
        w = wr_ref[...]
        hi = w.astype(BF16)
        whi_ref[...] = hi
        wlo_ref[...] = (w - hi.astype(F32)).astype(BF16)
        carry_ref[...] = jnp.zeros_like(carry_ref)

    hn = _rms(x_ref[...], g_ref[...])
    h_ref[...] = hn
    rows = hn.shape[0]
    hi = hn.astype(BF16)
    lo = (hn - hi.astype(F32)).astype(BF16)
    logits = (jnp.dot(hi, whi_ref[...], preferred_element_type=F32)
              + jnp.dot(hi, wlo_ref[...], preferred_element_type=F32)
              + jnp.dot(lo, whi_ref[...], preferred_element_type=F32)) + br_ref[...]
    lane = lax.broadcasted_iota(I32, logits.shape, 1)
    big = jnp.int32(LANES)

    lg = jnp.where(lane < n_groups, logits, -jnp.inf)
    mg = jnp.max(lg, axis=-1, keepdims=True)
    pg_sel = 1.0 / jnp.sum(jnp.exp(lg - mg), axis=-1, keepdims=True)
    gsel = jnp.min(jnp.where(lg == mg, lane, big), axis=-1, keepdims=True)

    first = n_groups + gsel * per_group
    emask = (lane >= first) & (lane < first + per_group)
    le = jnp.where(emask, logits, -jnp.inf)
    me = jnp.max(le, axis=-1, keepdims=True)
    ee = jnp.exp(le - me)
    pe = jnp.where(emask, ee / jnp.sum(ee, axis=-1, keepdims=True), -1.0)
    p1 = jnp.max(pe, axis=-1, keepdims=True)
    i1 = jnp.min(jnp.where(pe == p1, lane, big), axis=-1, keepdims=True)
    pe2 = jnp.where(lane == i1, -1.0, pe)
    p2 = jnp.max(pe2, axis=-1, keepdims=True)
    i2 = jnp.min(jnp.where(pe2 == p2, lane, big), axis=-1, keepdims=True)
    psum = p1 + p2
    w1 = p1 / psum * pg_sel
    w2 = p2 / psum * pg_sel
    e1 = i1 - n_groups
    e2 = i2 - n_groups

    oh1 = (lane == e1).astype(F32)
    oh2 = (lane == e2).astype(F32)
    ohs = oh1 + oh2
    ri = lax.broadcasted_iota(I32, (rows, rows), 0)
    ci = lax.broadcasted_iota(I32, (rows, rows), 1)
    below = (ci < ri).astype(BF16)
    before = jnp.dot(below, ohs.astype(BF16), preferred_element_type=F32) + carry_ref[...]
    r1 = jnp.sum(before * oh1, axis=-1, keepdims=True)
    r2 = jnp.sum(before * oh2, axis=-1, keepdims=True)
    carry_ref[...] = carry_ref[...] + jnp.sum(ohs, axis=0, keepdims=True)
    cnt_ref[...] = carry_ref[...]

    route = jnp.where(lane == 0, e1.astype(F32), 0.0)
    route = jnp.where(lane == 1, e2.astype(F32), route)
    route = jnp.where(lane == 2, w1, route)
    route = jnp.where(lane == 3, w2, route)
    route = jnp.where(lane == 4, r1, route)
    route = jnp.where(lane == 5, r2, route)
    route_ref[...] = route


def _router(x2, gain, wr, br, n_groups, per_group):
    m, d = x2.shape
    br_rows = min(ROUTE_ROWS, m)
    return pl.pallas_call(
        functools.partial(_router_kernel, n_groups=n_groups, per_group=per_group),
        grid=(m // br_rows,),
        in_specs=[
            pl.BlockSpec((br_rows, d), lambda i: (i, 0)),
            pl.BlockSpec((1, d), lambda i: (0, 0)),
            pl.BlockSpec((d, LANES), lambda i: (0, 0)),
            pl.BlockSpec((1, LANES), lambda i: (0, 0)),
        ],
        out_specs=[
            pl.BlockSpec((br_rows, d), lambda i: (i, 0)),
            pl.BlockSpec((br_rows, LANES), lambda i: (i, 0)),
            pl.BlockSpec((1, LANES), lambda i: (0, 0)),
        ],
        out_shape=[
            jax.ShapeDtypeStruct((m, d), F32),
            jax.ShapeDtypeStruct((m, LANES), F32),
            jax.ShapeDtypeStruct((1, LANES), F32),
        ],
        scratch_shapes=[pltpu.VMEM((d, LANES), BF16), pltpu.VMEM((d, LANES), BF16),
                        pltpu.VMEM((1, LANES), F32)],
        compiler_params=_cparams(("arbitrary",)),
        name="norm2_router",
    )(x2, gain, wr, br)


def _dispatch_kernel(dest_ref, h_ref, xs_ref, sem, *, n_tokens):
    def row_copy(n, k):
        d = dest_ref[TOP_K * n + k]
        return pltpu.make_async_copy(h_ref.at[pl.ds(n, 1)], xs_ref.at[pl.ds(d, 1)], sem)

    def wait_token():
        for k in range(TOP_K):
            pltpu.make_async_copy(h_ref.at[pl.ds(0, 1)], xs_ref.at[pl.ds(0, 1)], sem).wait()

    def body(n, c):
        for k in range(TOP_K):
            row_copy(n, k).start()

        @pl.when(n >= DMA_LAG)
        def _():
            wait_token()

        return c

    lax.fori_loop(0, n_tokens, body, 0)

    def drain(n, c):
        wait_token()
        return c

    lax.fori_loop(0, min(DMA_LAG, n_tokens), drain, 0)


def _dispatch(dest_flat, h2, n_rows):
    m, d = h2.shape
    return pl.pallas_call(
        functools.partial(_dispatch_kernel, n_tokens=m),
        grid_spec=pltpu.PrefetchScalarGridSpec(
            num_scalar_prefetch=1,
            grid=(1,),
            in_specs=[pl.BlockSpec(memory_space=pl.ANY)],
            out_specs=pl.BlockSpec(memory_space=pl.ANY),
            scratch_shapes=[pltpu.SemaphoreType.DMA],
        ),
        out_shape=jax.ShapeDtypeStruct((n_rows, d), F32),
        compiler_params=_cparams(("arbitrary",)),
        name="moe_dispatch",
    )(dest_flat, h2)


def _moe_kernel(be_ref, bidx_ref, nrows_ref, nv_ref, xs_ref, wg_ref, wu_ref, wd_ref, y_ref):
    i = pl.program_id(0)
    j = pl.program_id(1)

    @pl.when(i < nv_ref[0])
    def _():
        rows = xs_ref.shape[0]
        live = lax.broadcasted_iota(I32, (rows, 1), 0) < nrows_ref[i]
        x = jnp.where(live, xs_ref[...], 0.0).astype(BF16)
        a = jnp.dot(x, wg_ref[0].astype(BF16), preferred_element_type=F32)
        b = jnp.dot(x, wu_ref[0].astype(BF16), preferred_element_type=F32)
        act = (jax.nn.silu(a) * b).astype(BF16)
        part = jnp.dot(act, wd_ref[0].astype(BF16), preferred_element_type=F32)

        @pl.when(j == 0)
        def _():
            y_ref[...] = part

        @pl.when(j != 0)
        def _():
            y_ref[...] = y_ref[...] + part


def _moe_experts(block_e, block_idx, block_rows, n_valid, xs, w_gate, w_up, w_down, n_blocks):
    d = xs.shape[1]
    d_e = w_gate.shape[2]
    n_split = 2
    dh = d_e // n_split
    br = MOE_ROWS

    def col(i, j, nv):
        return jnp.where(i < nv[0], j, n_split - 1)

    return pl.pallas_call(
        _moe_kernel,
        grid_spec=pltpu.PrefetchScalarGridSpec(
            num_scalar_prefetch=4,
            grid=(n_blocks, n_split),
            in_specs=[
                pl.BlockSpec((br, d), lambda i, j, be, bi, nr, nv: (bi[i], 0)),
                pl.BlockSpec((1, d, dh), lambda i, j, be, bi, nr, nv: (be[i], 0, col(i, j, nv))),
                pl.BlockSpec((1, d, dh), lambda i, j, be, bi, nr, nv: (be[i], 0, col(i, j, nv))),
                pl.BlockSpec((1, dh, d), lambda i, j, be, bi, nr, nv: (be[i], col(i, j, nv), 0)),
            ],
            out_specs=pl.BlockSpec((br, d), lambda i, j, be, bi, nr, nv: (bi[i], 0)),
        ),
        out_shape=jax.ShapeDtypeStruct(xs.shape, F32),
        compiler_params=_cparams(("arbitrary", "arbitrary")),
        name="moe_experts",
    )(block_e, block_idx, block_rows, n_valid, xs, w_gate, w_up, w_down)


def _combine_kernel(dest_ref, x_ref, route_ref, y_ref, op_ref, os_ref, buf_ref, sem, *, n_prompt_tiles):
    i = pl.program_id(0)
    tm = x_ref.shape[0]

    def row_copy(r, k):
        d = dest_ref[TOP_K * (i * tm + r) + k]
        return pltpu.make_async_copy(y_ref.at[pl.ds(d, 1)], buf_ref.at[k, pl.ds(r, 1)], sem)

    def start(r, c):
        for k in range(TOP_K):
            row_copy(r, k).start()
        return c

    lax.fori_loop(0, tm, start, 0)

    def wait(r, c):
        for k in range(TOP_K):
            pltpu.make_async_copy(y_ref.at[pl.ds(0, 1)], buf_ref.at[k, pl.ds(0, 1)], sem).wait()
        return c

    lax.fori_loop(0, tm, wait, 0)

    out = x_ref[...]
    for k in range(TOP_K):
        out = out + route_ref[:, 2 + k:3 + k] * buf_ref[k]

    @pl.when(i < n_prompt_tiles)
    def _():
        op_ref[...] = out

    @pl.when(i >= n_prompt_tiles)
    def _():
        os_ref[...] = out


def _combine(dest_flat, x2, route, y, n_prompt_rows):
    m, d = x2.shape
    tm = COMBINE_ROWS
    npt = n_prompt_rows // tm
    return pl.pallas_call(
        functools.partial(_combine_kernel, n_prompt_tiles=npt),
        grid_spec=pltpu.PrefetchScalarGridSpec(
            num_scalar_prefetch=1,
            grid=(m // tm,),
            in_specs=[
                pl.BlockSpec((tm, d), lambda i, dest: (i, 0)),
                pl.BlockSpec((tm, LANES), lambda i, dest: (i, 0)),
                pl.BlockSpec(memory_space=pl.ANY),
            ],
            out_specs=[
                pl.BlockSpec((tm, d), lambda i, dest: (jnp.minimum(i, npt - 1), 0)),
                pl.BlockSpec((tm, d), lambda i, dest: (jnp.maximum(i - npt, 0), 0)),
            ],
            scratch_shapes=[pltpu.VMEM((TOP_K, tm, d), F32), pltpu.SemaphoreType.DMA],
        ),
        out_shape=[
            jax.ShapeDtypeStruct((n_prompt_rows, d), F32),
            jax.ShapeDtypeStruct((m - n_prompt_rows, d), F32),
        ],
        compiler_params=_cparams(("arbitrary",)),
        name="moe_combine",
    )(dest_flat, x2, route, y)


def _rope_tables(positions, head_dim):
    rot_dim = head_dim // 4
    half = rot_dim // 2
    inv_freq = jnp.power(ROPE_THETA, -jnp.arange(half, dtype=F32) * 2.0 / rot_dim)
    ang = positions.astype(F32)[:, None] * inv_freq[None, :]
    cos, sin = jnp.cos(ang), jnp.sin(ang)
    m = positions.shape[0]
    zeros = lambda n: jnp.zeros((m, n), F32)
    cos_h = jnp.concatenate([cos, cos, jnp.ones((m, head_dim - rot_dim), F32)], axis=1)
    sa_h = jnp.concatenate([-sin, zeros(head_dim - half)], axis=1)
    sb_h = jnp.concatenate([zeros(half), sin, zeros(head_dim - rot_dim)], axis=1)
    reps = LANES // head_dim
    return tuple(jnp.tile(a, (1, reps)) for a in (cos_h, sa_h, sb_h)), half


def _layer(xp, xs, cache_k, cache_v, norm1_g, w_in, gmlp_norm_g, w_s, b_s, q_norm_g, k_norm_g,
           sinks, p_a, p_b, w_out, norm2_g, w_rg, b_rg, w_re, b_re, w_gate_e, w_up_e, w_down_e):
    s, d = xp.shape
    dec_rows = xs.shape[0]
    dec_batch = cache_k.shape[0]
    t = dec_rows // dec_batch
    m = s + dec_rows
    d_a = gmlp_norm_g.shape[0]
    head_dim = q_norm_g.shape[0]
    n_heads = sinks.shape[0]
    q_dim = n_heads * head_dim
    kv_dim = cache_k.shape[2] * cache_k.shape[3]
    n_groups, per_group = w_re.shape[1], w_re.shape[2]
    n_experts = n_groups * per_group

    h = _norm1(xp, xs, norm1_g[None, :])
    ua = _inproj_act(h, w_in, 0, 2 * d_a, jax.nn.gelu, BF16, "inproj_gelu")
    positions = jnp.concatenate(
        [jnp.arange(s, dtype=I32), jnp.tile(PAST_LEN + jnp.arange(t, dtype=I32), dec_batch)])
    (cos_t, sa_t, sb_t), rot_half = _rope_tables(positions, head_dim)
    gain_row = jnp.concatenate([jnp.tile(q_norm_g, n_heads), jnp.tile(k_norm_g, kv_dim // head_dim),
                                jnp.ones((kv_dim,), F32)])[None, :]
    qkv = _inproj_qkv(h, w_in, 2 * d_a, q_dim, kv_dim, gain_row, cos_t, sa_t, sb_t, head_dim, rot_half)
    gates = _inproj_act(h, w_in, 2 * d_a + q_dim + 2 * kv_dim, 2 * d, jax.nn.sigmoid, BF16,
                        "inproj_gate")

    rows = w_s.shape[1]
    reps = rows // t
    w_sample = jnp.tile(w_s[:, :t, :t], (1, reps, reps))
    b_sample = jnp.tile(b_s[:, :t], (1, reps))
    o_a, vn_s = _gmlp(ua, w_s, w_sample, b_s.T, b_sample.T, gmlp_norm_g[None, :], s, t)
    o_b = _attention(qkv, cache_k.reshape(-1, kv_dim), cache_v.reshape(-1, kv_dim), sinks[None, :],
                     s, q_dim, kv_dim, head_dim, n_heads)

    merged = _merge(o_a, o_b, p_a, p_b, gates)
    x2 = _outproj(merged, w_out, xp, xs)

    pad = LANES - n_groups - n_experts
    wr = jnp.concatenate([w_rg, w_re.reshape(d, n_experts), jnp.zeros((d, pad), F32)], axis=1)
    br = jnp.concatenate([b_rg, b_re.reshape(n_experts), jnp.zeros((pad,), F32)])[None, :]
    h2, route, counts = _router(x2, norm2_g[None, :], wr, br, n_groups, per_group)
    e_idx = route[:, 0:TOP_K].astype(I32)
    rank = route[:, 4:4 + TOP_K].astype(I32)
    counts = counts[0, :n_experts].astype(I32)

    blk = MOE_ROWS
    n_blocks = -(-(m * TOP_K) // blk) + n_experts
    padded = ((counts + blk - 1) // blk) * blk
    pend = jnp.cumsum(padded)
    pstart = pend - padded
    dest = (pstart[e_idx] + rank).reshape(-1).astype(I32)
    n_valid = (pend[-1] // blk).astype(I32)
    ids = jnp.arange(n_blocks, dtype=I32)
    block_idx = jnp.minimum(ids, n_valid - 1)
    block_e = jnp.minimum(jnp.searchsorted(pend, block_idx * blk, side="right"), n_experts - 1).astype(I32)
    block_rows = jnp.clip(counts[block_e] - (block_idx * blk - pstart[block_e]), 0, blk).astype(I32)

    xs_sorted = _dispatch(dest, h2, n_blocks * blk)
    y_sorted = _moe_experts(block_e, block_idx, block_rows, n_valid[None], xs_sorted,
                            w_gate_e, w_up_e, w_down_e, n_blocks)
    yp, ys = _combine(dest, x2, route, y_sorted, s)
    return yp, ys, qkv, vn_s


def kernel(x_prompt, x_sample, cache_k_win, cache_v_win, norm1_g, w_in, gmlp_norm_g, w_s, b_s,
           q_norm_g, k_norm_g, sinks, p_a, p_b, w_out, norm2_g, w_rg, b_rg, w_re, b_re,
           w_gate_e, w_up_e, w_down_e):
    depth = norm1_g.shape[0]
    assert depth == 1, "weights of one layer are expected"
    batch, s, d = x_prompt.shape
    assert batch == 1
    dec_batch, t, _ = x_sample.shape
    head_dim = q_norm_g.shape[-1]
    n_kv = cache_k_win.shape[3]
    q_dim = sinks.shape[-1] * head_dim
    kv_dim = n_kv * head_dim
    keep = min(cache_k_win.shape[2], s)

    l = 0
    yp, ys, qkv, vn_s = _layer(
        x_prompt.reshape(s, d), x_sample.reshape(dec_batch * t, d), cache_k_win[l], cache_v_win[l],
        norm1_g[l], w_in[l], gmlp_norm_g[l], w_s[l], b_s[l], q_norm_g[l], k_norm_g[l], sinks[l],
        p_a[l], p_b[l], w_out[l], norm2_g[l], w_rg[l], b_rg[l], w_re[l], b_re[l],
        w_gate_e[l], w_up_e[l], w_down_e[l])

    k_all = qkv[:, q_dim:q_dim + kv_dim]
    v_all = qkv[:, q_dim + kv_dim:]
    k_win_p = k_all[s - keep:s].reshape(1, batch, keep, n_kv, head_dim)
    v_win_p = v_all[s - keep:s].reshape(1, batch, keep, n_kv, head_dim)
    k_new_s = k_all[s:].reshape(1, dec_batch, t, n_kv, head_dim)
    v_new_s = v_all[s:].reshape(1, dec_batch, t, n_kv, head_dim)
    gv_s = vn_s.reshape(1, dec_batch, t, -1)
    return (yp.reshape(batch, s, d), ys.reshape(dec_batch, t, d), k_win_p, v_win_p, k_new_s, v_new_s, gv_s)
```

```python
import functools

import jax
import jax.numpy as jnp
from jax import lax
from jax.experimental import pallas as pl
from jax.experimental.pallas import tpu as pltpu

F32 = jnp.float32
BF16 = jnp.bfloat16
I32 = jnp.int32

EPS = 1e-6
PAST_LEN = 1024
CHUNK = 64
ROPE_THETA = 500000.0
TOP_K = 2
LANES = 128

VMEM_LIMIT_BYTES = 56 * 1024 * 1024

ROW_TILE = 1024
COL_TILE = 512
NORM_ROWS = 256
ROUTE_ROWS = 256
MOE_ROWS = 320
COMBINE_ROWS = 128
DISPATCH_ROWS = 256
DMA_UNROLL = 8


def _cparams(sem):
    return pltpu.CompilerParams(dimension_semantics=sem, vmem_limit_bytes=VMEM_LIMIT_BYTES)


def _cast_rows(src_ref, dst_ref, rows):
    n = src_ref.shape[0] // rows

    def body(r, c):
        sl = pl.ds(pl.multiple_of(r * rows, rows), rows)
        dst_ref[sl, :] = src_ref[sl, :].astype(dst_ref.dtype)
        return c

    lax.fori_loop(0, n, body, 0)


def _rms(x, gain):
    ms = jnp.mean(x * x, axis=-1, keepdims=True)
    return x * lax.rsqrt(ms + EPS) * gain


def _norm1_kernel(xp_ref, xs_ref, g_ref, h_ref, *, n_prompt_blocks):
    i = pl.program_id(0)

    @pl.when(i < n_prompt_blocks)
    def _():
        h_ref[...] = _rms(xp_ref[...], g_ref[...]).astype(h_ref.dtype)

    @pl.when(i >= n_prompt_blocks)
    def _():
        h_ref[...] = _rms(xs_ref[...], g_ref[...]).astype(h_ref.dtype)


def _norm1(xp, xs, gain):
    s, d = xp.shape
    t = xs.shape[0]
    br = min(NORM_ROWS, t)
    nbp, nbs = s // br, t // br
    return pl.pallas_call(
        functools.partial(_norm1_kernel, n_prompt_blocks=nbp),
        grid=(nbp + nbs,),
        in_specs=[
            pl.BlockSpec((br, d), lambda i: (jnp.minimum(i, nbp - 1), 0)),
            pl.BlockSpec((br, d), lambda i: (jnp.maximum(i - nbp, 0), 0)),
            pl.BlockSpec((1, d), lambda i: (0, 0)),
        ],
        out_specs=pl.BlockSpec((br, d), lambda i: (i, 0)),
        out_shape=jax.ShapeDtypeStruct((s + t, d), BF16),
        compiler_params=_cparams(("arbitrary",)),
        name="norm1",
    )(xp, xs, gain)


def _inproj_act_kernel(h_ref, w_ref, o_ref, wb_ref, *, act):
    @pl.when(pl.program_id(1) == 0)
    def _():
        _cast_rows(w_ref, wb_ref, 256)

    z = jnp.dot(h_ref[...], wb_ref[...], preferred_element_type=F32)
    o_ref[...] = act(z).astype(o_ref.dtype)


def _inproj_act(h, w, col0, ncols, act, out_dtype, name):
    m, d = h.shape
    bm, bn = min(ROW_TILE, m), min(COL_TILE, ncols)
    assert col0 % bn == 0 and ncols % bn == 0 and m % bm == 0
    off = col0 // bn
    return pl.pallas_call(
        functools.partial(_inproj_act_kernel, act=act),
        grid=(ncols // bn, m // bm),
        in_specs=[
            pl.BlockSpec((bm, d), lambda n, i: (i, 0)),
            pl.BlockSpec((d, bn), lambda n, i: (0, n + off)),
        ],
        out_specs=pl.BlockSpec((bm, bn), lambda n, i: (i, n)),
        out_shape=jax.ShapeDtypeStruct((m, ncols), out_dtype),
        scratch_shapes=[pltpu.VMEM((d, bn), BF16)],
        compiler_params=_cparams(("arbitrary", "arbitrary")),
        name=name,
    )(h, w)


def _inproj_qkv_kernel(h_ref, w_ref, gain_ref, cos_ref, sa_ref, sb_ref, o_ref, wb_ref, *,
                       n_norm_tiles, head_dim, rot_half):
    n = pl.program_id(0)

    @pl.when(pl.program_id(1) == 0)
    def _():
        _cast_rows(w_ref, wb_ref, 256)

    z = jnp.dot(h_ref[...], wb_ref[...], preferred_element_type=F32)
    bn = z.shape[1]

    @pl.when(n < n_norm_tiles)
    def _():
        r = lax.broadcasted_iota(I32, (bn, bn), 0) // head_dim
        c = lax.broadcasted_iota(I32, (bn, bn), 1) // head_dim
        seg = (r == c).astype(BF16)
        zz = z * z
        hi = zz.astype(BF16)
        lo = (zz - hi.astype(F32)).astype(BF16)
        ssq = (jnp.dot(hi, seg, preferred_element_type=F32)
               + jnp.dot(lo, seg, preferred_element_type=F32))
        y = z * lax.rsqrt(ssq * (1.0 / head_dim) + EPS) * gain_ref[...]
        reps = bn // cos_ref.shape[1]
        cosv = jnp.tile(cos_ref[...], (1, reps))
        sa = jnp.tile(sa_ref[...], (1, reps))
        sb = jnp.tile(sb_ref[...], (1, reps))
        y = y * cosv + pltpu.roll(y, bn - rot_half, 1) * sa + pltpu.roll(y, rot_half, 1) * sb
        o_ref[...] = y

    @pl.when(n >= n_norm_tiles)
    def _():
        o_ref[...] = z


def _inproj_qkv(h, w, col0, q_dim, kv_dim, gain_row, cos_t, sa_t, sb_t, head_dim, rot_half):
    m, d = h.shape
    ncols = q_dim + 2 * kv_dim
    bm, bn = min(ROW_TILE, m), kv_dim
    assert col0 % bn == 0 and q_dim % bn == 0 and bn % LANES == 0 and m % bm == 0
    off = col0 // bn
    n_norm_tiles = (q_dim + kv_dim) // bn
    tw = cos_t.shape[1]
    return pl.pallas_call(
        functools.partial(_inproj_qkv_kernel, n_norm_tiles=n_norm_tiles, head_dim=head_dim,
                          rot_half=rot_half),
        grid=(ncols // bn, m // bm),
        in_specs=[
            pl.BlockSpec((bm, d), lambda n, i: (i, 0)),
            pl.BlockSpec((d, bn), lambda n, i: (0, n + off)),
            pl.BlockSpec((1, bn), lambda n, i: (0, n)),
            pl.BlockSpec((bm, tw), lambda n, i: (i, 0)),
            pl.BlockSpec((bm, tw), lambda n, i: (i, 0)),
            pl.BlockSpec((bm, tw), lambda n, i: (i, 0)),
        ],
        out_specs=pl.BlockSpec((bm, bn), lambda n, i: (i, n)),
        out_shape=jax.ShapeDtypeStruct((m, ncols), F32),
        scratch_shapes=[pltpu.VMEM((d, bn), BF16)],
        compiler_params=_cparams(("arbitrary", "arbitrary")),
        name="inproj_qkv",
    )(h, w, gain_row, cos_t, sa_t, sb_t)


def _gmlp_kernel(u_ref, va_ref, wp_ref, ws_ref, bp_ref, bs_ref, gain_ref, o_ref, vn_ref, *,
                 n_prompt_blocks, groups, sample_len):
    i = pl.program_id(0)
    rows, d_a = u_ref.shape
    gw = d_a // groups
    ri = lax.broadcasted_iota(I32, (rows, rows), 0)
    ci = lax.broadcasted_iota(I32, (rows, rows), 1)

    def run(w_ref, b_ref, sub, emit_vn):
        vn = _rms(va_ref[...].astype(F32), gain_ref[...])
        if emit_vn:
            vn_ref[...] = vn
        vb = vn.astype(BF16)
        mask = (ci <= ri) & ((ri // sub) == (ci // sub))
        for g in range(groups):
            sl = slice(g * gw, (g + 1) * gw)
            wg = jnp.where(mask, w_ref[g], 0.0).astype(BF16)
            s = jnp.dot(wg, vb[:, sl], preferred_element_type=F32) + b_ref[:, g:g + 1]
            o_ref[:, sl] = (u_ref[:, sl].astype(F32) * s).astype(o_ref.dtype)

    @pl.when(i < n_prompt_blocks)
    def _():
        run(wp_ref, bp_ref, rows, False)

    @pl.when(i >= n_prompt_blocks)
    def _():
        run(ws_ref, bs_ref, sample_len, True)


def _gmlp(ua, w_prompt, w_sample, b_prompt, b_sample, gain, n_prompt_rows, sample_len):
    m, two_da = ua.shape
    d_a = two_da // 2
    groups, rows, _ = w_prompt.shape
    nbp = n_prompt_rows // rows
    nb = m // rows
    full = lambda i: (0, 0, 0)
    return pl.pallas_call(
        functools.partial(_gmlp_kernel, n_prompt_blocks=nbp, groups=groups, sample_len=sample_len),
        grid=(nb,),
        in_specs=[
            pl.BlockSpec((rows, d_a), lambda i: (i, 0)),
            pl.BlockSpec((rows, d_a), lambda i: (i, 1)),
            pl.BlockSpec((groups, rows, rows), full),
            pl.BlockSpec((groups, rows, rows), full),
            pl.BlockSpec((rows, groups), lambda i: (0, 0)),
            pl.BlockSpec((rows, groups), lambda i: (0, 0)),
            pl.BlockSpec((1, d_a), lambda i: (0, 0)),
        ],
        out_specs=[
            pl.BlockSpec((rows, d_a), lambda i: (i, 0)),
            pl.BlockSpec((rows, d_a), lambda i: (jnp.maximum(i - nbp, 0), 0)),
        ],
        out_shape=[
            jax.ShapeDtypeStruct((m, d_a), BF16),
            jax.ShapeDtypeStruct((m - n_prompt_rows, d_a), F32),
        ],
        compiler_params=_cparams(("arbitrary",)),
        name="gmlp_spatial",
    )(ua, ua, w_prompt, w_sample, b_prompt, b_sample, gain)


def _attn_kernel(bias_ref, q_ref, k0_ref, k1_ref, k2_ref, v0_ref, v1_ref, v2_ref, ck_ref, cv_ref,
                 o_ref, *, n_prompt_chunks, tiles_per_kv, head_dim):
    c = pl.program_id(0)
    t = q_ref.shape[0]
    kv_dim = k2_ref.shape[1]
    lkp = bias_ref.shape[1] // 2
    scale = head_dim ** -0.5
    nt = (((1,), (1,)), ((), ()))

    def run(k_parts, v_parts, first_key_chunk):
        k = jnp.concatenate(k_parts, axis=0)
        v = jnp.concatenate(v_parts, axis=0)
        lk = k.shape[0]
        lower = lax.broadcasted_iota(I32, (lk, LANES), 1) < head_dim
        zpad = jnp.zeros((lkp - lk, LANES), F32)
        rr = lax.broadcasted_iota(I32, (2 * lkp, LANES), 0) < lkp
        rl = lax.broadcasted_iota(I32, (2 * lkp, LANES), 1) < head_dim
        ones_sel = (rr == rl).astype(BF16)
        chunk_bias = None
        if first_key_chunk is not None:
            col = lax.broadcasted_iota(I32, (1, 2 * lkp), 1)
            key_chunk = first_key_chunk + (col % lkp) // t
            chunk_bias = jnp.where(key_chunk >= 0, 0.0, -jnp.inf)

        def doubled(tile, rolled, x):
            lo_src, hi_src = (tile, rolled) if x == 0 else (rolled, tile)
            return jnp.concatenate([jnp.where(lower, lo_src, 0.0), zpad,
                                    jnp.where(lower, 0.0, hi_src), zpad], axis=0).astype(BF16)

        for b in range(kv_dim // LANES):
            kt = k[:, b * LANES:(b + 1) * LANES]
            vt = v[:, b * LANES:(b + 1) * LANES]
            kr = pltpu.roll(kt, head_dim, 1)
            vr = pltpu.roll(vt, head_dim, 1)
            for x in range(2):
                j = 2 * b + x
                kcat = doubled(kt, kr, x)
                rhs = jnp.concatenate([doubled(vt, vr, x), ones_sel], axis=1)
                tiles = [j * tiles_per_kv + a for a in range(tiles_per_kv)]
                q2 = jnp.concatenate([q_ref[:, a * LANES:(a + 1) * LANES] for a in tiles], axis=0)
                q2 = (q2 * scale).astype(BF16)
                rows = []
                for a in tiles:
                    row = bias_ref[a:a + 1, :]
                    if chunk_bias is not None:
                        row = row + chunk_bias
                    rows.append(jnp.broadcast_to(row, (t, 2 * lkp)))
                s = lax.dot_general(q2, kcat, nt, preferred_element_type=F32) + jnp.concatenate(rows, axis=0)
                halves = []
                for hs in (slice(0, lkp), slice(lkp, 2 * lkp)):
                    sh = s[:, hs]
                    halves.append(jnp.exp(sh - jnp.max(sh, axis=-1, keepdims=True)))
                e = jnp.concatenate(halves, axis=1).astype(BF16)
                r = jnp.dot(e, rhs, preferred_element_type=F32)
                o = r[:, :LANES] / r[:, LANES:]
                for i, a in enumerate(tiles):
                    o_ref[:, a * LANES:(a + 1) * LANES] = o[i * t:(i + 1) * t].astype(o_ref.dtype)

    @pl.when(c < n_prompt_chunks)
    def _():
        run([k0_ref[...], k1_ref[...], k2_ref[...]], [v0_ref[...], v1_ref[...], v2_ref[...]], c - 2)

    @pl.when(c >= n_prompt_chunks)
    def _():
        run([ck_ref[...], k2_ref[...]], [cv_ref[...], v2_ref[...]], None)


def _attention(qkv, cache_k, cache_v, sinks, n_prompt_rows, q_dim, kv_dim, head_dim, n_heads):
    m = qkv.shape[0]
    t = CHUNK
    nc = n_prompt_rows // t
    n_kv_heads = kv_dim // head_dim
    window = cache_k.shape[0] // ((m - n_prompt_rows) // t)
    kcol, vcol = q_dim // kv_dim, q_dim // kv_dim + 1
    gqa = n_heads // n_kv_heads
    assert 2 * head_dim == LANES and gqa % 2 == 0 and kv_dim % LANES == 0

    lk = window + t
    lkp = -(-(lk + 1) // LANES) * LANES
    pair = sinks.reshape(n_heads // 2, 2, 1)
    bias = jnp.concatenate([jnp.zeros((n_heads // 2, 2, lk), F32), pair,
                            jnp.full((n_heads // 2, 2, lkp - lk - 1), -jnp.inf, F32)], axis=2)
    bias = bias.reshape(n_heads // 2, 2 * lkp)

    def kv_spec(back, col):
        return pl.BlockSpec((t, kv_dim), lambda c: (jnp.maximum(c - back, 0), col))

    cache_spec = pl.BlockSpec((window, kv_dim), lambda c: (jnp.maximum(c - nc, 0), 0))
    return pl.pallas_call(
        functools.partial(_attn_kernel, n_prompt_chunks=nc, tiles_per_kv=gqa // 2, head_dim=head_dim),
        grid=(m // t,),
        in_specs=[
            pl.BlockSpec((n_heads // 2, 2 * lkp), lambda c: (0, 0)),
            pl.BlockSpec((t, q_dim), lambda c: (c, 0)),
            kv_spec(2, kcol), kv_spec(1, kcol), kv_spec(0, kcol),
            kv_spec(2, vcol), kv_spec(1, vcol), kv_spec(0, vcol),
            cache_spec, cache_spec,
        ],
        out_specs=pl.BlockSpec((t, q_dim), lambda c: (c, 0)),
        out_shape=jax.ShapeDtypeStruct((m, q_dim), BF16),
        compiler_params=_cparams(("arbitrary",)),
        name="window_attention",
    )(bias, qkv, qkv, qkv, qkv, qkv, qkv, qkv, cache_k, cache_v)


def _merge_kernel(oa_ref, ob_ref, pa_ref, pb_ref, ga_ref, gb_ref, o_ref, pab_ref, pbb_ref):
    @pl.when(pl.program_id(1) == 0)
    def _():
        _cast_rows(pa_ref, pab_ref, 256)
        _cast_rows(pb_ref, pbb_ref, 256)

    a = jnp.dot(oa_ref[...], pab_ref[...], preferred_element_type=F32)
    b = jnp.dot(ob_ref[...], pbb_ref[...], preferred_element_type=F32)
    o_ref[...] = (ga_ref[...].astype(F32) * a + gb_ref[...].astype(F32) * b).astype(o_ref.dtype)


def _merge(o_a, o_b, p_a, p_b, gates):
    m, d_a = o_a.shape
    q_dim = o_b.shape[1]
    d = p_a.shape[1]
    bm, bn = min(ROW_TILE, m), min(COL_TILE, d)
    goff = d // bn
    return pl.pallas_call(
        _merge_kernel,
        grid=(d // bn, m // bm),
        in_specs=[
            pl.BlockSpec((bm, d_a), lambda n, i: (i, 0)),
            pl.BlockSpec((bm, q_dim), lambda n, i: (i, 0)),
            pl.BlockSpec((d_a, bn), lambda n, i: (0, n)),
            pl.BlockSpec((q_dim, bn), lambda n, i: (0, n)),
            pl.BlockSpec((bm, bn), lambda n, i: (i, n)),
            pl.BlockSpec((bm, bn), lambda n, i: (i, n + goff)),
        ],
        out_specs=pl.BlockSpec((bm, bn), lambda n, i: (i, n)),
        out_shape=jax.ShapeDtypeStruct((m, d), BF16),
        scratch_shapes=[pltpu.VMEM((d_a, bn), BF16), pltpu.VMEM((q_dim, bn), BF16)],
        compiler_params=_cparams(("arbitrary", "arbitrary")),
        name="merge_proj",
    )(o_a, o_b, p_a, p_b, gates, gates)


def _outproj_kernel(t_ref, w_ref, xp_ref, xs_ref, o_ref, wb_ref, *, n_prompt_tiles):
    i = pl.program_id(1)

    @pl.when(i == 0)
    def _():
        _cast_rows(w_ref, wb_ref, 256)

    z = jnp.dot(t_ref[...], wb_ref[...], preferred_element_type=F32)

    @pl.when(i < n_prompt_tiles)
    def _():
        o_ref[...] = xp_ref[...] + z

    @pl.when(i >= n_prompt_tiles)
    def _():
        o_ref[...] = xs_ref[...] + z


def _outproj(tm, w_out, xp, xs):
    m, d = tm.shape
    s, t = xp.shape[0], xs.shape[0]
    bm, bn = min(ROW_TILE, t), min(COL_TILE, d)
    npt = s // bm
    return pl.pallas_call(
        functools.partial(_outproj_kernel, n_prompt_tiles=npt),
        grid=(d // bn, m // bm),
        in_specs=[
            pl.BlockSpec((bm, d), lambda n, i: (i, 0)),
            pl.BlockSpec((d, bn), lambda n, i: (0, n)),
            pl.BlockSpec((bm, bn), lambda n, i: (jnp.minimum(i, npt - 1), n)),
            pl.BlockSpec((bm, bn), lambda n, i: (jnp.maximum(i - npt, 0), n)),
        ],
        out_specs=pl.BlockSpec((bm, bn), lambda n, i: (i, n)),
        out_shape=jax.ShapeDtypeStruct((m, d), F32),
        scratch_shapes=[pltpu.VMEM((d, bn), BF16)],
        compiler_params=_cparams(("arbitrary", "arbitrary")),
        name="out_proj",
    )(tm, w_out, xp, xs)


def _router_kernel(x_ref, g_ref, wr_ref, br_ref, route_ref, cnt_ref, whi_ref, wlo_ref,
                   carry_ref, *, n_groups, per_group):
    i = pl.program_id(0)

    @pl.when(i == 0)
    def _():
        w = wr_ref[...]
        hi = w.astype(BF16)
        whi_ref[...] = hi
        wlo_ref[...] = (w - hi.astype(F32)).astype(BF16)
        carry_ref[...] = jnp.zeros_like(carry_ref)

    hn = _rms(x_ref[...], g_ref[...])
    rows = hn.shape[0]
    hi = hn.astype(BF16)
    lo = (hn - hi.astype(F32)).astype(BF16)
    logits = (jnp.dot(hi, whi_ref[...], preferred_element_type=F32)
              + jnp.dot(hi, wlo_ref[...], preferred_element_type=F32)
              + jnp.dot(lo, whi_ref[...], preferred_element_type=F32)) + br_ref[...]
    lane = lax.broadcasted_iota(I32, logits.shape, 1)
    big = jnp.int32(LANES)

    lg = jnp.where(lane < n_groups, logits, -jnp.inf)
    mg = jnp.max(lg, axis=-1, keepdims=True)
    pg_sel = 1.0 / jnp.sum(jnp.exp(lg - mg), axis=-1, keepdims=True)
    gsel = jnp.min(jnp.where(lg == mg, lane, big), axis=-1, keepdims=True)

    first = n_groups + gsel * per_group
    emask = (lane >= first) & (lane < first + per_group)
    le = jnp.where(emask, logits, -jnp.inf)
    me = jnp.max(le, axis=-1, keepdims=True)
    ee = jnp.exp(le - me)
    pe = jnp.where(emask, ee / jnp.sum(ee, axis=-1, keepdims=True), -1.0)
    p1 = jnp.max(pe, axis=-1, keepdims=True)
    i1 = jnp.min(jnp.where(pe == p1, lane, big), axis=-1, keepdims=True)
    pe2 = jnp.where(lane == i1, -1.0, pe)
    p2 = jnp.max(pe2, axis=-1, keepdims=True)
    i2 = jnp.min(jnp.where(pe2 == p2, lane, big), axis=-1, keepdims=True)
    psum = p1 + p2
    w1 = p1 / psum * pg_sel
    w2 = p2 / psum * pg_sel
    e1 = i1 - n_groups
    e2 = i2 - n_groups

    oh1 = (lane == e1).astype(F32)
    oh2 = (lane == e2).astype(F32)
    ohs = oh1 + oh2
    ri = lax.broadcasted_iota(I32, (rows, rows), 0)
    ci = lax.broadcasted_iota(I32, (rows, rows), 1)
    below = (ci < ri).astype(BF16)
    before = jnp.dot(below, ohs.astype(BF16), preferred_element_type=F32) + carry_ref[...]
    r1 = jnp.sum(before * oh1, axis=-1, keepdims=True)
    r2 = jnp.sum(before * oh2, axis=-1, keepdims=True)
    carry_ref[...] = carry_ref[...] + jnp.sum(ohs, axis=0, keepdims=True)
    cnt_ref[...] = carry_ref[...]

    route = jnp.where(lane == 0, e1.astype(F32), 0.0)
    route = jnp.where(lane == 1, e2.astype(F32), route)
    route = jnp.where(lane == 2, w1, route)
    route = jnp.where(lane == 3, w2, route)
    route = jnp.where(lane == 4, r1, route)
    route = jnp.where(lane == 5, r2, route)
    route_ref[...] = route


def _router(x2, gain, wr, br, n_groups, per_group):
    m, d = x2.shape
    br_rows = min(ROUTE_ROWS, m)
    return pl.pallas_call(
        functools.partial(_router_kernel, n_groups=n_groups, per_group=per_group),
        grid=(m // br_rows,),
        in_specs=[
            pl.BlockSpec((br_rows, d), lambda i: (i, 0)),
            pl.BlockSpec((1, d), lambda i: (0, 0)),
            pl.BlockSpec((d, LANES), lambda i: (0, 0)),
            pl.BlockSpec((1, LANES), lambda i: (0, 0)),
        ],
        out_specs=[
            pl.BlockSpec((br_rows, LANES), lambda i: (i, 0)),
            pl.BlockSpec((1, LANES), lambda i: (0, 0)),
        ],
        out_shape=[
            jax.ShapeDtypeStruct((m, LANES), F32),
            jax.ShapeDtypeStruct((1, LANES), F32),
        ],
        scratch_shapes=[pltpu.VMEM((d, LANES), BF16), pltpu.VMEM((d, LANES), BF16),
                        pltpu.VMEM((1, LANES), F32)],
        compiler_params=_cparams(("arbitrary",)),
        name="norm2_router",
    )(x2, gain, wr, br)


def _dispatch_kernel(dest_ref, x_ref, g_ref, xs_ref, buf_ref, sem):
    i = pl.program_id(0)
    tm = x_ref.shape[0]
    buf_ref[...] = _rms(x_ref[...], g_ref[...])

    def start(r, c):
        for k in range(TOP_K):
            d = dest_ref[TOP_K * (i * tm + r) + k]
            pltpu.make_async_copy(buf_ref.at[pl.ds(r, 1)], xs_ref.at[pl.ds(d, 1)], sem).start()
        return c

    lax.fori_loop(0, tm, start, 0, unroll=DMA_UNROLL)
    for k in range(TOP_K):
        pltpu.make_async_copy(buf_ref, xs_ref.at[pl.ds(0, tm)], sem).wait()


def _dispatch(dest_flat, x2, gain, n_rows):
    m, d = x2.shape
    tm = min(DISPATCH_ROWS, m)
    return pl.pallas_call(
        _dispatch_kernel,
        grid_spec=pltpu.PrefetchScalarGridSpec(
            num_scalar_prefetch=1,
            grid=(m // tm,),
            in_specs=[
                pl.BlockSpec((tm, d), lambda i, dest: (i, 0)),
                pl.BlockSpec((1, d), lambda i, dest: (0, 0)),
            ],
            out_specs=pl.BlockSpec(memory_space=pl.ANY),
            scratch_shapes=[pltpu.VMEM((tm, d), F32), pltpu.SemaphoreType.DMA],
        ),
        out_shape=jax.ShapeDtypeStruct((n_rows, d), F32),
        compiler_params=_cparams(("arbitrary",)),
        name="moe_dispatch",
    )(dest_flat, x2, gain)


def _moe_kernel(be_ref, bidx_ref, nrows_ref, nv_ref, xs_ref, wg_ref, wu_ref, wd_ref, y_ref):
    i = pl.program_id(0)
    j = pl.program_id(1)

    @pl.when(i < nv_ref[0])
    def _():
        rows = xs_ref.shape[0]
        live = lax.broadcasted_iota(I32, (rows, 1), 0) < nrows_ref[i]
        x = jnp.where(live, xs_ref[...], 0.0).astype(BF16)
        a = jnp.dot(x, wg_ref[0].astype(BF16), preferred_element_type=F32)
        b = jnp.dot(x, wu_ref[0].astype(BF16), preferred_element_type=F32)
        act = (jax.nn.silu(a) * b).astype(BF16)
        part = jnp.dot(act, wd_ref[0].astype(BF16), preferred_element_type=F32)

        @pl.when(j == 0)
        def _():
            y_ref[...] = part

        @pl.when(j != 0)
        def _():
            y_ref[...] = y_ref[...] + part


def _moe_experts(block_e, block_idx, block_rows, n_valid, xs, w_gate, w_up, w_down, n_blocks):
    d = xs.shape[1]
    d_e = w_gate.shape[2]
    n_split = 2
    dh = d_e // n_split
    br = MOE_ROWS

    def col(i, j, nv):
        return jnp.where(i < nv[0], j, n_split - 1)

    return pl.pallas_call(
        _moe_kernel,
        grid_spec=pltpu.PrefetchScalarGridSpec(
            num_scalar_prefetch=4,
            grid=(n_blocks, n_split),
            in_specs=[
                pl.BlockSpec((br, d), lambda i, j, be, bi, nr, nv: (bi[i], 0)),
                pl.BlockSpec((1, d, dh), lambda i, j, be, bi, nr, nv: (be[i], 0, col(i, j, nv))),
                pl.BlockSpec((1, d, dh), lambda i, j, be, bi, nr, nv: (be[i], 0, col(i, j, nv))),
                pl.BlockSpec((1, dh, d), lambda i, j, be, bi, nr, nv: (be[i], col(i, j, nv), 0)),
            ],
            out_specs=pl.BlockSpec((br, d), lambda i, j, be, bi, nr, nv: (bi[i], 0)),
        ),
        out_shape=jax.ShapeDtypeStruct(xs.shape, F32),
        compiler_params=_cparams(("arbitrary", "arbitrary")),
        name="moe_experts",
    )(block_e, block_idx, block_rows, n_valid, xs, w_gate, w_up, w_down)


def _combine_kernel(dest_ref, x_ref, route_ref, y_ref, op_ref, os_ref, buf_ref, sem, *,
                    n_prompt_tiles, n_tiles):
    i = pl.program_id(0)
    tm = x_ref.shape[0]
    slot = i % 2

    def gather(tile, to_slot):
        def start(r, c):
            for k in range(TOP_K):
                d = dest_ref[TOP_K * (tile * tm + r) + k]
                pltpu.make_async_copy(y_ref.at[pl.ds(d, 1)], buf_ref.at[to_slot, k, pl.ds(r, 1)],
                                      sem.at[to_slot]).start()
            return c

        lax.fori_loop(0, tm, start, 0, unroll=DMA_UNROLL)

    @pl.when(i == 0)
    def _():
        gather(0, 0)

    @pl.when(i + 1 < n_tiles)
    def _():
        gather(i + 1, 1 - slot)

    for k in range(TOP_K):
        pltpu.make_async_copy(y_ref.at[pl.ds(0, tm)], buf_ref.at[slot, k], sem.at[slot]).wait()

    out = x_ref[...]
    for k in range(TOP_K):
        out = out + route_ref[:, 2 + k:3 + k] * buf_ref[slot, k]

    @pl.when(i < n_prompt_tiles)
    def _():
        op_ref[...] = out

    @pl.when(i >= n_prompt_tiles)
    def _():
        os_ref[...] = out


def _combine(dest_flat, x2, route, y, n_prompt_rows):
    m, d = x2.shape
    tm = COMBINE_ROWS
    npt = n_prompt_rows // tm
    return pl.pallas_call(
        functools.partial(_combine_kernel, n_prompt_tiles=npt, n_tiles=m // tm),
        grid_spec=pltpu.PrefetchScalarGridSpec(
            num_scalar_prefetch=1,
            grid=(m // tm,),
            in_specs=[
                pl.BlockSpec((tm, d), lambda i, dest: (i, 0)),
                pl.BlockSpec((tm, LANES), lambda i, dest: (i, 0)),
                pl.BlockSpec(memory_space=pl.ANY),
            ],
            out_specs=[
                pl.BlockSpec((tm, d), lambda i, dest: (jnp.minimum(i, npt - 1), 0)),
                pl.BlockSpec((tm, d), lambda i, dest: (jnp.maximum(i - npt, 0), 0)),
            ],
            scratch_shapes=[pltpu.VMEM((2, TOP_K, tm, d), F32), pltpu.SemaphoreType.DMA((2,))],
        ),
        out_shape=[
            jax.ShapeDtypeStruct((n_prompt_rows, d), F32),
            jax.ShapeDtypeStruct((m - n_prompt_rows, d), F32),
        ],
        compiler_params=_cparams(("arbitrary",)),
        name="moe_combine",
    )(dest_flat, x2, route, y)


def _rope_tables(positions, head_dim):
    rot_dim = head_dim // 4
    half = rot_dim // 2
    inv_freq = jnp.power(ROPE_THETA, -jnp.arange(half, dtype=F32) * 2.0 / rot_dim)
    ang = positions.astype(F32)[:, None] * inv_freq[None, :]
    cos, sin = jnp.cos(ang), jnp.sin(ang)
    m = positions.shape[0]
    zeros = lambda n: jnp.zeros((m, n), F32)
    cos_h = jnp.concatenate([cos, cos, jnp.ones((m, head_dim - rot_dim), F32)], axis=1)
    sa_h = jnp.concatenate([-sin, zeros(head_dim - half)], axis=1)
    sb_h = jnp.concatenate([zeros(half), sin, zeros(head_dim - rot_dim)], axis=1)
    reps = LANES // head_dim
    return tuple(jnp.tile(a, (1, reps)) for a in (cos_h, sa_h, sb_h)), half


def _layer(xp, xs, cache_k, cache_v, norm1_g, w_in, gmlp_norm_g, w_s, b_s, q_norm_g, k_norm_g,
           sinks, p_a, p_b, w_out, norm2_g, w_rg, b_rg, w_re, b_re, w_gate_e, w_up_e, w_down_e):
    s, d = xp.shape
    dec_rows = xs.shape[0]
    dec_batch = cache_k.shape[0]
    t = dec_rows // dec_batch
    m = s + dec_rows
    d_a = gmlp_norm_g.shape[0]
    head_dim = q_norm_g.shape[0]
    n_heads = sinks.shape[0]
    q_dim = n_heads * head_dim
    kv_dim = cache_k.shape[2] * cache_k.shape[3]
    n_groups, per_group = w_re.shape[1], w_re.shape[2]
    n_experts = n_groups * per_group

    h = _norm1(xp, xs, norm1_g[None, :])
    ua = _inproj_act(h, w_in, 0, 2 * d_a, jax.nn.gelu, BF16, "inproj_gelu")
    positions = jnp.concatenate(
        [jnp.arange(s, dtype=I32), jnp.tile(PAST_LEN + jnp.arange(t, dtype=I32), dec_batch)])
    (cos_t, sa_t, sb_t), rot_half = _rope_tables(positions, head_dim)
    gain_row = jnp.concatenate([jnp.tile(q_norm_g, n_heads), jnp.tile(k_norm_g, kv_dim // head_dim),
                                jnp.ones((kv_dim,), F32)])[None, :]
    qkv = _inproj_qkv(h, w_in, 2 * d_a, q_dim, kv_dim, gain_row, cos_t, sa_t, sb_t, head_dim, rot_half)
    gates = _inproj_act(h, w_in, 2 * d_a + q_dim + 2 * kv_dim, 2 * d, jax.nn.sigmoid, BF16,
                        "inproj_gate")

    rows = w_s.shape[1]
    reps = rows // t
    w_sample = jnp.tile(w_s[:, :t, :t], (1, reps, reps))
    b_sample = jnp.tile(b_s[:, :t], (1, reps))
    o_a, vn_s = _gmlp(ua, w_s, w_sample, b_s.T, b_sample.T, gmlp_norm_g[None, :], s, t)
    o_b = _attention(qkv, cache_k.reshape(-1, kv_dim), cache_v.reshape(-1, kv_dim), sinks[None, :],
                     s, q_dim, kv_dim, head_dim, n_heads)

    merged = _merge(o_a, o_b, p_a, p_b, gates)
    x2 = _outproj(merged, w_out, xp, xs)

    pad = LANES - n_groups - n_experts
    wr = jnp.concatenate([w_rg, w_re.reshape(d, n_experts), jnp.zeros((d, pad), F32)], axis=1)
    br = jnp.concatenate([b_rg, b_re.reshape(n_experts), jnp.zeros((pad,), F32)])[None, :]
    route, counts = _router(x2, norm2_g[None, :], wr, br, n_groups, per_group)
    e_idx = route[:, 0:TOP_K].astype(I32)
    rank = route[:, 4:4 + TOP_K].astype(I32)
    counts = counts[0, :n_experts].astype(I32)

    blk = MOE_ROWS
    n_blocks = -(-(m * TOP_K) // blk) + n_experts
    padded = ((counts + blk - 1) // blk) * blk
    pend = jnp.cumsum(padded)
    pstart = pend - padded
    dest = (pstart[e_idx] + rank).reshape(-1).astype(I32)
    n_valid = (pend[-1] // blk).astype(I32)
    ids = jnp.arange(n_blocks, dtype=I32)
    block_idx = jnp.minimum(ids, n_valid - 1)
    block_e = jnp.minimum(jnp.searchsorted(pend, block_idx * blk, side="right"), n_experts - 1).astype(I32)
    block_rows = jnp.clip(counts[block_e] - (block_idx * blk - pstart[block_e]), 0, blk).astype(I32)

    xs_sorted = _dispatch(dest, x2, norm2_g[None, :], n_blocks * blk)
    y_sorted = _moe_experts(block_e, block_idx, block_rows, n_valid[None], xs_sorted,
                            w_gate_e, w_up_e, w_down_e, n_blocks)
    yp, ys = _combine(dest, x2, route, y_sorted, s)
    return yp, ys, qkv, vn_s


def kernel(x_prompt, x_sample, cache_k_win, cache_v_win, norm1_g, w_in, gmlp_norm_g, w_s, b_s,
           q_norm_g, k_norm_g, sinks, p_a, p_b, w_out, norm2_g, w_rg, b_rg, w_re, b_re,
           w_gate_e, w_up_e, w_down_e):
    depth = norm1_g.shape[0]
    assert depth == 1, "weights of one layer are expected"
    batch, s, d = x_prompt.shape
    assert batch == 1
    dec_batch, t, _ = x_sample.shape
    head_dim = q_norm_g.shape[-1]
    n_kv = cache_k_win.shape[3]
    q_dim = sinks.shape[-1] * head_dim
    kv_dim = n_kv * head_dim
    keep = min(cache_k_win.shape[2], s)

    l = 0
    yp, ys, qkv, vn_s = _layer(
        x_prompt.reshape(s, d), x_sample.reshape(dec_batch * t, d), cache_k_win[l], cache_v_win[l],
        norm1_g[l], w_in[l], gmlp_norm_g[l], w_s[l], b_s[l], q_norm_g[l], k_norm_g[l], sinks[l],
        p_a[l], p_b[l], w_out[l], norm2_g[l], w_rg[l], b_rg[l], w_re[l], b_re[l],
        w_gate_e[l], w_up_e[l], w_down_e[l])

    k_all = qkv[:, q_dim:q_dim + kv_dim]
    v_all = qkv[:, q_dim + kv_dim:]
    k_win_p = k_all[s - keep:s].reshape(1, batch, keep, n_kv, head_dim)
    v_win_p = v_all[s - keep:s].reshape(1, batch, keep, n_kv, head_dim)
    k_new_s = k_all[s:].reshape(1, dec_batch, t, n_kv, head_dim)
    v_new_s = v_all[s:].reshape(1, dec_batch, t, n_kv, head_dim)
    gv_s = vn_s.reshape(1, dec_batch, t, -1)
    return (yp.reshape(batch, s, d), ys.reshape(dec_batch, t, d), k_win_p, v_win_p, k_new_s, v_new_s, gv_s)
```

```python
import functools

import jax
import jax.numpy as jnp
from jax import lax
from jax.experimental import pallas as pl
from jax.experimental.pallas import tpu as pltpu

F32 = jnp.float32
BF16 = jnp.bfloat16
I32 = jnp.int32
U32 = jnp.uint32

EPS = 1e-6
PAST_LEN = 1024
CHUNK = 64
ROPE_THETA = 500000.0
TOP_K = 2
LANES = 128

VMEM_LIMIT_BYTES = 56 * 1024 * 1024

ROW_TILE = 1024
WIDE_ROW_TILE = 1536
COL_TILE = 512
NORM_ROWS = 256
ROUTE_ROWS = 256
MOE_ROWS = 320
COMBINE_ROWS = 128
DISPATCH_ROWS = 256
DMA_UNROLL = 8


def _cparams(sem):
    return pltpu.CompilerParams(dimension_semantics=sem, vmem_limit_bytes=VMEM_LIMIT_BYTES)


def _cast_rows(src_ref, dst_ref, rows):
    n = src_ref.shape[0] // rows

    def body(r, c):
        sl = pl.ds(pl.multiple_of(r * rows, rows), rows)
        dst_ref[sl, :] = src_ref[sl, :].astype(dst_ref.dtype)
        return c

    lax.fori_loop(0, n, body, 0)


def _rms(x, gain):
    ms = jnp.mean(x * x, axis=-1, keepdims=True)
    return x * lax.rsqrt(ms + EPS) * gain


def _norm1_kernel(xp_ref, xs_ref, g_ref, h_ref, *, n_prompt_blocks):
    i = pl.program_id(0)

    @pl.when(i < n_prompt_blocks)
    def _():
        h_ref[...] = _rms(xp_ref[...], g_ref[...]).astype(h_ref.dtype)

    @pl.when(i >= n_prompt_blocks)
    def _():
        h_ref[...] = _rms(xs_ref[...], g_ref[...]).astype(h_ref.dtype)


def _norm1(xp, xs, gain):
    s, d = xp.shape
    t = xs.shape[0]
    br = min(NORM_ROWS, t)
    nbp, nbs = s // br, t // br
    return pl.pallas_call(
        functools.partial(_norm1_kernel, n_prompt_blocks=nbp),
        grid=(nbp + nbs,),
        in_specs=[
            pl.BlockSpec((br, d), lambda i: (jnp.minimum(i, nbp - 1), 0)),
            pl.BlockSpec((br, d), lambda i: (jnp.maximum(i - nbp, 0), 0)),
            pl.BlockSpec((1, d), lambda i: (0, 0)),
        ],
        out_specs=pl.BlockSpec((br, d), lambda i: (i, 0)),
        out_shape=jax.ShapeDtypeStruct((s + t, d), BF16),
        compiler_params=_cparams(("arbitrary",)),
        name="norm1",
    )(xp, xs, gain)


def _inproj_act_kernel(h_ref, w_ref, o_ref, wb_ref, *, act):
    @pl.when(pl.program_id(1) == 0)
    def _():
        _cast_rows(w_ref, wb_ref, 256)

    z = jnp.dot(h_ref[...], wb_ref[...], preferred_element_type=F32)
    o_ref[...] = act(z).astype(o_ref.dtype)


def _inproj_act(h, w, col0, ncols, act, out_dtype, name):
    m, d = h.shape
    bm = WIDE_ROW_TILE if m % WIDE_ROW_TILE == 0 else min(ROW_TILE, m)
    bn = min(COL_TILE, ncols)
    assert col0 % bn == 0 and ncols % bn == 0 and m % bm == 0
    off = col0 // bn
    return pl.pallas_call(
        functools.partial(_inproj_act_kernel, act=act),
        grid=(ncols // bn, m // bm),
        in_specs=[
            pl.BlockSpec((bm, d), lambda n, i: (i, 0)),
            pl.BlockSpec((d, bn), lambda n, i: (0, n + off)),
        ],
        out_specs=pl.BlockSpec((bm, bn), lambda n, i: (i, n)),
        out_shape=jax.ShapeDtypeStruct((m, ncols), out_dtype),
        scratch_shapes=[pltpu.VMEM((d, bn), BF16)],
        compiler_params=_cparams(("arbitrary", "arbitrary")),
        name=name,
    )(h, w)


def _inproj_qkv_kernel(h_ref, w_ref, gain_ref, cos_ref, sa_ref, sb_ref, o_ref, wb_ref, *,
                       n_norm_tiles, head_dim, rot_half):
    n = pl.program_id(0)

    @pl.when(pl.program_id(1) == 0)
    def _():
        _cast_rows(w_ref, wb_ref, 256)

    z = jnp.dot(h_ref[...], wb_ref[...], preferred_element_type=F32)
    bn = z.shape[1]

    @pl.when(n < n_norm_tiles)
    def _():
        r = lax.broadcasted_iota(I32, (bn, bn), 0) // head_dim
        c = lax.broadcasted_iota(I32, (bn, bn), 1) // head_dim
        seg = (r == c).astype(BF16)
        zz = z * z
        hi = zz.astype(BF16)
        lo = (zz - hi.astype(F32)).astype(BF16)
        ssq = (jnp.dot(hi, seg, preferred_element_type=F32)
               + jnp.dot(lo, seg, preferred_element_type=F32))
        y = z * lax.rsqrt(ssq * (1.0 / head_dim) + EPS) * gain_ref[...]
        reps = bn // cos_ref.shape[1]
        cosv = jnp.tile(cos_ref[...], (1, reps))
        sa = jnp.tile(sa_ref[...], (1, reps))
        sb = jnp.tile(sb_ref[...], (1, reps))
        y = y * cosv + pltpu.roll(y, bn - rot_half, 1) * sa + pltpu.roll(y, rot_half, 1) * sb
        o_ref[...] = y

    @pl.when(n >= n_norm_tiles)
    def _():
        o_ref[...] = z


def _inproj_qkv(h, w, col0, q_dim, kv_dim, gain_row, cos_t, sa_t, sb_t, head_dim, rot_half):
    m, d = h.shape
    ncols = q_dim + 2 * kv_dim
    bm, bn = min(ROW_TILE, m), kv_dim
    assert col0 % bn == 0 and q_dim % bn == 0 and bn % LANES == 0 and m % bm == 0
    off = col0 // bn
    n_norm_tiles = (q_dim + kv_dim) // bn
    tw = cos_t.shape[1]
    return pl.pallas_call(
        functools.partial(_inproj_qkv_kernel, n_norm_tiles=n_norm_tiles, head_dim=head_dim,
                          rot_half=rot_half),
        grid=(ncols // bn, m // bm),
        in_specs=[
            pl.BlockSpec((bm, d), lambda n, i: (i, 0)),
            pl.BlockSpec((d, bn), lambda n, i: (0, n + off)),
            pl.BlockSpec((1, bn), lambda n, i: (0, n)),
            pl.BlockSpec((bm, tw), lambda n, i: (i, 0)),
            pl.BlockSpec((bm, tw), lambda n, i: (i, 0)),
            pl.BlockSpec((bm, tw), lambda n, i: (i, 0)),
        ],
        out_specs=pl.BlockSpec((bm, bn), lambda n, i: (i, n)),
        out_shape=jax.ShapeDtypeStruct((m, ncols), F32),
        scratch_shapes=[pltpu.VMEM((d, bn), BF16)],
        compiler_params=_cparams(("arbitrary", "arbitrary")),
        name="inproj_qkv",
    )(h, w, gain_row, cos_t, sa_t, sb_t)


def _gmlp_kernel(u_ref, va_ref, wp_ref, ws_ref, bp_ref, bs_ref, gain_ref, o_ref, vn_ref, *,
                 n_prompt_blocks, groups, sample_len):
    i = pl.program_id(0)
    rows, d_a = u_ref.shape
    gw = d_a // groups
    ri = lax.broadcasted_iota(I32, (rows, rows), 0)
    ci = lax.broadcasted_iota(I32, (rows, rows), 1)

    def run(w_ref, b_ref, sub, emit_vn):
        vn = _rms(va_ref[...].astype(F32), gain_ref[...])
        if emit_vn:
            vn_ref[...] = vn
        vb = vn.astype(BF16)
        mask = (ci <= ri) & ((ri // sub) == (ci // sub))
        for g in range(groups):
            sl = slice(g * gw, (g + 1) * gw)
            wg = jnp.where(mask, w_ref[g], 0.0).astype(BF16)
            s = jnp.dot(wg, vb[:, sl], preferred_element_type=F32) + b_ref[:, g:g + 1]
            o_ref[:, sl] = (u_ref[:, sl].astype(F32) * s).astype(o_ref.dtype)

    @pl.when(i < n_prompt_blocks)
    def _():
        run(wp_ref, bp_ref, rows, False)

    @pl.when(i >= n_prompt_blocks)
    def _():
        run(ws_ref, bs_ref, sample_len, True)


def _gmlp(ua, w_prompt, w_sample, b_prompt, b_sample, gain, n_prompt_rows, sample_len):
    m, two_da = ua.shape
    d_a = two_da // 2
    groups, rows, _ = w_prompt.shape
    nbp = n_prompt_rows // rows
    nb = m // rows
    full = lambda i: (0, 0, 0)
    return pl.pallas_call(
        functools.partial(_gmlp_kernel, n_prompt_blocks=nbp, groups=groups, sample_len=sample_len),
        grid=(nb,),
        in_specs=[
            pl.BlockSpec((rows, d_a), lambda i: (i, 0)),
            pl.BlockSpec((rows, d_a), lambda i: (i, 1)),
            pl.BlockSpec((groups, rows, rows), full),
            pl.BlockSpec((groups, rows, rows), full),
            pl.BlockSpec((rows, groups), lambda i: (0, 0)),
            pl.BlockSpec((rows, groups), lambda i: (0, 0)),
            pl.BlockSpec((1, d_a), lambda i: (0, 0)),
        ],
        out_specs=[
            pl.BlockSpec((rows, d_a), lambda i: (i, 0)),
            pl.BlockSpec((rows, d_a), lambda i: (jnp.maximum(i - nbp, 0), 0)),
        ],
        out_shape=[
            jax.ShapeDtypeStruct((m, d_a), BF16),
            jax.ShapeDtypeStruct((m - n_prompt_rows, d_a), F32),
        ],
        compiler_params=_cparams(("arbitrary",)),
        name="gmlp_spatial",
    )(ua, ua, w_prompt, w_sample, b_prompt, b_sample, gain)


def _attn_kernel(bias_ref, q_ref, k0_ref, k1_ref, k2_ref, v0_ref, v1_ref, v2_ref, ck_ref, cv_ref,
                 o_ref, *, n_prompt_chunks, tiles_per_kv, head_dim):
    c = pl.program_id(0)
    t = q_ref.shape[0]
    kv_dim = k2_ref.shape[1]
    lkp = bias_ref.shape[1] // 2
    scale = head_dim ** -0.5
    nt = (((1,), (1,)), ((), ()))

    def run(k_parts, v_parts, first_key_chunk):
        k = jnp.concatenate(k_parts, axis=0)
        v = jnp.concatenate(v_parts, axis=0)
        lk = k.shape[0]
        lower = lax.broadcasted_iota(I32, (lk, LANES), 1) < head_dim
        zpad = jnp.zeros((lkp - lk, LANES), F32)
        rr = lax.broadcasted_iota(I32, (2 * lkp, LANES), 0) < lkp
        rl = lax.broadcasted_iota(I32, (2 * lkp, LANES), 1) < head_dim
        ones_sel = (rr == rl).astype(BF16)
        chunk_bias = None
        if first_key_chunk is not None:
            col = lax.broadcasted_iota(I32, (1, 2 * lkp), 1)
            key_chunk = first_key_chunk + (col % lkp) // t
            chunk_bias = jnp.where(key_chunk >= 0, 0.0, -jnp.inf)

        def doubled(tile, rolled, x):
            lo_src, hi_src = (tile, rolled) if x == 0 else (rolled, tile)
            return jnp.concatenate([jnp.where(lower, lo_src, 0.0), zpad,
                                    jnp.where(lower, 0.0, hi_src), zpad], axis=0).astype(BF16)

        for b in range(kv_dim // LANES):
            kt = k[:, b * LANES:(b + 1) * LANES]
            vt = v[:, b * LANES:(b + 1) * LANES]
            kr = pltpu.roll(kt, head_dim, 1)
            vr = pltpu.roll(vt, head_dim, 1)
            for x in range(2):
                j = 2 * b + x
                kcat = doubled(kt, kr, x)
                rhs = jnp.concatenate([doubled(vt, vr, x), ones_sel], axis=1)
                tiles = [j * tiles_per_kv + a for a in range(tiles_per_kv)]
                q2 = jnp.concatenate([q_ref[:, a * LANES:(a + 1) * LANES] for a in tiles], axis=0)
                q2 = (q2 * scale).astype(BF16)
                rows = []
                for a in tiles:
                    row = bias_ref[a:a + 1, :]
                    if chunk_bias is not None:
                        row = row + chunk_bias
                    rows.append(jnp.broadcast_to(row, (t, 2 * lkp)))
                s = lax.dot_general(q2, kcat, nt, preferred_element_type=F32) + jnp.concatenate(rows, axis=0)
                halves = []
                for hs in (slice(0, lkp), slice(lkp, 2 * lkp)):
                    sh = s[:, hs]
                    halves.append(jnp.exp(sh - jnp.max(sh, axis=-1, keepdims=True)))
                e = jnp.concatenate(halves, axis=1).astype(BF16)
                r = jnp.dot(e, rhs, preferred_element_type=F32)
                o = r[:, :LANES] / r[:, LANES:]
                for i, a in enumerate(tiles):
                    o_ref[:, a * LANES:(a + 1) * LANES] = o[i * t:(i + 1) * t].astype(o_ref.dtype)

    @pl.when(c < n_prompt_chunks)
    def _():
        run([k0_ref[...], k1_ref[...], k2_ref[...]], [v0_ref[...], v1_ref[...], v2_ref[...]], c - 2)

    @pl.when(c >= n_prompt_chunks)
    def _():
        run([ck_ref[...], k2_ref[...]], [cv_ref[...], v2_ref[...]], None)


def _attention(qkv, cache_k, cache_v, sinks, n_prompt_rows, q_dim, kv_dim, head_dim, n_heads):
    m = qkv.shape[0]
    t = CHUNK
    nc = n_prompt_rows // t
    n_kv_heads = kv_dim // head_dim
    window = cache_k.shape[0] // ((m - n_prompt_rows) // t)
    kcol, vcol = q_dim // kv_dim, q_dim // kv_dim + 1
    gqa = n_heads // n_kv_heads
    assert 2 * head_dim == LANES and gqa % 2 == 0 and kv_dim % LANES == 0

    lk = window + t
    lkp = -(-(lk + 1) // LANES) * LANES
    pair = sinks.reshape(n_heads // 2, 2, 1)
    bias = jnp.concatenate([jnp.zeros((n_heads // 2, 2, lk), F32), pair,
                            jnp.full((n_heads // 2, 2, lkp - lk - 1), -jnp.inf, F32)], axis=2)
    bias = bias.reshape(n_heads // 2, 2 * lkp)

    def kv_spec(back, col):
        return pl.BlockSpec((t, kv_dim), lambda c: (jnp.maximum(c - back, 0), col))

    cache_spec = pl.BlockSpec((window, kv_dim), lambda c: (jnp.maximum(c - nc, 0), 0))
    return pl.pallas_call(
        functools.partial(_attn_kernel, n_prompt_chunks=nc, tiles_per_kv=gqa // 2, head_dim=head_dim),
        grid=(m // t,),
        in_specs=[
            pl.BlockSpec((n_heads // 2, 2 * lkp), lambda c: (0, 0)),
            pl.BlockSpec((t, q_dim), lambda c: (c, 0)),
            kv_spec(2, kcol), kv_spec(1, kcol), kv_spec(0, kcol),
            kv_spec(2, vcol), kv_spec(1, vcol), kv_spec(0, vcol),
            cache_spec, cache_spec,
        ],
        out_specs=pl.BlockSpec((t, q_dim), lambda c: (c, 0)),
        out_shape=jax.ShapeDtypeStruct((m, q_dim), BF16),
        compiler_params=_cparams(("arbitrary",)),
        name="window_attention",
    )(bias, qkv, qkv, qkv, qkv, qkv, qkv, qkv, cache_k, cache_v)


def _merge_kernel(oa_ref, ob_ref, pa_ref, pb_ref, ga_ref, gb_ref, o_ref, pab_ref, pbb_ref):
    @pl.when(pl.program_id(1) == 0)
    def _():
        _cast_rows(pa_ref, pab_ref, 256)
        _cast_rows(pb_ref, pbb_ref, 256)

    a = jnp.dot(oa_ref[...], pab_ref[...], preferred_element_type=F32)
    b = jnp.dot(ob_ref[...], pbb_ref[...], preferred_element_type=F32)
    o_ref[...] = (ga_ref[...].astype(F32) * a + gb_ref[...].astype(F32) * b).astype(o_ref.dtype)


def _merge(o_a, o_b, p_a, p_b, gates):
    m, d_a = o_a.shape
    q_dim = o_b.shape[1]
    d = p_a.shape[1]
    bm, bn = min(ROW_TILE, m), min(COL_TILE, d)
    goff = d // bn
    return pl.pallas_call(
        _merge_kernel,
        grid=(d // bn, m // bm),
        in_specs=[
            pl.BlockSpec((bm, d_a), lambda n, i: (i, 0)),
            pl.BlockSpec((bm, q_dim), lambda n, i: (i, 0)),
            pl.BlockSpec((d_a, bn), lambda n, i: (0, n)),
            pl.BlockSpec((q_dim, bn), lambda n, i: (0, n)),
            pl.BlockSpec((bm, bn), lambda n, i: (i, n)),
            pl.BlockSpec((bm, bn), lambda n, i: (i, n + goff)),
        ],
        out_specs=pl.BlockSpec((bm, bn), lambda n, i: (i, n)),
        out_shape=jax.ShapeDtypeStruct((m, d), BF16),
        scratch_shapes=[pltpu.VMEM((d_a, bn), BF16), pltpu.VMEM((q_dim, bn), BF16)],
        compiler_params=_cparams(("arbitrary", "arbitrary")),
        name="merge_proj",
    )(o_a, o_b, p_a, p_b, gates, gates)


def _outproj_kernel(t_ref, w_ref, xp_ref, xs_ref, o_ref, wb_ref, *, n_prompt_tiles):
    i = pl.program_id(1)

    @pl.when(i == 0)
    def _():
        _cast_rows(w_ref, wb_ref, 256)

    z = jnp.dot(t_ref[...], wb_ref[...], preferred_element_type=F32)

    @pl.when(i < n_prompt_tiles)
    def _():
        o_ref[...] = xp_ref[...] + z

    @pl.when(i >= n_prompt_tiles)
    def _():
        o_ref[...] = xs_ref[...] + z


def _outproj(tm, w_out, xp, xs):
    m, d = tm.shape
    s, t = xp.shape[0], xs.shape[0]
    bm, bn = min(ROW_TILE, t), min(COL_TILE, d)
    npt = s // bm
    return pl.pallas_call(
        functools.partial(_outproj_kernel, n_prompt_tiles=npt),
        grid=(d // bn, m // bm),
        in_specs=[
            pl.BlockSpec((bm, d), lambda n, i: (i, 0)),
            pl.BlockSpec((d, bn), lambda n, i: (0, n)),
            pl.BlockSpec((bm, bn), lambda n, i: (jnp.minimum(i, npt - 1), n)),
            pl.BlockSpec((bm, bn), lambda n, i: (jnp.maximum(i - npt, 0), n)),
        ],
        out_specs=pl.BlockSpec((bm, bn), lambda n, i: (i, n)),
        out_shape=jax.ShapeDtypeStruct((m, d), F32),
        scratch_shapes=[pltpu.VMEM((d, bn), BF16)],
        compiler_params=_cparams(("arbitrary", "arbitrary")),
        name="out_proj",
    )(tm, w_out, xp, xs)


def _router_kernel(x_ref, g_ref, wr_ref, br_ref, route_ref, cnt_ref, whi_ref, wlo_ref,
                   carry_ref, *, n_groups, per_group):
    i = pl.program_id(0)

    @pl.when(i == 0)
    def _():
        w = wr_ref[...]
        hi = w.astype(BF16)
        whi_ref[...] = hi
        wlo_ref[...] = (w - hi.astype(F32)).astype(BF16)
        carry_ref[...] = jnp.zeros_like(carry_ref)

    hn = _rms(x_ref[...], g_ref[...])
    rows = hn.shape[0]
    hi = hn.astype(BF16)
    lo = (hn - hi.astype(F32)).astype(BF16)
    logits = (jnp.dot(hi, whi_ref[...], preferred_element_type=F32)
              + jnp.dot(hi, wlo_ref[...], preferred_element_type=F32)
              + jnp.dot(lo, whi_ref[...], preferred_element_type=F32)) + br_ref[...]
    lane = lax.broadcasted_iota(I32, logits.shape, 1)
    big = jnp.int32(LANES)

    lg = jnp.where(lane < n_groups, logits, -jnp.inf)
    mg = jnp.max(lg, axis=-1, keepdims=True)
    pg_sel = 1.0 / jnp.sum(jnp.exp(lg - mg), axis=-1, keepdims=True)
    gsel = jnp.min(jnp.where(lg == mg, lane, big), axis=-1, keepdims=True)

    first = n_groups + gsel * per_group
    emask = (lane >= first) & (lane < first + per_group)
    le = jnp.where(emask, logits, -jnp.inf)
    me = jnp.max(le, axis=-1, keepdims=True)
    ee = jnp.exp(le - me)
    pe = jnp.where(emask, ee / jnp.sum(ee, axis=-1, keepdims=True), -1.0)
    p1 = jnp.max(pe, axis=-1, keepdims=True)
    i1 = jnp.min(jnp.where(pe == p1, lane, big), axis=-1, keepdims=True)
    pe2 = jnp.where(lane == i1, -1.0, pe)
    p2 = jnp.max(pe2, axis=-1, keepdims=True)
    i2 = jnp.min(jnp.where(pe2 == p2, lane, big), axis=-1, keepdims=True)
    psum = p1 + p2
    w1 = p1 / psum * pg_sel
    w2 = p2 / psum * pg_sel
    e1 = i1 - n_groups
    e2 = i2 - n_groups

    oh1 = (lane == e1).astype(F32)
    oh2 = (lane == e2).astype(F32)
    ohs = oh1 + oh2
    ri = lax.broadcasted_iota(I32, (rows, rows), 0)
    ci = lax.broadcasted_iota(I32, (rows, rows), 1)
    below = (ci < ri).astype(BF16)
    before = jnp.dot(below, ohs.astype(BF16), preferred_element_type=F32) + carry_ref[...]
    r1 = jnp.sum(before * oh1, axis=-1, keepdims=True)
    r2 = jnp.sum(before * oh2, axis=-1, keepdims=True)
    carry_ref[...] = carry_ref[...] + jnp.sum(ohs, axis=0, keepdims=True)
    cnt_ref[...] = carry_ref[...]

    route = jnp.where(lane == 0, e1.astype(F32), 0.0)
    route = jnp.where(lane == 1, e2.astype(F32), route)
    route = jnp.where(lane == 2, w1, route)
    route = jnp.where(lane == 3, w2, route)
    route = jnp.where(lane == 4, r1, route)
    route = jnp.where(lane == 5, r2, route)
    route_ref[...] = route


def _router(x2, gain, wr, br, n_groups, per_group):
    m, d = x2.shape
    br_rows = min(ROUTE_ROWS, m)
    return pl.pallas_call(
        functools.partial(_router_kernel, n_groups=n_groups, per_group=per_group),
        grid=(m // br_rows,),
        in_specs=[
            pl.BlockSpec((br_rows, d), lambda i: (i, 0)),
            pl.BlockSpec((1, d), lambda i: (0, 0)),
            pl.BlockSpec((d, LANES), lambda i: (0, 0)),
            pl.BlockSpec((1, LANES), lambda i: (0, 0)),
        ],
        out_specs=[
            pl.BlockSpec((br_rows, LANES), lambda i: (i, 0)),
            pl.BlockSpec((1, LANES), lambda i: (0, 0)),
        ],
        out_shape=[
            jax.ShapeDtypeStruct((m, LANES), F32),
            jax.ShapeDtypeStruct((1, LANES), F32),
        ],
        scratch_shapes=[pltpu.VMEM((d, LANES), BF16), pltpu.VMEM((d, LANES), BF16),
                        pltpu.VMEM((1, LANES), F32)],
        compiler_params=_cparams(("arbitrary",)),
        name="norm2_router",
    )(x2, gain, wr, br)


def _pack_halves(x):
    half = x.shape[1] // 2
    hi = lax.bitcast_convert_type(x[:, :half].astype(BF16).astype(F32), U32)
    lo = lax.bitcast_convert_type(x[:, half:].astype(BF16).astype(F32), U32)
    return hi | (lo >> 16)


def _unpack_halves(p):
    first = lax.bitcast_convert_type(p & jnp.uint32(0xFFFF0000), F32)
    second = lax.bitcast_convert_type(p << 16, F32)
    return first, second


def _dispatch_kernel(dest_ref, x_ref, g_ref, xs_ref, buf_ref, sem, *, n_tiles):
    i = pl.program_id(0)
    tm = x_ref.shape[0]
    slot = i % 2

    def wait_rows(s):
        for k in range(TOP_K):
            pltpu.make_async_copy(buf_ref.at[s], xs_ref.at[pl.ds(0, tm)], sem.at[s]).wait()

    buf_ref[slot] = _pack_halves(_rms(x_ref[...], g_ref[...]))

    def start(r, c):
        for k in range(TOP_K):
            d = dest_ref[TOP_K * (i * tm + r) + k]
            pltpu.make_async_copy(buf_ref.at[slot, pl.ds(r, 1)], xs_ref.at[pl.ds(d, 1)],
                                  sem.at[slot]).start()
        return c

    lax.fori_loop(0, tm, start, 0, unroll=DMA_UNROLL)

    @pl.when(i > 0)
    def _():
        wait_rows(1 - slot)

    @pl.when(i == n_tiles - 1)
    def _():
        wait_rows(slot)


def _dispatch(dest_flat, x2, gain, n_rows):
    m, d = x2.shape
    tm = min(DISPATCH_ROWS, m)
    return pl.pallas_call(
        functools.partial(_dispatch_kernel, n_tiles=m // tm),
        grid_spec=pltpu.PrefetchScalarGridSpec(
            num_scalar_prefetch=1,
            grid=(m // tm,),
            in_specs=[
                pl.BlockSpec((tm, d), lambda i, dest: (i, 0)),
                pl.BlockSpec((1, d), lambda i, dest: (0, 0)),
            ],
            out_specs=pl.BlockSpec(memory_space=pl.ANY),
            scratch_shapes=[pltpu.VMEM((2, tm, d // 2), U32), pltpu.SemaphoreType.DMA((2,))],
        ),
        out_shape=jax.ShapeDtypeStruct((n_rows, d // 2), U32),
        compiler_params=_cparams(("arbitrary",)),
        name="moe_dispatch",
    )(dest_flat, x2, gain)


def _moe_kernel(be_ref, bidx_ref, nrows_ref, nv_ref, xs_ref, wg_ref, wu_ref, wd_ref, y_ref, acc_ref, *,
                n_split):
    i = pl.program_id(0)
    j = pl.program_id(1)

    @pl.when(i < nv_ref[0])
    def _():
        rows, half = xs_ref.shape
        live = lax.broadcasted_iota(I32, (rows, 1), 0) < nrows_ref[i]
        x1, x2 = _unpack_halves(xs_ref[...])
        x1 = jnp.where(live, x1, 0.0).astype(BF16)
        x2 = jnp.where(live, x2, 0.0).astype(BF16)

        def proj(w_ref):
            return (jnp.dot(x1, w_ref[0, :half, :].astype(BF16), preferred_element_type=F32)
                    + jnp.dot(x2, w_ref[0, half:, :].astype(BF16), preferred_element_type=F32))

        act = (jax.nn.silu(proj(wg_ref)) * proj(wu_ref)).astype(BF16)
        part = jnp.dot(act, wd_ref[0].astype(BF16), preferred_element_type=F32)

        @pl.when(j == 0)
        def _():
            acc_ref[...] = part

        @pl.when((j > 0) & (j < n_split - 1))
        def _():
            acc_ref[...] = acc_ref[...] + part

        @pl.when(j == n_split - 1)
        def _():
            y_ref[...] = _pack_halves(acc_ref[...] + part)


def _moe_experts(block_e, block_idx, block_rows, n_valid, xs, w_gate, w_up, w_down, n_blocks):
    half = xs.shape[1]
    d = 2 * half
    d_e = w_gate.shape[2]
    n_split = 2
    dh = d_e // n_split
    br = MOE_ROWS

    def col(i, j, nv):
        return jnp.where(i < nv[0], j, n_split - 1)

    return pl.pallas_call(
        functools.partial(_moe_kernel, n_split=n_split),
        grid_spec=pltpu.PrefetchScalarGridSpec(
            num_scalar_prefetch=4,
            grid=(n_blocks, n_split),
            in_specs=[
                pl.BlockSpec((br, half), lambda i, j, be, bi, nr, nv: (bi[i], 0)),
                pl.BlockSpec((1, d, dh), lambda i, j, be, bi, nr, nv: (be[i], 0, col(i, j, nv))),
                pl.BlockSpec((1, d, dh), lambda i, j, be, bi, nr, nv: (be[i], 0, col(i, j, nv))),
                pl.BlockSpec((1, dh, d), lambda i, j, be, bi, nr, nv: (be[i], col(i, j, nv), 0)),
            ],
            out_specs=pl.BlockSpec((br, half), lambda i, j, be, bi, nr, nv: (bi[i], 0)),
            scratch_shapes=[pltpu.VMEM((br, d), F32)],
        ),
        out_shape=jax.ShapeDtypeStruct(xs.shape, U32),
        compiler_params=_cparams(("arbitrary", "arbitrary")),
        name="moe_experts",
    )(block_e, block_idx, block_rows, n_valid, xs, w_gate, w_up, w_down)


def _combine_kernel(dest_ref, x_ref, route_ref, y_ref, op_ref, os_ref, buf_ref, sem, *,
                    n_prompt_tiles, n_tiles):
    i = pl.program_id(0)
    tm = x_ref.shape[0]
    slot = i % 2

    def gather(tile, to_slot):
        def start(r, c):
            for k in range(TOP_K):
                d = dest_ref[TOP_K * (tile * tm + r) + k]
                pltpu.make_async_copy(y_ref.at[pl.ds(d, 1)], buf_ref.at[to_slot, k, pl.ds(r, 1)],
                                      sem.at[to_slot]).start()
            return c

        lax.fori_loop(0, tm, start, 0, unroll=DMA_UNROLL)

    @pl.when(i == 0)
    def _():
        gather(0, 0)

    @pl.when(i + 1 < n_tiles)
    def _():
        gather(i + 1, 1 - slot)

    for k in range(TOP_K):
        pltpu.make_async_copy(y_ref.at[pl.ds(0, tm)], buf_ref.at[slot, k], sem.at[slot]).wait()

    half = x_ref.shape[1] // 2
    first, second = x_ref[:, :half], x_ref[:, half:]
    for k in range(TOP_K):
        y1, y2 = _unpack_halves(buf_ref[slot, k])
        w = route_ref[:, 2 + k:3 + k]
        first = first + w * y1
        second = second + w * y2
    out = jnp.concatenate([first, second], axis=1)

    @pl.when(i < n_prompt_tiles)
    def _():
        op_ref[...] = out

    @pl.when(i >= n_prompt_tiles)
    def _():
        os_ref[...] = out


def _combine(dest_flat, x2, route, y, n_prompt_rows):
    m, d = x2.shape
    tm = COMBINE_ROWS
    npt = n_prompt_rows // tm
    return pl.pallas_call(
        functools.partial(_combine_kernel, n_prompt_tiles=npt, n_tiles=m // tm),
        grid_spec=pltpu.PrefetchScalarGridSpec(
            num_scalar_prefetch=1,
            grid=(m // tm,),
            in_specs=[
                pl.BlockSpec((tm, d), lambda i, dest: (i, 0)),
                pl.BlockSpec((tm, LANES), lambda i, dest: (i, 0)),
                pl.BlockSpec(memory_space=pl.ANY),
            ],
            out_specs=[
                pl.BlockSpec((tm, d), lambda i, dest: (jnp.minimum(i, npt - 1), 0)),
                pl.BlockSpec((tm, d), lambda i, dest: (jnp.maximum(i - npt, 0), 0)),
            ],
            scratch_shapes=[pltpu.VMEM((2, TOP_K, tm, d // 2), U32), pltpu.SemaphoreType.DMA((2,))],
        ),
        out_shape=[
            jax.ShapeDtypeStruct((n_prompt_rows, d), F32),
            jax.ShapeDtypeStruct((m - n_prompt_rows, d), F32),
        ],
        compiler_params=_cparams(("arbitrary",)),
        name="moe_combine",
    )(dest_flat, x2, route, y)


def _rope_tables(positions, head_dim):
    rot_dim = head_dim // 4
    half = rot_dim // 2
    inv_freq = jnp.power(ROPE_THETA, -jnp.arange(half, dtype=F32) * 2.0 / rot_dim)
    ang = positions.astype(F32)[:, None] * inv_freq[None, :]
    cos, sin = lax.optimization_barrier((jnp.cos(ang), jnp.sin(ang)))
    m = positions.shape[0]
    zeros = lambda n: jnp.zeros((m, n), F32)
    cos_h = jnp.concatenate([cos, cos, jnp.ones((m, head_dim - rot_dim), F32)], axis=1)
    sa_h = jnp.concatenate([-sin, zeros(head_dim - half)], axis=1)
    sb_h = jnp.concatenate([zeros(half), sin, zeros(head_dim - rot_dim)], axis=1)
    reps = LANES // head_dim
    return tuple(jnp.tile(a, (1, reps)) for a in (cos_h, sa_h, sb_h)), half


def _layer(xp, xs, cache_k, cache_v, norm1_g, w_in, gmlp_norm_g, w_s, b_s, q_norm_g, k_norm_g,
           sinks, p_a, p_b, w_out, norm2_g, w_rg, b_rg, w_re, b_re, w_gate_e, w_up_e, w_down_e):
    s, d = xp.shape
    dec_rows = xs.shape[0]
    dec_batch = cache_k.shape[0]
    t = dec_rows // dec_batch
    m = s + dec_rows
    d_a = gmlp_norm_g.shape[0]
    head_dim = q_norm_g.shape[0]
    n_heads = sinks.shape[0]
    q_dim = n_heads * head_dim
    kv_dim = cache_k.shape[2] * cache_k.shape[3]
    n_groups, per_group = w_re.shape[1], w_re.shape[2]
    n_experts = n_groups * per_group

    h = _norm1(xp, xs, norm1_g[None, :])
    ua = _inproj_act(h, w_in, 0, 2 * d_a, jax.nn.gelu, BF16, "inproj_gelu")
    positions = jnp.concatenate(
        [jnp.arange(s, dtype=I32), jnp.tile(PAST_LEN + jnp.arange(t, dtype=I32), dec_batch)])
    (cos_t, sa_t, sb_t), rot_half = _rope_tables(positions, head_dim)
    gain_row = jnp.concatenate([jnp.tile(q_norm_g, n_heads), jnp.tile(k_norm_g, kv_dim // head_dim),
                                jnp.ones((kv_dim,), F32)])[None, :]
    qkv = _inproj_qkv(h, w_in, 2 * d_a, q_dim, kv_dim, gain_row, cos_t, sa_t, sb_t, head_dim, rot_half)
    gates = _inproj_act(h, w_in, 2 * d_a + q_dim + 2 * kv_dim, 2 * d, jax.nn.sigmoid, BF16,
                        "inproj_gate")

    rows = w_s.shape[1]
    reps = rows // t
    w_sample = jnp.tile(w_s[:, :t, :t], (1, reps, reps))
    b_sample = jnp.tile(b_s[:, :t], (1, reps))
    o_a, vn_s = _gmlp(ua, w_s, w_sample, b_s.T, b_sample.T, gmlp_norm_g[None, :], s, t)
    o_b = _attention(qkv, cache_k.reshape(-1, kv_dim), cache_v.reshape(-1, kv_dim), sinks[None, :],
                     s, q_dim, kv_dim, head_dim, n_heads)

    merged = _merge(o_a, o_b, p_a, p_b, gates)
    x2 = _outproj(merged, w_out, xp, xs)

    pad = LANES - n_groups - n_experts
    wr = jnp.concatenate([w_rg, w_re.reshape(d, n_experts), jnp.zeros((d, pad), F32)], axis=1)
    br = jnp.concatenate([b_rg, b_re.reshape(n_experts), jnp.zeros((pad,), F32)])[None, :]
    route, counts = _router(x2, norm2_g[None, :], wr, br, n_groups, per_group)
    e_idx = route[:, 0:TOP_K].astype(I32)
    rank = route[:, 4:4 + TOP_K].astype(I32)
    counts = counts[0, :n_experts].astype(I32)

    blk = MOE_ROWS
    n_blocks = -(-(m * TOP_K) // blk) + n_experts
    padded = ((counts + blk - 1) // blk) * blk
    pend = jnp.cumsum(padded)
    pstart = pend - padded
    dest = (pstart[e_idx] + rank).reshape(-1).astype(I32)
    n_valid = (pend[-1] // blk).astype(I32)
    ids = jnp.arange(n_blocks, dtype=I32)
    block_idx = jnp.minimum(ids, n_valid - 1)
    block_e = jnp.minimum(jnp.searchsorted(pend, block_idx * blk, side="right"), n_experts - 1).astype(I32)
    block_rows = jnp.clip(counts[block_e] - (block_idx * blk - pstart[block_e]), 0, blk).astype(I32)

    xs_sorted = _dispatch(dest, x2, norm2_g[None, :], n_blocks * blk)
    y_sorted = _moe_experts(block_e, block_idx, block_rows, n_valid[None], xs_sorted,
                            w_gate_e, w_up_e, w_down_e, n_blocks)
    yp, ys = _combine(dest, x2, route, y_sorted, s)
    return yp, ys, qkv, vn_s


def kernel(x_prompt, x_sample, cache_k_win, cache_v_win, norm1_g, w_in, gmlp_norm_g, w_s, b_s,
           q_norm_g, k_norm_g, sinks, p_a, p_b, w_out, norm2_g, w_rg, b_rg, w_re, b_re,
           w_gate_e, w_up_e, w_down_e):
    depth = norm1_g.shape[0]
    assert depth == 1, "weights of one layer are expected"
    batch, s, d = x_prompt.shape
    assert batch == 1
    dec_batch, t, _ = x_sample.shape
    head_dim = q_norm_g.shape[-1]
    n_kv = cache_k_win.shape[3]
    q_dim = sinks.shape[-1] * head_dim
    kv_dim = n_kv * head_dim
    keep = min(cache_k_win.shape[2], s)

    l = 0
    yp, ys, qkv, vn_s = _layer(
        x_prompt.reshape(s, d), x_sample.reshape(dec_batch * t, d), cache_k_win[l], cache_v_win[l],
        norm1_g[l], w_in[l], gmlp_norm_g[l], w_s[l], b_s[l], q_norm_g[l], k_norm_g[l], sinks[l],
        p_a[l], p_b[l], w_out[l], norm2_g[l], w_rg[l], b_rg[l], w_re[l], b_re[l],
        w_gate_e[l], w_up_e[l], w_down_e[l])

    k_all = qkv[:, q_dim:q_dim + kv_dim]
    v_all = qkv[:, q_dim + kv_dim:]
    k_win_p = k_all[s - keep:s].reshape(1, batch, keep, n_kv, head_dim)
    v_win_p = v_all[s - keep:s].reshape(1, batch, keep, n_kv, head_dim)
    k_new_s = k_all[s:].reshape(1, dec_batch, t, n_kv, head_dim)
    v_new_s = v_all[s:].reshape(1, dec_batch, t, n_kv, head_dim)
    gv_s = vn_s.reshape(1, dec_batch, t, -1)
    return (yp.reshape(batch, s, d), ys.reshape(dec_batch, t, d), k_win_p, v_win_p, k_new_s, v_new_s, gv_s)
```

```python
import functools

import jax
import jax.numpy as jnp
from jax import lax
from jax.experimental import pallas as pl
from jax.experimental.pallas import tpu as pltpu

F32 = jnp.float32
BF16 = jnp.bfloat16
I32 = jnp.int32
U32 = jnp.uint32

EPS = 1e-6
PAST_LEN = 1024
CHUNK = 64
ROPE_THETA = 500000.0
TOP_K = 2
LANES = 128

VMEM_LIMIT_BYTES = 56 * 1024 * 1024

ROW_TILE = 1024
WIDE_ROW_TILE = 1536
COL_TILE = 512
NORM_ROWS = 256
ROUTE_ROWS = 256
MOE_ROWS = 320
COMBINE_ROWS = 128
DISPATCH_ROWS = 256
DMA_UNROLL = 8


def _cparams(sem):
    return pltpu.CompilerParams(dimension_semantics=sem, vmem_limit_bytes=VMEM_LIMIT_BYTES)


def _cast_rows(src_ref, dst_ref, rows):
    n = src_ref.shape[0] // rows

    def body(r, c):
        sl = pl.ds(pl.multiple_of(r * rows, rows), rows)
        dst_ref[sl, :] = src_ref[sl, :].astype(dst_ref.dtype)
        return c

    lax.fori_loop(0, n, body, 0)


def _rms(x, gain):
    ms = jnp.mean(x * x, axis=-1, keepdims=True)
    return x * lax.rsqrt(ms + EPS) * gain


def _norm1_kernel(xp_ref, xs_ref, g_ref, h_ref, *, n_prompt_blocks):
    i = pl.program_id(0)

    @pl.when(i < n_prompt_blocks)
    def _():
        h_ref[...] = _rms(xp_ref[...], g_ref[...]).astype(h_ref.dtype)

    @pl.when(i >= n_prompt_blocks)
    def _():
        h_ref[...] = _rms(xs_ref[...], g_ref[...]).astype(h_ref.dtype)


def _norm1(xp, xs, gain):
    s, d = xp.shape
    t = xs.shape[0]
    br = min(NORM_ROWS, t)
    nbp, nbs = s // br, t // br
    return pl.pallas_call(
        functools.partial(_norm1_kernel, n_prompt_blocks=nbp),
        grid=(nbp + nbs,),
        in_specs=[
            pl.BlockSpec((br, d), lambda i: (jnp.minimum(i, nbp - 1), 0)),
            pl.BlockSpec((br, d), lambda i: (jnp.maximum(i - nbp, 0), 0)),
            pl.BlockSpec((1, d), lambda i: (0, 0)),
        ],
        out_specs=pl.BlockSpec((br, d), lambda i: (i, 0)),
        out_shape=jax.ShapeDtypeStruct((s + t, d), BF16),
        compiler_params=_cparams(("arbitrary",)),
        name="norm1",
    )(xp, xs, gain)


def _inproj_act_kernel(h_ref, w_ref, o_ref, wb_ref, *, act):
    @pl.when(pl.program_id(1) == 0)
    def _():
        _cast_rows(w_ref, wb_ref, 256)

    z = jnp.dot(h_ref[...], wb_ref[...], preferred_element_type=F32)
    o_ref[...] = act(z).astype(o_ref.dtype)


def _inproj_act(h, w, col0, ncols, act, out_dtype, name):
    m, d = h.shape
    bm = WIDE_ROW_TILE if m % WIDE_ROW_TILE == 0 else min(ROW_TILE, m)
    bn = min(COL_TILE, ncols)
    assert col0 % bn == 0 and ncols % bn == 0 and m % bm == 0
    off = col0 // bn
    return pl.pallas_call(
        functools.partial(_inproj_act_kernel, act=act),
        grid=(ncols // bn, m // bm),
        in_specs=[
            pl.BlockSpec((bm, d), lambda n, i: (i, 0)),
            pl.BlockSpec((d, bn), lambda n, i: (0, n + off)),
        ],
        out_specs=pl.BlockSpec((bm, bn), lambda n, i: (i, n)),
        out_shape=jax.ShapeDtypeStruct((m, ncols), out_dtype),
        scratch_shapes=[pltpu.VMEM((d, bn), BF16)],
        compiler_params=_cparams(("arbitrary", "arbitrary")),
        name=name,
    )(h, w)


def _inproj_qkv_kernel(h_ref, w_ref, gain_ref, cos_ref, sa_ref, sb_ref, o_ref, wb_ref, *,
                       n_norm_tiles, head_dim, rot_half):
    n = pl.program_id(0)

    @pl.when(pl.program_id(1) == 0)
    def _():
        _cast_rows(w_ref, wb_ref, 256)

    z = jnp.dot(h_ref[...], wb_ref[...], preferred_element_type=F32)
    bn = z.shape[1]

    @pl.when(n < n_norm_tiles)
    def _():
        r = lax.broadcasted_iota(I32, (bn, bn), 0) // head_dim
        c = lax.broadcasted_iota(I32, (bn, bn), 1) // head_dim
        seg = (r == c).astype(BF16)
        ssq = jnp.dot((z * z).astype(BF16), seg, preferred_element_type=F32)
        y = z * lax.rsqrt(ssq * (1.0 / head_dim) + EPS) * gain_ref[...]
        reps = bn // cos_ref.shape[1]
        cosv = jnp.tile(cos_ref[...], (1, reps))
        sa = jnp.tile(sa_ref[...], (1, reps))
        sb = jnp.tile(sb_ref[...], (1, reps))
        y = y * cosv + pltpu.roll(y, bn - rot_half, 1) * sa + pltpu.roll(y, rot_half, 1) * sb
        o_ref[...] = y

    @pl.when(n >= n_norm_tiles)
    def _():
        o_ref[...] = z


def _inproj_qkv(h, w, col0, q_dim, kv_dim, gain_row, cos_t, sa_t, sb_t, head_dim, rot_half):
    m, d = h.shape
    ncols = q_dim + 2 * kv_dim
    bm, bn = min(ROW_TILE, m), kv_dim
    assert col0 % bn == 0 and q_dim % bn == 0 and bn % LANES == 0 and m % bm == 0
    off = col0 // bn
    n_norm_tiles = (q_dim + kv_dim) // bn
    tw = cos_t.shape[1]
    return pl.pallas_call(
        functools.partial(_inproj_qkv_kernel, n_norm_tiles=n_norm_tiles, head_dim=head_dim,
                          rot_half=rot_half),
        grid=(ncols // bn, m // bm),
        in_specs=[
            pl.BlockSpec((bm, d), lambda n, i: (i, 0)),
            pl.BlockSpec((d, bn), lambda n, i: (0, n + off)),
            pl.BlockSpec((1, bn), lambda n, i: (0, n)),
            pl.BlockSpec((bm, tw), lambda n, i: (i, 0)),
            pl.BlockSpec((bm, tw), lambda n, i: (i, 0)),
            pl.BlockSpec((bm, tw), lambda n, i: (i, 0)),
        ],
        out_specs=pl.BlockSpec((bm, bn), lambda n, i: (i, n)),
        out_shape=jax.ShapeDtypeStruct((m, ncols), F32),
        scratch_shapes=[pltpu.VMEM((d, bn), BF16)],
        compiler_params=_cparams(("arbitrary", "arbitrary")),
        name="inproj_qkv",
    )(h, w, gain_row, cos_t, sa_t, sb_t)


def _gmlp_kernel(u_ref, va_ref, wp_ref, ws_ref, bp_ref, bs_ref, gain_ref, o_ref, vn_ref, *,
                 n_prompt_blocks, groups, sample_len):
    i = pl.program_id(0)
    rows, d_a = u_ref.shape
    gw = d_a // groups
    ri = lax.broadcasted_iota(I32, (rows, rows), 0)
    ci = lax.broadcasted_iota(I32, (rows, rows), 1)

    def run(w_ref, b_ref, sub, emit_vn):
        vn = _rms(va_ref[...].astype(F32), gain_ref[...])
        if emit_vn:
            vn_ref[...] = vn
        vb = vn.astype(BF16)
        mask = (ci <= ri) & ((ri // sub) == (ci // sub))
        for g in range(groups):
            sl = slice(g * gw, (g + 1) * gw)
            wg = jnp.where(mask, w_ref[g], 0.0).astype(BF16)
            s = jnp.dot(wg, vb[:, sl], preferred_element_type=F32) + b_ref[:, g:g + 1]
            o_ref[:, sl] = (u_ref[:, sl].astype(F32) * s).astype(o_ref.dtype)

    @pl.when(i < n_prompt_blocks)
    def _():
        run(wp_ref, bp_ref, rows, False)

    @pl.when(i >= n_prompt_blocks)
    def _():
        run(ws_ref, bs_ref, sample_len, True)


def _gmlp(ua, w_prompt, w_sample, b_prompt, b_sample, gain, n_prompt_rows, sample_len):
    m, two_da = ua.shape
    d_a = two_da // 2
    groups, rows, _ = w_prompt.shape
    nbp = n_prompt_rows // rows
    nb = m // rows
    full = lambda i: (0, 0, 0)
    return pl.pallas_call(
        functools.partial(_gmlp_kernel, n_prompt_blocks=nbp, groups=groups, sample_len=sample_len),
        grid=(nb,),
        in_specs=[
            pl.BlockSpec((rows, d_a), lambda i: (i, 0)),
            pl.BlockSpec((rows, d_a), lambda i: (i, 1)),
            pl.BlockSpec((groups, rows, rows), full),
            pl.BlockSpec((groups, rows, rows), full),
            pl.BlockSpec((rows, groups), lambda i: (0, 0)),
            pl.BlockSpec((rows, groups), lambda i: (0, 0)),
            pl.BlockSpec((1, d_a), lambda i: (0, 0)),
        ],
        out_specs=[
            pl.BlockSpec((rows, d_a), lambda i: (i, 0)),
            pl.BlockSpec((rows, d_a), lambda i: (jnp.maximum(i - nbp, 0), 0)),
        ],
        out_shape=[
            jax.ShapeDtypeStruct((m, d_a), BF16),
            jax.ShapeDtypeStruct((m - n_prompt_rows, d_a), F32),
        ],
        compiler_params=_cparams(("arbitrary",)),
        name="gmlp_spatial",
    )(ua, ua, w_prompt, w_sample, b_prompt, b_sample, gain)


def _attn_kernel(bias_ref, q_ref, k0_ref, k1_ref, k2_ref, v0_ref, v1_ref, v2_ref, ck_ref, cv_ref,
                 o_ref, s_ref, e_ref, *, n_prompt_chunks, tiles_per_kv, head_dim):
    c = pl.program_id(0)
    t = q_ref.shape[0]
    kv_dim = k2_ref.shape[1]
    lkp = bias_ref.shape[1] // 2
    scale = head_dim ** -0.5
    nt = (((1,), (1,)), ((), ()))

    def run(k_parts, v_parts, first_key_chunk):
        k = jnp.concatenate(k_parts, axis=0)
        v = jnp.concatenate(v_parts, axis=0)
        lk = k.shape[0]
        lower = lax.broadcasted_iota(I32, (lk, LANES), 1) < head_dim
        zpad = jnp.zeros((lkp - lk, LANES), F32)
        rr = lax.broadcasted_iota(I32, (2 * lkp, LANES), 0) < lkp
        rl = lax.broadcasted_iota(I32, (2 * lkp, LANES), 1) < head_dim
        ones_sel = (rr == rl).astype(BF16)
        chunk_bias = None
        if first_key_chunk is not None:
            col = lax.broadcasted_iota(I32, (1, 2 * lkp), 1)
            key_chunk = first_key_chunk + (col % lkp) // t
            chunk_bias = jnp.where(key_chunk >= 0, 0.0, -jnp.inf)

        def doubled(tile, rolled, x):
            lo_src, hi_src = (tile, rolled) if x == 0 else (rolled, tile)
            return jnp.concatenate([jnp.where(lower, lo_src, 0.0), zpad,
                                    jnp.where(lower, 0.0, hi_src), zpad], axis=0).astype(BF16)

        n_kv = kv_dim // head_dim
        tiles_of = lambda j: [j * tiles_per_kv + a for a in range(tiles_per_kv)]

        for b in range(kv_dim // LANES):
            kt = k[:, b * LANES:(b + 1) * LANES]
            kr = pltpu.roll(kt, head_dim, 1)
            for x in range(2):
                j = 2 * b + x
                q2 = jnp.concatenate([q_ref[:, a * LANES:(a + 1) * LANES] for a in tiles_of(j)], axis=0)
                q2 = (q2 * scale).astype(BF16)
                rows = []
                for a in tiles_of(j):
                    row = bias_ref[a:a + 1, :]
                    if chunk_bias is not None:
                        row = row + chunk_bias
                    rows.append(jnp.broadcast_to(row, (t, 2 * lkp)))
                s_ref[j] = (lax.dot_general(q2, doubled(kt, kr, x), nt, preferred_element_type=F32)
                            + jnp.concatenate(rows, axis=0))

        for j in range(n_kv):
            for hs in (slice(0, lkp), slice(lkp, 2 * lkp)):
                sh = s_ref[j, :, hs]
                e_ref[j, :, hs] = jnp.exp(sh - jnp.max(sh, axis=-1, keepdims=True)).astype(BF16)

        for b in range(kv_dim // LANES):
            vt = v[:, b * LANES:(b + 1) * LANES]
            vr = pltpu.roll(vt, head_dim, 1)
            for x in range(2):
                j = 2 * b + x
                rhs = jnp.concatenate([doubled(vt, vr, x), ones_sel], axis=1)
                r = jnp.dot(e_ref[j], rhs, preferred_element_type=F32)
                o = r[:, :LANES] / r[:, LANES:]
                for i, a in enumerate(tiles_of(j)):
                    o_ref[:, a * LANES:(a + 1) * LANES] = o[i * t:(i + 1) * t].astype(o_ref.dtype)

    @pl.when(c < n_prompt_chunks)
    def _():
        run([k0_ref[...], k1_ref[...], k2_ref[...]], [v0_ref[...], v1_ref[...], v2_ref[...]], c - 2)

    @pl.when(c >= n_prompt_chunks)
    def _():
        run([ck_ref[...], k2_ref[...]], [cv_ref[...], v2_ref[...]], None)


def _attention(qkv, cache_k, cache_v, sinks, n_prompt_rows, q_dim, kv_dim, head_dim, n_heads):
    m = qkv.shape[0]
    t = CHUNK
    nc = n_prompt_rows // t
    n_kv_heads = kv_dim // head_dim
    window = cache_k.shape[0] // ((m - n_prompt_rows) // t)
    kcol, vcol = q_dim // kv_dim, q_dim // kv_dim + 1
    gqa = n_heads // n_kv_heads
    assert 2 * head_dim == LANES and gqa % 2 == 0 and kv_dim % LANES == 0

    lk = window + t
    lkp = -(-(lk + 1) // LANES) * LANES
    pair = sinks.reshape(n_heads // 2, 2, 1)
    bias = jnp.concatenate([jnp.zeros((n_heads // 2, 2, lk), F32), pair,
                            jnp.full((n_heads // 2, 2, lkp - lk - 1), -jnp.inf, F32)], axis=2)
    bias = bias.reshape(n_heads // 2, 2 * lkp)

    def kv_spec(back, col):
        return pl.BlockSpec((t, kv_dim), lambda c: (jnp.maximum(c - back, 0), col))

    cache_spec = pl.BlockSpec((window, kv_dim), lambda c: (jnp.maximum(c - nc, 0), 0))
    return pl.pallas_call(
        functools.partial(_attn_kernel, n_prompt_chunks=nc, tiles_per_kv=gqa // 2, head_dim=head_dim),
        grid=(m // t,),
        in_specs=[
            pl.BlockSpec((n_heads // 2, 2 * lkp), lambda c: (0, 0)),
            pl.BlockSpec((t, q_dim), lambda c: (c, 0)),
            kv_spec(2, kcol), kv_spec(1, kcol), kv_spec(0, kcol),
            kv_spec(2, vcol), kv_spec(1, vcol), kv_spec(0, vcol),
            cache_spec, cache_spec,
        ],
        out_specs=pl.BlockSpec((t, q_dim), lambda c: (c, 0)),
        out_shape=jax.ShapeDtypeStruct((m, q_dim), BF16),
        scratch_shapes=[pltpu.VMEM((n_kv_heads, gqa // 2 * t, 2 * lkp), F32),
                        pltpu.VMEM((n_kv_heads, gqa // 2 * t, 2 * lkp), BF16)],
        compiler_params=_cparams(("arbitrary",)),
        name="window_attention",
    )(bias, qkv, qkv, qkv, qkv, qkv, qkv, qkv, cache_k, cache_v)


def _merge_kernel(oa_ref, ob_ref, pa_ref, pb_ref, ga_ref, gb_ref, o_ref, pab_ref, pbb_ref):
    @pl.when(pl.program_id(1) == 0)
    def _():
        _cast_rows(pa_ref, pab_ref, 256)
        _cast_rows(pb_ref, pbb_ref, 256)

    a = jnp.dot(oa_ref[...], pab_ref[...], preferred_element_type=F32)
    b = jnp.dot(ob_ref[...], pbb_ref[...], preferred_element_type=F32)
    o_ref[...] = (ga_ref[...].astype(F32) * a + gb_ref[...].astype(F32) * b).astype(o_ref.dtype)


def _merge(o_a, o_b, p_a, p_b, gates):
    m, d_a = o_a.shape
    q_dim = o_b.shape[1]
    d = p_a.shape[1]
    bm, bn = min(ROW_TILE, m), min(COL_TILE, d)
    goff = d // bn
    return pl.pallas_call(
        _merge_kernel,
        grid=(d // bn, m // bm),
        in_specs=[
            pl.BlockSpec((bm, d_a), lambda n, i: (i, 0)),
            pl.BlockSpec((bm, q_dim), lambda n, i: (i, 0)),
            pl.BlockSpec((d_a, bn), lambda n, i: (0, n)),
            pl.BlockSpec((q_dim, bn), lambda n, i: (0, n)),
            pl.BlockSpec((bm, bn), lambda n, i: (i, n)),
            pl.BlockSpec((bm, bn), lambda n, i: (i, n + goff)),
        ],
        out_specs=pl.BlockSpec((bm, bn), lambda n, i: (i, n)),
        out_shape=jax.ShapeDtypeStruct((m, d), BF16),
        scratch_shapes=[pltpu.VMEM((d_a, bn), BF16), pltpu.VMEM((q_dim, bn), BF16)],
        compiler_params=_cparams(("arbitrary", "arbitrary")),
        name="merge_proj",
    )(o_a, o_b, p_a, p_b, gates, gates)


def _outproj_kernel(t_ref, w_ref, xp_ref, xs_ref, o_ref, wb_ref, *, n_prompt_tiles):
    i = pl.program_id(1)

    @pl.when(i == 0)
    def _():
        _cast_rows(w_ref, wb_ref, 256)

    z = jnp.dot(t_ref[...], wb_ref[...], preferred_element_type=F32)

    @pl.when(i < n_prompt_tiles)
    def _():
        o_ref[...] = xp_ref[...] + z

    @pl.when(i >= n_prompt_tiles)
    def _():
        o_ref[...] = xs_ref[...] + z


def _outproj(tm, w_out, xp, xs):
    m, d = tm.shape
    s, t = xp.shape[0], xs.shape[0]
    bm, bn = min(ROW_TILE, t), min(COL_TILE, d)
    npt = s // bm
    return pl.pallas_call(
        functools.partial(_outproj_kernel, n_prompt_tiles=npt),
        grid=(d // bn, m // bm),
        in_specs=[
            pl.BlockSpec((bm, d), lambda n, i: (i, 0)),
            pl.BlockSpec((d, bn), lambda n, i: (0, n)),
            pl.BlockSpec((bm, bn), lambda n, i: (jnp.minimum(i, npt - 1), n)),
            pl.BlockSpec((bm, bn), lambda n, i: (jnp.maximum(i - npt, 0), n)),
        ],
        out_specs=pl.BlockSpec((bm, bn), lambda n, i: (i, n)),
        out_shape=jax.ShapeDtypeStruct((m, d), F32),
        scratch_shapes=[pltpu.VMEM((d, bn), BF16)],
        compiler_params=_cparams(("arbitrary", "arbitrary")),
        name="out_proj",
    )(tm, w_out, xp, xs)


def _router_kernel(x_ref, g_ref, wr_ref, br_ref, route_ref, cnt_ref, whi_ref, wlo_ref,
                   carry_ref, *, n_groups, per_group):
    i = pl.program_id(0)

    @pl.when(i == 0)
    def _():
        w = wr_ref[...]
        hi = w.astype(BF16)
        whi_ref[...] = hi
        wlo_ref[...] = (w - hi.astype(F32)).astype(BF16)
        carry_ref[...] = jnp.zeros_like(carry_ref)

    hn = _rms(x_ref[...], g_ref[...])
    rows = hn.shape[0]
    hi = hn.astype(BF16)
    lo = (hn - hi.astype(F32)).astype(BF16)
    logits = (jnp.dot(hi, whi_ref[...], preferred_element_type=F32)
              + jnp.dot(hi, wlo_ref[...], preferred_element_type=F32)
              + jnp.dot(lo, whi_ref[...], preferred_element_type=F32)) + br_ref[...]
    lane = lax.broadcasted_iota(I32, logits.shape, 1)
    big = jnp.int32(LANES)

    lg = jnp.where(lane < n_groups, logits, -jnp.inf)
    mg = jnp.max(lg, axis=-1, keepdims=True)
    pg_sel = 1.0 / jnp.sum(jnp.exp(lg - mg), axis=-1, keepdims=True)
    gsel = jnp.min(jnp.where(lg == mg, lane, big), axis=-1, keepdims=True)

    first = n_groups + gsel * per_group
    emask = (lane >= first) & (lane < first + per_group)
    le = jnp.where(emask, logits, -jnp.inf)
    me = jnp.max(le, axis=-1, keepdims=True)
    ee = jnp.exp(le - me)
    pe = jnp.where(emask, ee / jnp.sum(ee, axis=-1, keepdims=True), -1.0)
    p1 = jnp.max(pe, axis=-1, keepdims=True)
    i1 = jnp.min(jnp.where(pe == p1, lane, big), axis=-1, keepdims=True)
    pe2 = jnp.where(lane == i1, -1.0, pe)
    p2 = jnp.max(pe2, axis=-1, keepdims=True)
    i2 = jnp.min(jnp.where(pe2 == p2, lane, big), axis=-1, keepdims=True)
    psum = p1 + p2
    w1 = p1 / psum * pg_sel
    w2 = p2 / psum * pg_sel
    e1 = i1 - n_groups
    e2 = i2 - n_groups

    oh1 = (lane == e1).astype(F32)
    oh2 = (lane == e2).astype(F32)
    ohs = oh1 + oh2
    ri = lax.broadcasted_iota(I32, (rows, rows), 0)
    ci = lax.broadcasted_iota(I32, (rows, rows), 1)
    below = (ci < ri).astype(BF16)
    before = jnp.dot(below, ohs.astype(BF16), preferred_element_type=F32) + carry_ref[...]
    r1 = jnp.sum(before * oh1, axis=-1, keepdims=True)
    r2 = jnp.sum(before * oh2, axis=-1, keepdims=True)
    carry_ref[...] = carry_ref[...] + jnp.sum(ohs, axis=0, keepdims=True)
    cnt_ref[...] = carry_ref[...]

    route = jnp.where(lane == 0, e1.astype(F32), 0.0)
    route = jnp.where(lane == 1, e2.astype(F32), route)
    route = jnp.where(lane == 2, w1, route)
    route = jnp.where(lane == 3, w2, route)
    route = jnp.where(lane == 4, r1, route)
    route = jnp.where(lane == 5, r2, route)
    route_ref[...] = route


def _router(x2, gain, wr, br, n_groups, per_group):
    m, d = x2.shape
    br_rows = min(ROUTE_ROWS, m)
    return pl.pallas_call(
        functools.partial(_router_kernel, n_groups=n_groups, per_group=per_group),
        grid=(m // br_rows,),
        in_specs=[
            pl.BlockSpec((br_rows, d), lambda i: (i, 0)),
            pl.BlockSpec((1, d), lambda i: (0, 0)),
            pl.BlockSpec((d, LANES), lambda i: (0, 0)),
            pl.BlockSpec((1, LANES), lambda i: (0, 0)),
        ],
        out_specs=[
            pl.BlockSpec((br_rows, LANES), lambda i: (i, 0)),
            pl.BlockSpec((1, LANES), lambda i: (0, 0)),
        ],
        out_shape=[
            jax.ShapeDtypeStruct((m, LANES), F32),
            jax.ShapeDtypeStruct((1, LANES), F32),
        ],
        scratch_shapes=[pltpu.VMEM((d, LANES), BF16), pltpu.VMEM((d, LANES), BF16),
                        pltpu.VMEM((1, LANES), F32)],
        compiler_params=_cparams(("arbitrary",)),
        name="norm2_router",
    )(x2, gain, wr, br)


def _pack_halves(x):
    half = x.shape[1] // 2
    hi = lax.bitcast_convert_type(x[:, :half].astype(BF16).astype(F32), U32)
    lo = lax.bitcast_convert_type(x[:, half:].astype(BF16).astype(F32), U32)
    return hi | (lo >> 16)


def _unpack_halves(p):
    first = lax.bitcast_convert_type(p & jnp.uint32(0xFFFF0000), F32)
    second = lax.bitcast_convert_type(p << 16, F32)
    return first, second


def _dispatch_kernel(dest_ref, x_ref, g_ref, xs_ref, buf_ref, sem, *, n_tiles):
    i = pl.program_id(0)
    tm = x_ref.shape[0]
    n_tokens = n_tiles * tm
    slot = i % 2

    def wait_rows(s):
        for k in range(TOP_K):
            pltpu.make_async_copy(buf_ref.at[s], xs_ref.at[pl.ds(0, tm)], sem.at[s]).wait()

    buf_ref[slot] = _pack_halves(_rms(x_ref[...], g_ref[...]))

    def start(r, c):
        for k in range(TOP_K):
            d = dest_ref[k * n_tokens + i * tm + r]
            pltpu.make_async_copy(buf_ref.at[slot, pl.ds(r, 1)], xs_ref.at[pl.ds(d, 1)],
                                  sem.at[slot]).start()
        return c

    lax.fori_loop(0, tm, start, 0, unroll=DMA_UNROLL)

    @pl.when(i > 0)
    def _():
        wait_rows(1 - slot)

    @pl.when(i == n_tiles - 1)
    def _():
        wait_rows(slot)


def _dispatch(dest_flat, x2, gain, n_rows):
    m, d = x2.shape
    tm = min(DISPATCH_ROWS, m)
    return pl.pallas_call(
        functools.partial(_dispatch_kernel, n_tiles=m // tm),
        grid_spec=pltpu.PrefetchScalarGridSpec(
            num_scalar_prefetch=1,
            grid=(m // tm,),
            in_specs=[
                pl.BlockSpec((tm, d), lambda i, dest: (i, 0)),
                pl.BlockSpec((1, d), lambda i, dest: (0, 0)),
            ],
            out_specs=pl.BlockSpec(memory_space=pl.ANY),
            scratch_shapes=[pltpu.VMEM((2, tm, d // 2), U32), pltpu.SemaphoreType.DMA((2,))],
        ),
        out_shape=jax.ShapeDtypeStruct((n_rows, d // 2), U32),
        compiler_params=_cparams(("arbitrary",)),
        name="moe_dispatch",
    )(dest_flat, x2, gain)


def _moe_kernel(be_ref, bidx_ref, nrows_ref, nv_ref, xs_ref, wg_ref, wu_ref, wd_ref, y_ref, acc_ref, *,
                n_split):
    i = pl.program_id(0)
    j = pl.program_id(1)

    @pl.when(i < nv_ref[0])
    def _():
        rows, half = xs_ref.shape
        live = lax.broadcasted_iota(I32, (rows, 1), 0) < nrows_ref[i]
        x1, x2 = _unpack_halves(xs_ref[...])
        x1 = jnp.where(live, x1, 0.0).astype(BF16)
        x2 = jnp.where(live, x2, 0.0).astype(BF16)

        def proj(w_ref):
            return (jnp.dot(x1, w_ref[0, :half, :].astype(BF16), preferred_element_type=F32)
                    + jnp.dot(x2, w_ref[0, half:, :].astype(BF16), preferred_element_type=F32))

        act = (jax.nn.silu(proj(wg_ref)) * proj(wu_ref)).astype(BF16)
        part = jnp.dot(act, wd_ref[0].astype(BF16), preferred_element_type=F32)

        @pl.when(j == 0)
        def _():
            acc_ref[...] = part

        @pl.when((j > 0) & (j < n_split - 1))
        def _():
            acc_ref[...] = acc_ref[...] + part

        @pl.when(j == n_split - 1)
        def _():
            y_ref[...] = _pack_halves(acc_ref[...] + part)


def _moe_experts(block_e, block_idx, block_rows, n_valid, xs, w_gate, w_up, w_down, n_blocks):
    half = xs.shape[1]
    d = 2 * half
    d_e = w_gate.shape[2]
    n_split = 2
    dh = d_e // n_split
    br = MOE_ROWS

    def col(i, j, nv):
        return jnp.where(i < nv[0], j, n_split - 1)

    return pl.pallas_call(
        functools.partial(_moe_kernel, n_split=n_split),
        grid_spec=pltpu.PrefetchScalarGridSpec(
            num_scalar_prefetch=4,
            grid=(n_blocks, n_split),
            in_specs=[
                pl.BlockSpec((br, half), lambda i, j, be, bi, nr, nv: (bi[i], 0)),
                pl.BlockSpec((1, d, dh), lambda i, j, be, bi, nr, nv: (be[i], 0, col(i, j, nv))),
                pl.BlockSpec((1, d, dh), lambda i, j, be, bi, nr, nv: (be[i], 0, col(i, j, nv))),
                pl.BlockSpec((1, dh, d), lambda i, j, be, bi, nr, nv: (be[i], col(i, j, nv), 0)),
            ],
            out_specs=pl.BlockSpec((br, half), lambda i, j, be, bi, nr, nv: (bi[i], 0)),
            scratch_shapes=[pltpu.VMEM((br, d), F32)],
        ),
        out_shape=jax.ShapeDtypeStruct(xs.shape, U32),
        compiler_params=_cparams(("arbitrary", "arbitrary")),
        name="moe_experts",
    )(block_e, block_idx, block_rows, n_valid, xs, w_gate, w_up, w_down)


def _combine_kernel(dest_ref, x_ref, route_ref, y_ref, op_ref, os_ref, buf_ref, sem, *,
                    n_prompt_tiles, n_tiles):
    i = pl.program_id(0)
    tm = x_ref.shape[0]
    n_tokens = n_tiles * tm
    slot = i % 2

    def gather(tile, to_slot):
        def start(r, c):
            for k in range(TOP_K):
                d = dest_ref[k * n_tokens + tile * tm + r]
                pltpu.make_async_copy(y_ref.at[pl.ds(d, 1)], buf_ref.at[to_slot, k, pl.ds(r, 1)],
                                      sem.at[to_slot]).start()
            return c

        lax.fori_loop(0, tm, start, 0, unroll=DMA_UNROLL)

    @pl.when(i == 0)
    def _():
        gather(0, 0)

    @pl.when(i + 1 < n_tiles)
    def _():
        gather(i + 1, 1 - slot)

    for k in range(TOP_K):
        pltpu.make_async_copy(y_ref.at[pl.ds(0, tm)], buf_ref.at[slot, k], sem.at[slot]).wait()

    half = x_ref.shape[1] // 2
    first, second = x_ref[:, :half], x_ref[:, half:]
    for k in range(TOP_K):
        y1, y2 = _unpack_halves(buf_ref[slot, k])
        w = route_ref[:, 2 + k:3 + k]
        first = first + w * y1
        second = second + w * y2
    out = jnp.concatenate([first, second], axis=1)

    @pl.when(i < n_prompt_tiles)
    def _():
        op_ref[...] = out

    @pl.when(i >= n_prompt_tiles)
    def _():
        os_ref[...] = out


def _combine(dest_flat, x2, route, y, n_prompt_rows):
    m, d = x2.shape
    tm = COMBINE_ROWS
    npt = n_prompt_rows // tm
    return pl.pallas_call(
        functools.partial(_combine_kernel, n_prompt_tiles=npt, n_tiles=m // tm),
        grid_spec=pltpu.PrefetchScalarGridSpec(
            num_scalar_prefetch=1,
            grid=(m // tm,),
            in_specs=[
                pl.BlockSpec((tm, d), lambda i, dest: (i, 0)),
                pl.BlockSpec((tm, LANES), lambda i, dest: (i, 0)),
                pl.BlockSpec(memory_space=pl.ANY),
            ],
            out_specs=[
                pl.BlockSpec((tm, d), lambda i, dest: (jnp.minimum(i, npt - 1), 0)),
                pl.BlockSpec((tm, d), lambda i, dest: (jnp.maximum(i - npt, 0), 0)),
            ],
            scratch_shapes=[pltpu.VMEM((2, TOP_K, tm, d // 2), U32), pltpu.SemaphoreType.DMA((2,))],
        ),
        out_shape=[
            jax.ShapeDtypeStruct((n_prompt_rows, d), F32),
            jax.ShapeDtypeStruct((m - n_prompt_rows, d), F32),
        ],
        compiler_params=_cparams(("arbitrary",)),
        name="moe_combine",
    )(dest_flat, x2, route, y)


def _rope_tables(positions, head_dim):
    rot_dim = head_dim // 4
    half = rot_dim // 2
    inv_freq = jnp.power(ROPE_THETA, -jnp.arange(half, dtype=F32) * 2.0 / rot_dim)
    ang = positions.astype(F32)[:, None] * inv_freq[None, :]
    cos, sin = lax.optimization_barrier((jnp.cos(ang), jnp.sin(ang)))
    m = positions.shape[0]
    zeros = lambda n: jnp.zeros((m, n), F32)
    cos_h = jnp.concatenate([cos, cos, jnp.ones((m, head_dim - rot_dim), F32)], axis=1)
    sa_h = jnp.concatenate([-sin, zeros(head_dim - half)], axis=1)
    sb_h = jnp.concatenate([zeros(half), sin, zeros(head_dim - rot_dim)], axis=1)
    reps = LANES // head_dim
    return tuple(jnp.tile(a, (1, reps)) for a in (cos_h, sa_h, sb_h)), half


def _layer(xp, xs, cache_k, cache_v, norm1_g, w_in, gmlp_norm_g, w_s, b_s, q_norm_g, k_norm_g,
           sinks, p_a, p_b, w_out, norm2_g, w_rg, b_rg, w_re, b_re, w_gate_e, w_up_e, w_down_e):
    s, d = xp.shape
    dec_rows = xs.shape[0]
    dec_batch = cache_k.shape[0]
    t = dec_rows // dec_batch
    m = s + dec_rows
    d_a = gmlp_norm_g.shape[0]
    head_dim = q_norm_g.shape[0]
    n_heads = sinks.shape[0]
    q_dim = n_heads * head_dim
    kv_dim = cache_k.shape[2] * cache_k.shape[3]
    n_groups, per_group = w_re.shape[1], w_re.shape[2]
    n_experts = n_groups * per_group

    h = _norm1(xp, xs, norm1_g[None, :])
    ua = _inproj_act(h, w_in, 0, 2 * d_a, jax.nn.gelu, BF16, "inproj_gelu")
    positions = jnp.concatenate(
        [jnp.arange(s, dtype=I32), jnp.tile(PAST_LEN + jnp.arange(t, dtype=I32), dec_batch)])
    (cos_t, sa_t, sb_t), rot_half = _rope_tables(positions, head_dim)
    gain_row = jnp.concatenate([jnp.tile(q_norm_g, n_heads), jnp.tile(k_norm_g, kv_dim // head_dim),
                                jnp.ones((kv_dim,), F32)])[None, :]
    qkv = _inproj_qkv(h, w_in, 2 * d_a, q_dim, kv_dim, gain_row, cos_t, sa_t, sb_t, head_dim, rot_half)
    gates = _inproj_act(h, w_in, 2 * d_a + q_dim + 2 * kv_dim, 2 * d, jax.nn.sigmoid, BF16,
                        "inproj_gate")

    rows = w_s.shape[1]
    reps = rows // t
    w_sample = jnp.tile(w_s[:, :t, :t], (1, reps, reps))
    b_sample = jnp.tile(b_s[:, :t], (1, reps))
    o_a, vn_s = _gmlp(ua, w_s, w_sample, b_s.T, b_sample.T, gmlp_norm_g[None, :], s, t)
    o_b = _attention(qkv, cache_k.reshape(-1, kv_dim), cache_v.reshape(-1, kv_dim), sinks[None, :],
                     s, q_dim, kv_dim, head_dim, n_heads)

    merged = _merge(o_a, o_b, p_a, p_b, gates)
    x2 = _outproj(merged, w_out, xp, xs)

    pad = LANES - n_groups - n_experts
    wr = jnp.concatenate([w_rg, w_re.reshape(d, n_experts), jnp.zeros((d, pad), F32)], axis=1)
    br = jnp.concatenate([b_rg, b_re.reshape(n_experts), jnp.zeros((pad,), F32)])[None, :]
    route, counts = _router(x2, norm2_g[None, :], wr, br, n_groups, per_group)
    e_idx = route[:, 0:TOP_K].T.astype(I32)
    rank = route[:, 4:4 + TOP_K].T.astype(I32)
    counts = counts[0, :n_experts].astype(I32)

    blk = MOE_ROWS
    n_blocks = -(-(m * TOP_K) // blk) + n_experts
    padded = ((counts + blk - 1) // blk) * blk
    pend = jnp.cumsum(padded)
    pstart = pend - padded
    dest = (pstart[e_idx] + rank).reshape(-1).astype(I32)
    n_valid = (pend[-1] // blk).astype(I32)
    ids = jnp.arange(n_blocks, dtype=I32)
    block_idx = jnp.minimum(ids, n_valid - 1)
    first_row = block_idx * blk
    block_e = jnp.minimum(jnp.sum((pend[None, :] <= first_row[:, None]).astype(I32), axis=1), n_experts - 1)
    block_rows = jnp.clip(counts[block_e] - (first_row - pstart[block_e]), 0, blk).astype(I32)

    xs_sorted = _dispatch(dest, x2, norm2_g[None, :], n_blocks * blk)
    y_sorted = _moe_experts(block_e, block_idx, block_rows, n_valid[None], xs_sorted,
                            w_gate_e, w_up_e, w_down_e, n_blocks)
    yp, ys = _combine(dest, x2, route, y_sorted, s)
    return yp, ys, qkv, vn_s


def kernel(x_prompt, x_sample, cache_k_win, cache_v_win, norm1_g, w_in, gmlp_norm_g, w_s, b_s,
           q_norm_g, k_norm_g, sinks, p_a, p_b, w_out, norm2_g, w_rg, b_rg, w_re, b_re,
           w_gate_e, w_up_e, w_down_e):
    depth = norm1_g.shape[0]
    assert depth == 1, "weights of one layer are expected"
    batch, s, d = x_prompt.shape
    assert batch == 1
    dec_batch, t, _ = x_sample.shape
    head_dim = q_norm_g.shape[-1]
    n_kv = cache_k_win.shape[3]
    q_dim = sinks.shape[-1] * head_dim
    kv_dim = n_kv * head_dim
    keep = min(cache_k_win.shape[2], s)

    l = 0
    yp, ys, qkv, vn_s = _layer(
        x_prompt.reshape(s, d), x_sample.reshape(dec_batch * t, d), cache_k_win[l], cache_v_win[l],
        norm1_g[l], w_in[l], gmlp_norm_g[l], w_s[l], b_s[l], q_norm_g[l], k_norm_g[l], sinks[l],
        p_a[l], p_b[l], w_out[l], norm2_g[l], w_rg[l], b_rg[l], w_re[l], b_re[l],
        w_gate_e[l], w_up_e[l], w_down_e[l])

    k_all = qkv[:, q_dim:q_dim + kv_dim]
    v_all = qkv[:, q_dim + kv_dim:]
    k_win_p = k_all[s - keep:s].reshape(1, batch, keep, n_kv, head_dim)
    v_win_p = v_all[s - keep:s].reshape(1, batch, keep, n_kv, head_dim)
    k_new_s = k_all[s:].reshape(1, dec_batch, t, n_kv, head_dim)
    v_new_s = v_all[s:].reshape(1, dec_batch, t, n_kv, head_dim)
    gv_s = vn_s.reshape(1, dec_batch, t, -1)
    return (yp.reshape(batch, s, d), ys.reshape(dec_batch, t, d), k_win_p, v_win_p, k_new_s, v_new_s, gv_s)
```

```python
import functools

import jax
import jax.numpy as jnp
from jax import lax
from jax.experimental import pallas as pl
from jax.experimental.pallas import tpu as pltpu

F32 = jnp.float32
BF16 = jnp.bfloat16
I32 = jnp.int32
U32 = jnp.uint32

EPS = 1e-6
PAST_LEN = 1024
CHUNK = 64
ROPE_THETA = 500000.0
TOP_K = 2
LANES = 128

VMEM_LIMIT_BYTES = 56 * 1024 * 1024

ROW_TILE = 1024
WIDE_ROW_TILE = 1536
COL_TILE = 512
NORM_ROWS = 256
ROUTE_ROWS = 256
MOE_ROWS = 320
COMBINE_ROWS = 128
DISPATCH_ROWS = 256
DMA_UNROLL = 8


def _cparams(sem):
    return pltpu.CompilerParams(dimension_semantics=sem, vmem_limit_bytes=VMEM_LIMIT_BYTES)


def _cast_rows(src_ref, dst_ref, rows):
    n = src_ref.shape[0] // rows

    def body(r, c):
        sl = pl.ds(pl.multiple_of(r * rows, rows), rows)
        dst_ref[sl, :] = src_ref[sl, :].astype(dst_ref.dtype)
        return c

    lax.fori_loop(0, n, body, 0)


def _rms(x, gain):
    ms = jnp.mean(x * x, axis=-1, keepdims=True)
    return x * lax.rsqrt(ms + EPS) * gain


def _norm1_kernel(xp_ref, xs_ref, g_ref, h_ref, *, n_prompt_blocks):
    i = pl.program_id(0)

    @pl.when(i < n_prompt_blocks)
    def _():
        h_ref[...] = _rms(xp_ref[...], g_ref[...]).astype(h_ref.dtype)

    @pl.when(i >= n_prompt_blocks)
    def _():
        h_ref[...] = _rms(xs_ref[...], g_ref[...]).astype(h_ref.dtype)


def _norm1(xp, xs, gain):
    s, d = xp.shape
    t = xs.shape[0]
    br = min(NORM_ROWS, t)
    nbp, nbs = s // br, t // br
    return pl.pallas_call(
        functools.partial(_norm1_kernel, n_prompt_blocks=nbp),
        grid=(nbp + nbs,),
        in_specs=[
            pl.BlockSpec((br, d), lambda i: (jnp.minimum(i, nbp - 1), 0)),
            pl.BlockSpec((br, d), lambda i: (jnp.maximum(i - nbp, 0), 0)),
            pl.BlockSpec((1, d), lambda i: (0, 0)),
        ],
        out_specs=pl.BlockSpec((br, d), lambda i: (i, 0)),
        out_shape=jax.ShapeDtypeStruct((s + t, d), BF16),
        compiler_params=_cparams(("arbitrary",)),
        name="norm1",
    )(xp, xs, gain)


def _inproj_act_kernel(h_ref, w_ref, o_ref, wb_ref, *, act):
    @pl.when(pl.program_id(1) == 0)
    def _():
        _cast_rows(w_ref, wb_ref, 256)

    z = jnp.dot(h_ref[...], wb_ref[...], preferred_element_type=F32)
    o_ref[...] = act(z).astype(o_ref.dtype)


def _inproj_act(h, w, col0, ncols, act, out_dtype, name):
    m, d = h.shape
    bm = WIDE_ROW_TILE if m % WIDE_ROW_TILE == 0 else min(ROW_TILE, m)
    bn = min(COL_TILE, ncols)
    assert col0 % bn == 0 and ncols % bn == 0 and m % bm == 0
    off = col0 // bn
    return pl.pallas_call(
        functools.partial(_inproj_act_kernel, act=act),
        grid=(ncols // bn, m // bm),
        in_specs=[
            pl.BlockSpec((bm, d), lambda n, i: (i, 0)),
            pl.BlockSpec((d, bn), lambda n, i: (0, n + off)),
        ],
        out_specs=pl.BlockSpec((bm, bn), lambda n, i: (i, n)),
        out_shape=jax.ShapeDtypeStruct((m, ncols), out_dtype),
        scratch_shapes=[pltpu.VMEM((d, bn), BF16)],
        compiler_params=_cparams(("arbitrary", "arbitrary")),
        name=name,
    )(h, w)


def _inproj_qkv_kernel(h_ref, w_ref, gain_ref, cos_ref, sa_ref, sb_ref, o_ref, wb_ref, *,
                       n_norm_tiles, head_dim, rot_half):
    n = pl.program_id(0)

    @pl.when(pl.program_id(1) == 0)
    def _():
        _cast_rows(w_ref, wb_ref, 256)

    z = jnp.dot(h_ref[...], wb_ref[...], preferred_element_type=F32)
    bn = z.shape[1]

    @pl.when(n < n_norm_tiles)
    def _():
        r = lax.broadcasted_iota(I32, (bn, bn), 0) // head_dim
        c = lax.broadcasted_iota(I32, (bn, bn), 1) // head_dim
        seg = (r == c).astype(BF16)
        ssq = jnp.dot((z * z).astype(BF16), seg, preferred_element_type=F32)
        y = z * lax.rsqrt(ssq * (1.0 / head_dim) + EPS) * gain_ref[...]
        reps = bn // cos_ref.shape[1]
        cosv = jnp.tile(cos_ref[...], (1, reps))
        sa = jnp.tile(sa_ref[...], (1, reps))
        sb = jnp.tile(sb_ref[...], (1, reps))
        y = y * cosv + pltpu.roll(y, bn - rot_half, 1) * sa + pltpu.roll(y, rot_half, 1) * sb
        o_ref[...] = y

    @pl.when(n >= n_norm_tiles)
    def _():
        o_ref[...] = z


def _inproj_qkv(h, w, col0, q_dim, kv_dim, gain_row, cos_t, sa_t, sb_t, head_dim, rot_half):
    m, d = h.shape
    ncols = q_dim + 2 * kv_dim
    bm, bn = min(ROW_TILE, m), kv_dim
    assert col0 % bn == 0 and q_dim % bn == 0 and bn % LANES == 0 and m % bm == 0
    off = col0 // bn
    n_norm_tiles = (q_dim + kv_dim) // bn
    tw = cos_t.shape[1]
    return pl.pallas_call(
        functools.partial(_inproj_qkv_kernel, n_norm_tiles=n_norm_tiles, head_dim=head_dim,
                          rot_half=rot_half),
        grid=(ncols // bn, m // bm),
        in_specs=[
            pl.BlockSpec((bm, d), lambda n, i: (i, 0)),
            pl.BlockSpec((d, bn), lambda n, i: (0, n + off)),
            pl.BlockSpec((1, bn), lambda n, i: (0, n)),
            pl.BlockSpec((bm, tw), lambda n, i: (i, 0)),
            pl.BlockSpec((bm, tw), lambda n, i: (i, 0)),
            pl.BlockSpec((bm, tw), lambda n, i: (i, 0)),
        ],
        out_specs=pl.BlockSpec((bm, bn), lambda n, i: (i, n)),
        out_shape=jax.ShapeDtypeStruct((m, ncols), F32),
        scratch_shapes=[pltpu.VMEM((d, bn), BF16)],
        compiler_params=_cparams(("arbitrary", "arbitrary")),
        name="inproj_qkv",
    )(h, w, gain_row, cos_t, sa_t, sb_t)


def _gmlp_kernel(u_ref, va_ref, wp_ref, ws_ref, bp_ref, bs_ref, gain_ref, o_ref, vn_ref, *,
                 n_prompt_blocks, groups, sample_len):
    i = pl.program_id(0)
    rows, d_a = u_ref.shape
    gw = d_a // groups
    ri = lax.broadcasted_iota(I32, (rows, rows), 0)
    ci = lax.broadcasted_iota(I32, (rows, rows), 1)

    def run(w_ref, b_ref, sub, emit_vn):
        vn = _rms(va_ref[...].astype(F32), gain_ref[...])
        if emit_vn:
            vn_ref[...] = vn
        vb = vn.astype(BF16)
        mask = (ci <= ri) & ((ri // sub) == (ci // sub))
        for g in range(groups):
            sl = slice(g * gw, (g + 1) * gw)
            wg = jnp.where(mask, w_ref[g], 0.0).astype(BF16)
            s = jnp.dot(wg, vb[:, sl], preferred_element_type=F32) + b_ref[:, g:g + 1]
            o_ref[:, sl] = (u_ref[:, sl].astype(F32) * s).astype(o_ref.dtype)

    @pl.when(i < n_prompt_blocks)
    def _():
        run(wp_ref, bp_ref, rows, False)

    @pl.when(i >= n_prompt_blocks)
    def _():
        run(ws_ref, bs_ref, sample_len, True)


def _gmlp(ua, w_prompt, w_sample, b_prompt, b_sample, gain, n_prompt_rows, sample_len):
    m, two_da = ua.shape
    d_a = two_da // 2
    groups, rows, _ = w_prompt.shape
    nbp = n_prompt_rows // rows
    nb = m // rows
    full = lambda i: (0, 0, 0)
    return pl.pallas_call(
        functools.partial(_gmlp_kernel, n_prompt_blocks=nbp, groups=groups, sample_len=sample_len),
        grid=(nb,),
        in_specs=[
            pl.BlockSpec((rows, d_a), lambda i: (i, 0)),
            pl.BlockSpec((rows, d_a), lambda i: (i, 1)),
            pl.BlockSpec((groups, rows, rows), full),
            pl.BlockSpec((groups, rows, rows), full),
            pl.BlockSpec((rows, groups), lambda i: (0, 0)),
            pl.BlockSpec((rows, groups), lambda i: (0, 0)),
            pl.BlockSpec((1, d_a), lambda i: (0, 0)),
        ],
        out_specs=[
            pl.BlockSpec((rows, d_a), lambda i: (i, 0)),
            pl.BlockSpec((rows, d_a), lambda i: (jnp.maximum(i - nbp, 0), 0)),
        ],
        out_shape=[
            jax.ShapeDtypeStruct((m, d_a), BF16),
            jax.ShapeDtypeStruct((m - n_prompt_rows, d_a), F32),
        ],
        compiler_params=_cparams(("arbitrary",)),
        name="gmlp_spatial",
    )(ua, ua, w_prompt, w_sample, b_prompt, b_sample, gain)


def _attn_kernel(bias_ref, q_ref, k0_ref, k1_ref, k2_ref, v0_ref, v1_ref, v2_ref, ck_ref, cv_ref,
                 o_ref, s_ref, e_ref, *, n_prompt_chunks, tiles_per_kv, head_dim):
    c = pl.program_id(0)
    t = q_ref.shape[0]
    kv_dim = k2_ref.shape[1]
    lkp = bias_ref.shape[1] // 2
    scale = head_dim ** -0.5
    nt = (((1,), (1,)), ((), ()))

    def run(k_parts, v_parts, first_key_chunk):
        k = jnp.concatenate(k_parts, axis=0)
        v = jnp.concatenate(v_parts, axis=0)
        lk = k.shape[0]
        lower = lax.broadcasted_iota(I32, (lk, LANES), 1) < head_dim
        zpad = jnp.zeros((lkp - lk, LANES), F32)
        rr = lax.broadcasted_iota(I32, (2 * lkp, LANES), 0) < lkp
        rl = lax.broadcasted_iota(I32, (2 * lkp, LANES), 1) < head_dim
        ones_sel = (rr == rl).astype(BF16)
        chunk_bias = None
        if first_key_chunk is not None:
            col = lax.broadcasted_iota(I32, (1, 2 * lkp), 1)
            key_chunk = first_key_chunk + (col % lkp) // t
            chunk_bias = jnp.where(key_chunk >= 0, 0.0, -jnp.inf)

        def doubled(tile, rolled, x):
            lo_src, hi_src = (tile, rolled) if x == 0 else (rolled, tile)
            return jnp.concatenate([jnp.where(lower, lo_src, 0.0), zpad,
                                    jnp.where(lower, 0.0, hi_src), zpad], axis=0).astype(BF16)

        n_kv = kv_dim // head_dim
        tiles_of = lambda j: [j * tiles_per_kv + a for a in range(tiles_per_kv)]

        for b in range(kv_dim // LANES):
            kt = k[:, b * LANES:(b + 1) * LANES]
            kr = pltpu.roll(kt, head_dim, 1)
            for x in range(2):
                j = 2 * b + x
                q2 = jnp.concatenate([q_ref[:, a * LANES:(a + 1) * LANES] for a in tiles_of(j)], axis=0)
                q2 = (q2 * scale).astype(BF16)
                rows = []
                for a in tiles_of(j):
                    row = bias_ref[a:a + 1, :]
                    if chunk_bias is not None:
                        row = row + chunk_bias
                    rows.append(jnp.broadcast_to(row, (t, 2 * lkp)))
                s_ref[j] = (lax.dot_general(q2, doubled(kt, kr, x), nt, preferred_element_type=F32)
                            + jnp.concatenate(rows, axis=0))

        for j in range(n_kv):
            for hs in (slice(0, lkp), slice(lkp, 2 * lkp)):
                sh = s_ref[j, :, hs]
                e_ref[j, :, hs] = jnp.exp(sh - jnp.max(sh, axis=-1, keepdims=True)).astype(BF16)

        for b in range(kv_dim // LANES):
            vt = v[:, b * LANES:(b + 1) * LANES]
            vr = pltpu.roll(vt, head_dim, 1)
            for x in range(2):
                j = 2 * b + x
                rhs = jnp.concatenate([doubled(vt, vr, x), ones_sel], axis=1)
                r = jnp.dot(e_ref[j], rhs, preferred_element_type=F32)
                o = r[:, :LANES] / r[:, LANES:]
                for i, a in enumerate(tiles_of(j)):
                    o_ref[:, a * LANES:(a + 1) * LANES] = o[i * t:(i + 1) * t].astype(o_ref.dtype)

    @pl.when(c < n_prompt_chunks)
    def _():
        run([k0_ref[...], k1_ref[...], k2_ref[...]], [v0_ref[...], v1_ref[...], v2_ref[...]], c - 2)

    @pl.when(c >= n_prompt_chunks)
    def _():
        run([ck_ref[...], k2_ref[...]], [cv_ref[...], v2_ref[...]], None)


def _attention(qkv, cache_k, cache_v, sinks, n_prompt_rows, q_dim, kv_dim, head_dim, n_heads):
    m = qkv.shape[0]
    t = CHUNK
    nc = n_prompt_rows // t
    n_kv_heads = kv_dim // head_dim
    window = cache_k.shape[0] // ((m - n_prompt_rows) // t)
    kcol, vcol = q_dim // kv_dim, q_dim // kv_dim + 1
    gqa = n_heads // n_kv_heads
    assert 2 * head_dim == LANES and gqa % 2 == 0 and kv_dim % LANES == 0

    lk = window + t
    lkp = -(-(lk + 1) // LANES) * LANES
    pair = sinks.reshape(n_heads // 2, 2, 1)
    bias = jnp.concatenate([jnp.zeros((n_heads // 2, 2, lk), F32), pair,
                            jnp.full((n_heads // 2, 2, lkp - lk - 1), -jnp.inf, F32)], axis=2)
    bias = bias.reshape(n_heads // 2, 2 * lkp)

    def kv_spec(back, col):
        return pl.BlockSpec((t, kv_dim), lambda c: (jnp.maximum(c - back, 0), col))

    cache_spec = pl.BlockSpec((window, kv_dim), lambda c: (jnp.maximum(c - nc, 0), 0))
    return pl.pallas_call(
        functools.partial(_attn_kernel, n_prompt_chunks=nc, tiles_per_kv=gqa // 2, head_dim=head_dim),
        grid=(m // t,),
        in_specs=[
            pl.BlockSpec((n_heads // 2, 2 * lkp), lambda c: (0, 0)),
            pl.BlockSpec((t, q_dim), lambda c: (c, 0)),
            kv_spec(2, kcol), kv_spec(1, kcol), kv_spec(0, kcol),
            kv_spec(2, vcol), kv_spec(1, vcol), kv_spec(0, vcol),
            cache_spec, cache_spec,
        ],
        out_specs=pl.BlockSpec((t, q_dim), lambda c: (c, 0)),
        out_shape=jax.ShapeDtypeStruct((m, q_dim), BF16),
        scratch_shapes=[pltpu.VMEM((n_kv_heads, gqa // 2 * t, 2 * lkp), F32),
                        pltpu.VMEM((n_kv_heads, gqa // 2 * t, 2 * lkp), BF16)],
        compiler_params=_cparams(("arbitrary",)),
        name="window_attention",
    )(bias, qkv, qkv, qkv, qkv, qkv, qkv, qkv, cache_k, cache_v)


def _merge_kernel(oa_ref, ob_ref, pa_ref, pb_ref, ga_ref, gb_ref, o_ref, pab_ref, pbb_ref):
    @pl.when(pl.program_id(1) == 0)
    def _():
        _cast_rows(pa_ref, pab_ref, 256)
        _cast_rows(pb_ref, pbb_ref, 256)

    a = jnp.dot(oa_ref[...], pab_ref[...], preferred_element_type=F32)
    b = jnp.dot(ob_ref[...], pbb_ref[...], preferred_element_type=F32)
    o_ref[...] = (ga_ref[...].astype(F32) * a + gb_ref[...].astype(F32) * b).astype(o_ref.dtype)


def _merge(o_a, o_b, p_a, p_b, gates):
    m, d_a = o_a.shape
    q_dim = o_b.shape[1]
    d = p_a.shape[1]
    bm, bn = min(ROW_TILE, m), min(COL_TILE, d)
    goff = d // bn
    return pl.pallas_call(
        _merge_kernel,
        grid=(d // bn, m // bm),
        in_specs=[
            pl.BlockSpec((bm, d_a), lambda n, i: (i, 0)),
            pl.BlockSpec((bm, q_dim), lambda n, i: (i, 0)),
            pl.BlockSpec((d_a, bn), lambda n, i: (0, n)),
            pl.BlockSpec((q_dim, bn), lambda n, i: (0, n)),
            pl.BlockSpec((bm, bn), lambda n, i: (i, n)),
            pl.BlockSpec((bm, bn), lambda n, i: (i, n + goff)),
        ],
        out_specs=pl.BlockSpec((bm, bn), lambda n, i: (i, n)),
        out_shape=jax.ShapeDtypeStruct((m, d), BF16),
        scratch_shapes=[pltpu.VMEM((d_a, bn), BF16), pltpu.VMEM((q_dim, bn), BF16)],
        compiler_params=_cparams(("arbitrary", "arbitrary")),
        name="merge_proj",
    )(o_a, o_b, p_a, p_b, gates, gates)


def _outproj_kernel(t_ref, w_ref, xp_ref, xs_ref, o_ref, wb_ref, *, n_prompt_tiles):
    i = pl.program_id(1)

    @pl.when(i == 0)
    def _():
        _cast_rows(w_ref, wb_ref, 256)

    z = jnp.dot(t_ref[...], wb_ref[...], preferred_element_type=F32)

    @pl.when(i < n_prompt_tiles)
    def _():
        o_ref[...] = xp_ref[...] + z

    @pl.when(i >= n_prompt_tiles)
    def _():
        o_ref[...] = xs_ref[...] + z


def _outproj(tm, w_out, xp, xs):
    m, d = tm.shape
    s, t = xp.shape[0], xs.shape[0]
    bm, bn = min(ROW_TILE, t), min(COL_TILE, d)
    npt = s // bm
    return pl.pallas_call(
        functools.partial(_outproj_kernel, n_prompt_tiles=npt),
        grid=(d // bn, m // bm),
        in_specs=[
            pl.BlockSpec((bm, d), lambda n, i: (i, 0)),
            pl.BlockSpec((d, bn), lambda n, i: (0, n)),
            pl.BlockSpec((bm, bn), lambda n, i: (jnp.minimum(i, npt - 1), n)),
            pl.BlockSpec((bm, bn), lambda n, i: (jnp.maximum(i - npt, 0), n)),
        ],
        out_specs=pl.BlockSpec((bm, bn), lambda n, i: (i, n)),
        out_shape=jax.ShapeDtypeStruct((m, d), F32),
        scratch_shapes=[pltpu.VMEM((d, bn), BF16)],
        compiler_params=_cparams(("arbitrary", "arbitrary")),
        name="out_proj",
    )(tm, w_out, xp, xs)


def _router_kernel(x_ref, g_ref, wr_ref, br_ref, route_ref, cnt_ref, wcat_ref, carry_ref, *,
                   n_groups, per_group):
    i = pl.program_id(0)

    @pl.when(i == 0)
    def _():
        w = wr_ref[...]
        hi = w.astype(BF16)
        wcat_ref[:, :LANES] = hi
        wcat_ref[:, LANES:] = (w - hi.astype(F32)).astype(BF16)
        carry_ref[...] = jnp.zeros_like(carry_ref)

    hn = _rms(x_ref[...], g_ref[...])
    rows = hn.shape[0]
    hi = hn.astype(BF16)
    lo = (hn - hi.astype(F32)).astype(BF16)
    prod = jnp.dot(jnp.concatenate([hi, lo], axis=0), wcat_ref[...], preferred_element_type=F32)
    logits = prod[:rows, :LANES] + prod[:rows, LANES:] + prod[rows:, :LANES] + br_ref[...]
    lane = lax.broadcasted_iota(I32, logits.shape, 1)
    big = jnp.int32(LANES)

    lg = jnp.where(lane < n_groups, logits, -jnp.inf)
    mg = jnp.max(lg, axis=-1, keepdims=True)
    pg_sel = 1.0 / jnp.sum(jnp.exp(lg - mg), axis=-1, keepdims=True)
    gsel = jnp.min(jnp.where(lg == mg, lane, big), axis=-1, keepdims=True)

    first = n_groups + gsel * per_group
    emask = (lane >= first) & (lane < first + per_group)
    le = jnp.where(emask, logits, -jnp.inf)
    me = jnp.max(le, axis=-1, keepdims=True)
    ee = jnp.exp(le - me)
    pe = jnp.where(emask, ee / jnp.sum(ee, axis=-1, keepdims=True), -1.0)
    p1 = jnp.max(pe, axis=-1, keepdims=True)
    i1 = jnp.min(jnp.where(pe == p1, lane, big), axis=-1, keepdims=True)
    pe2 = jnp.where(lane == i1, -1.0, pe)
    p2 = jnp.max(pe2, axis=-1, keepdims=True)
    i2 = jnp.min(jnp.where(pe2 == p2, lane, big), axis=-1, keepdims=True)
    psum = p1 + p2
    w1 = p1 / psum * pg_sel
    w2 = p2 / psum * pg_sel
    e1 = i1 - n_groups
    e2 = i2 - n_groups

    oh1 = (lane == e1).astype(F32)
    oh2 = (lane == e2).astype(F32)
    ohs = oh1 + oh2
    ri = lax.broadcasted_iota(I32, (rows, rows), 0)
    ci = lax.broadcasted_iota(I32, (rows, rows), 1)
    below = (ci < ri).astype(BF16)
    before = jnp.dot(below, ohs.astype(BF16), preferred_element_type=F32) + carry_ref[...]
    r1 = jnp.sum(before * oh1, axis=-1, keepdims=True)
    r2 = jnp.sum(before * oh2, axis=-1, keepdims=True)
    carry_ref[...] = carry_ref[...] + jnp.sum(ohs, axis=0, keepdims=True)
    cnt_ref[...] = carry_ref[...]

    route = jnp.where(lane == 0, e1.astype(F32), 0.0)
    route = jnp.where(lane == 1, e2.astype(F32), route)
    route = jnp.where(lane == 2, w1, route)
    route = jnp.where(lane == 3, w2, route)
    route = jnp.where(lane == 4, r1, route)
    route = jnp.where(lane == 5, r2, route)
    route_ref[...] = route


def _router(x2, gain, wr, br, n_groups, per_group):
    m, d = x2.shape
    br_rows = min(ROUTE_ROWS, m)
    return pl.pallas_call(
        functools.partial(_router_kernel, n_groups=n_groups, per_group=per_group),
        grid=(m // br_rows,),
        in_specs=[
            pl.BlockSpec((br_rows, d), lambda i: (i, 0)),
            pl.BlockSpec((1, d), lambda i: (0, 0)),
            pl.BlockSpec((d, LANES), lambda i: (0, 0)),
            pl.BlockSpec((1, LANES), lambda i: (0, 0)),
        ],
        out_specs=[
            pl.BlockSpec((br_rows, LANES), lambda i: (i, 0)),
            pl.BlockSpec((1, LANES), lambda i: (0, 0)),
        ],
        out_shape=[
            jax.ShapeDtypeStruct((m, LANES), F32),
            jax.ShapeDtypeStruct((1, LANES), F32),
        ],
        scratch_shapes=[pltpu.VMEM((d, 2 * LANES), BF16), pltpu.VMEM((1, LANES), F32)],
        compiler_params=_cparams(("arbitrary",)),
        name="norm2_router",
    )(x2, gain, wr, br)


def _pack_pair(first, second):
    hi = lax.bitcast_convert_type(first.astype(BF16).astype(F32), U32)
    lo = lax.bitcast_convert_type(second.astype(BF16).astype(F32), U32)
    return hi | (lo >> 16)


def _pack_halves(x):
    half = x.shape[1] // 2
    return _pack_pair(x[:, :half], x[:, half:])


def _unpack_halves(p):
    first = lax.bitcast_convert_type(p & jnp.uint32(0xFFFF0000), F32)
    second = lax.bitcast_convert_type(p << 16, F32)
    return first, second


def _dispatch_kernel(dest_ref, x_ref, g_ref, xs_ref, buf_ref, sem, *, n_tiles):
    i = pl.program_id(0)
    tm = x_ref.shape[0]
    n_tokens = n_tiles * tm
    slot = i % 2

    def wait_rows(s):
        for k in range(TOP_K):
            pltpu.make_async_copy(buf_ref.at[s], xs_ref.at[pl.ds(0, tm)], sem.at[s]).wait()

    buf_ref[slot] = _pack_halves(_rms(x_ref[...], g_ref[...]))

    def start(r, c):
        for k in range(TOP_K):
            d = dest_ref[k * n_tokens + i * tm + r]
            pltpu.make_async_copy(buf_ref.at[slot, pl.ds(r, 1)], xs_ref.at[pl.ds(d, 1)],
                                  sem.at[slot]).start()
        return c

    lax.fori_loop(0, tm, start, 0, unroll=DMA_UNROLL)

    @pl.when(i > 0)
    def _():
        wait_rows(1 - slot)

    @pl.when(i == n_tiles - 1)
    def _():
        wait_rows(slot)


def _dispatch(dest_flat, x2, gain, n_rows):
    m, d = x2.shape
    tm = min(DISPATCH_ROWS, m)
    return pl.pallas_call(
        functools.partial(_dispatch_kernel, n_tiles=m // tm),
        grid_spec=pltpu.PrefetchScalarGridSpec(
            num_scalar_prefetch=1,
            grid=(m // tm,),
            in_specs=[
                pl.BlockSpec((tm, d), lambda i, dest: (i, 0)),
                pl.BlockSpec((1, d), lambda i, dest: (0, 0)),
            ],
            out_specs=pl.BlockSpec(memory_space=pl.ANY),
            scratch_shapes=[pltpu.VMEM((2, tm, d // 2), U32), pltpu.SemaphoreType.DMA((2,))],
        ),
        out_shape=jax.ShapeDtypeStruct((n_rows, d // 2), U32),
        compiler_params=_cparams(("arbitrary",)),
        name="moe_dispatch",
    )(dest_flat, x2, gain)


def _moe_kernel(be_ref, bidx_ref, nrows_ref, nv_ref, xs_ref, wg_ref, wu_ref, wd_ref, y_ref,
                x1_ref, x2_ref, act_ref, *, n_split):
    i = pl.program_id(0)
    j = pl.program_id(1)

    @pl.when(i < nv_ref[0])
    def _():
        rows, half = xs_ref.shape
        dh = wg_ref.shape[2]

        @pl.when(j == 0)
        def _():
            live = lax.broadcasted_iota(I32, (rows, 1), 0) < nrows_ref[i]
            x1, x2 = _unpack_halves(xs_ref[...])
            x1_ref[...] = jnp.where(live, x1, 0.0).astype(BF16)
            x2_ref[...] = jnp.where(live, x2, 0.0).astype(BF16)

        def proj(w_ref):
            return (jnp.dot(x1_ref[...], w_ref[0, :half, :].astype(BF16), preferred_element_type=F32)
                    + jnp.dot(x2_ref[...], w_ref[0, half:, :].astype(BF16), preferred_element_type=F32))

        act = (jax.nn.silu(proj(wg_ref)) * proj(wu_ref)).astype(BF16)
        for q in range(n_split):
            @pl.when(j == q)
            def _(q=q):
                act_ref[:, q * dh:(q + 1) * dh] = act

        @pl.when(j == n_split - 1)
        def _():
            a = act_ref[...]
            first = jnp.dot(a, wd_ref[0, :, :half].astype(BF16), preferred_element_type=F32)
            second = jnp.dot(a, wd_ref[0, :, half:].astype(BF16), preferred_element_type=F32)
            y_ref[...] = _pack_pair(first, second)


def _moe_experts(block_e, block_idx, block_rows, n_valid, xs, w_gate, w_up, w_down, n_blocks):
    half = xs.shape[1]
    d = 2 * half
    d_e = w_gate.shape[2]
    n_split = 2
    dh = d_e // n_split
    br = MOE_ROWS

    def col(i, j, nv):
        return jnp.where(i < nv[0], j, n_split - 1)

    return pl.pallas_call(
        functools.partial(_moe_kernel, n_split=n_split),
        grid_spec=pltpu.PrefetchScalarGridSpec(
            num_scalar_prefetch=4,
            grid=(n_blocks, n_split),
            in_specs=[
                pl.BlockSpec((br, half), lambda i, j, be, bi, nr, nv: (bi[i], 0)),
                pl.BlockSpec((1, d, dh), lambda i, j, be, bi, nr, nv: (be[i], 0, col(i, j, nv))),
                pl.BlockSpec((1, d, dh), lambda i, j, be, bi, nr, nv: (be[i], 0, col(i, j, nv))),
                pl.BlockSpec((1, d_e, d), lambda i, j, be, bi, nr, nv: (be[i], 0, 0)),
            ],
            out_specs=pl.BlockSpec((br, half), lambda i, j, be, bi, nr, nv: (bi[i], 0)),
            scratch_shapes=[pltpu.VMEM((br, half), BF16), pltpu.VMEM((br, half), BF16),
                            pltpu.VMEM((br, d_e), BF16)],
        ),
        out_shape=jax.ShapeDtypeStruct(xs.shape, U32),
        compiler_params=_cparams(("arbitrary", "arbitrary")),
        name="moe_experts",
    )(block_e, block_idx, block_rows, n_valid, xs, w_gate, w_up, w_down)


def _combine_kernel(dest_ref, x_ref, route_ref, y_ref, op_ref, os_ref, buf_ref, sem, *,
                    n_prompt_tiles, n_tiles):
    i = pl.program_id(0)
    tm = x_ref.shape[0]
    n_tokens = n_tiles * tm
    slot = i % 2

    def gather(tile, to_slot):
        def start(r, c):
            for k in range(TOP_K):
                d = dest_ref[k * n_tokens + tile * tm + r]
                pltpu.make_async_copy(y_ref.at[pl.ds(d, 1)], buf_ref.at[to_slot, k, pl.ds(r, 1)],
                                      sem.at[to_slot]).start()
            return c

        lax.fori_loop(0, tm, start, 0, unroll=DMA_UNROLL)

    @pl.when(i == 0)
    def _():
        gather(0, 0)

    @pl.when(i + 1 < n_tiles)
    def _():
        gather(i + 1, 1 - slot)

    for k in range(TOP_K):
        pltpu.make_async_copy(y_ref.at[pl.ds(0, tm)], buf_ref.at[slot, k], sem.at[slot]).wait()

    half = x_ref.shape[1] // 2
    first, second = x_ref[:, :half], x_ref[:, half:]
    for k in range(TOP_K):
        y1, y2 = _unpack_halves(buf_ref[slot, k])
        w = route_ref[:, 2 + k:3 + k]
        first = first + w * y1
        second = second + w * y2
    out = jnp.concatenate([first, second], axis=1)

    @pl.when(i < n_prompt_tiles)
    def _():
        op_ref[...] = out

    @pl.when(i >= n_prompt_tiles)
    def _():
        os_ref[...] = out


def _combine(dest_flat, x2, route, y, n_prompt_rows):
    m, d = x2.shape
    tm = COMBINE_ROWS
    npt = n_prompt_rows // tm
    return pl.pallas_call(
        functools.partial(_combine_kernel, n_prompt_tiles=npt, n_tiles=m // tm),
        grid_spec=pltpu.PrefetchScalarGridSpec(
            num_scalar_prefetch=1,
            grid=(m // tm,),
            in_specs=[
                pl.BlockSpec((tm, d), lambda i, dest: (i, 0)),
                pl.BlockSpec((tm, LANES), lambda i, dest: (i, 0)),
                pl.BlockSpec(memory_space=pl.ANY),
            ],
            out_specs=[
                pl.BlockSpec((tm, d), lambda i, dest: (jnp.minimum(i, npt - 1), 0)),
                pl.BlockSpec((tm, d), lambda i, dest: (jnp.maximum(i - npt, 0), 0)),
            ],
            scratch_shapes=[pltpu.VMEM((2, TOP_K, tm, d // 2), U32), pltpu.SemaphoreType.DMA((2,))],
        ),
        out_shape=[
            jax.ShapeDtypeStruct((n_prompt_rows, d), F32),
            jax.ShapeDtypeStruct((m - n_prompt_rows, d), F32),
        ],
        compiler_params=_cparams(("arbitrary",)),
        name="moe_combine",
    )(dest_flat, x2, route, y)


def _rope_tables(positions, head_dim):
    rot_dim = head_dim // 4
    half = rot_dim // 2
    inv_freq = jnp.power(ROPE_THETA, -jnp.arange(half, dtype=F32) * 2.0 / rot_dim)
    ang = positions.astype(F32)[:, None] * inv_freq[None, :]
    cos, sin = lax.optimization_barrier((jnp.cos(ang), jnp.sin(ang)))
    m = positions.shape[0]
    zeros = lambda n: jnp.zeros((m, n), F32)
    cos_h = jnp.concatenate([cos, cos, jnp.ones((m, head_dim - rot_dim), F32)], axis=1)
    sa_h = jnp.concatenate([-sin, zeros(head_dim - half)], axis=1)
    sb_h = jnp.concatenate([zeros(half), sin, zeros(head_dim - rot_dim)], axis=1)
    reps = LANES // head_dim
    return tuple(jnp.tile(a, (1, reps)) for a in (cos_h, sa_h, sb_h)), half


def _layer(xp, xs, cache_k, cache_v, norm1_g, w_in, gmlp_norm_g, w_s, b_s, q_norm_g, k_norm_g,
           sinks, p_a, p_b, w_out, norm2_g, w_rg, b_rg, w_re, b_re, w_gate_e, w_up_e, w_down_e):
    s, d = xp.shape
    dec_rows = xs.shape[0]
    dec_batch = cache_k.shape[0]
    t = dec_rows // dec_batch
    m = s + dec_rows
    d_a = gmlp_norm_g.shape[0]
    head_dim = q_norm_g.shape[0]
    n_heads = sinks.shape[0]
    q_dim = n_heads * head_dim
    kv_dim = cache_k.shape[2] * cache_k.shape[3]
    n_groups, per_group = w_re.shape[1], w_re.shape[2]
    n_experts = n_groups * per_group

    h = _norm1(xp, xs, norm1_g[None, :])
    ua = _inproj_act(h, w_in, 0, 2 * d_a, jax.nn.gelu, BF16, "inproj_gelu")
    positions = jnp.concatenate(
        [jnp.arange(s, dtype=I32), jnp.tile(PAST_LEN + jnp.arange(t, dtype=I32), dec_batch)])
    (cos_t, sa_t, sb_t), rot_half = _rope_tables(positions, head_dim)
    gain_row = jnp.concatenate([jnp.tile(q_norm_g, n_heads), jnp.tile(k_norm_g, kv_dim // head_dim),
                                jnp.ones((kv_dim,), F32)])[None, :]
    qkv = _inproj_qkv(h, w_in, 2 * d_a, q_dim, kv_dim, gain_row, cos_t, sa_t, sb_t, head_dim, rot_half)
    gates = _inproj_act(h, w_in, 2 * d_a + q_dim + 2 * kv_dim, 2 * d, jax.nn.sigmoid, BF16,
                        "inproj_gate")

    rows = w_s.shape[1]
    reps = rows // t
    w_sample = jnp.tile(w_s[:, :t, :t], (1, reps, reps))
    b_sample = jnp.tile(b_s[:, :t], (1, reps))
    o_a, vn_s = _gmlp(ua, w_s, w_sample, b_s.T, b_sample.T, gmlp_norm_g[None, :], s, t)
    o_b = _attention(qkv, cache_k.reshape(-1, kv_dim), cache_v.reshape(-1, kv_dim), sinks[None, :],
                     s, q_dim, kv_dim, head_dim, n_heads)

    merged = _merge(o_a, o_b, p_a, p_b, gates)
    x2 = _outproj(merged, w_out, xp, xs)

    pad = LANES - n_groups - n_experts
    wr = jnp.concatenate([w_rg, w_re.reshape(d, n_experts), jnp.zeros((d, pad), F32)], axis=1)
    br = jnp.concatenate([b_rg, b_re.reshape(n_experts), jnp.zeros((pad,), F32)])[None, :]
    route, counts = _router(x2, norm2_g[None, :], wr, br, n_groups, per_group)
    e_idx = route[:, 0:TOP_K].T.astype(I32)
    rank = route[:, 4:4 + TOP_K].T.astype(I32)
    counts = counts[0, :n_experts].astype(I32)

    blk = MOE_ROWS
    n_blocks = -(-(m * TOP_K) // blk) + n_experts
    padded = ((counts + blk - 1) // blk) * blk
    pend = jnp.cumsum(padded)
    pstart = pend - padded
    hit = e_idx[:, :, None] == jnp.arange(n_experts, dtype=I32)
    dest = (jnp.sum(jnp.where(hit, pstart, 0), axis=-1) + rank).reshape(-1).astype(I32)
    n_valid = (pend[-1] // blk).astype(I32)
    ids = jnp.arange(n_blocks, dtype=I32)
    block_idx = jnp.minimum(ids, n_valid - 1)
    first_row = block_idx * blk
    block_e = jnp.minimum(jnp.sum((pend[None, :] <= first_row[:, None]).astype(I32), axis=1), n_experts - 1)
    block_rows = jnp.clip(counts[block_e] - (first_row - pstart[block_e]), 0, blk).astype(I32)

    xs_sorted = _dispatch(dest, x2, norm2_g[None, :], n_blocks * blk)
    y_sorted = _moe_experts(block_e, block_idx, block_rows, n_valid[None], xs_sorted,
                            w_gate_e, w_up_e, w_down_e, n_blocks)
    yp, ys = _combine(dest, x2, route, y_sorted, s)
    return yp, ys, qkv, vn_s


def kernel(x_prompt, x_sample, cache_k_win, cache_v_win, norm1_g, w_in, gmlp_norm_g, w_s, b_s,
           q_norm_g, k_norm_g, sinks, p_a, p_b, w_out, norm2_g, w_rg, b_rg, w_re, b_re,
           w_gate_e, w_up_e, w_down_e):
    depth = norm1_g.shape[0]
    assert depth == 1, "weights of one layer are expected"
    batch, s, d = x_prompt.shape
    assert batch == 1
    dec_batch, t, _ = x_sample.shape
    head_dim = q_norm_g.shape[-1]
    n_kv = cache_k_win.shape[3]
    q_dim = sinks.shape[-1] * head_dim
    kv_dim = n_kv * head_dim
    keep = min(cache_k_win.shape[2], s)

    l = 0
    yp, ys, qkv, vn_s = _layer(
        x_prompt.reshape(s, d), x_sample.reshape(dec_batch * t, d), cache_k_win[l], cache_v_win[l],
        norm1_g[l], w_in[l], gmlp_norm_g[l], w_s[l], b_s[l], q_norm_g[l], k_norm_g[l], sinks[l],
        p_a[l], p_b[l], w_out[l], norm2_g[l], w_rg[l], b_rg[l], w_re[l], b_re[l],
        w_gate_e[l], w_up_e[l], w_down_e[l])

    k_all = qkv[:, q_dim:q_dim + kv_dim]
    v_all = qkv[:, q_dim + kv_dim:]
    k_win_p = k_all[s - keep:s].reshape(1, batch, keep, n_kv, head_dim)
    v_win_p = v_all[s - keep:s].reshape(1, batch, keep, n_kv, head_dim)
    k_new_s = k_all[s:].reshape(1, dec_batch, t, n_kv, head_dim)
    v_new_s = v_all[s:].reshape(1, dec_batch, t, n_kv, head_dim)
    gv_s = vn_s.reshape(1, dec_batch, t, -1)
    return (yp.reshape(batch, s, d), ys.reshape(dec_batch, t, d), k_win_p, v_win_p, k_new_s, v_new_s, gv_s)
```

```python
import functools

import jax
import jax.numpy as jnp
from jax import lax
from jax.experimental import pallas as pl
from jax.experimental.pallas import tpu as pltpu

F32 = jnp.float32
BF16 = jnp.bfloat16
I32 = jnp.int32
U32 = jnp.uint32

EPS = 1e-6
PAST_LEN = 1024
CHUNK = 64
ROPE_THETA = 500000.0
TOP_K = 2
LANES = 128

VMEM_LIMIT_BYTES = 56 * 1024 * 1024

ROW_TILE = 1024
WIDE_ROW_TILE = 1536
COL_TILE = 512
NORM_ROWS = 256
ROUTE_ROWS = 256
MOE_ROWS = 320
COMBINE_ROWS = 256
DISPATCH_ROWS = 256
DMA_UNROLL = 8


def _cparams(sem):
    return pltpu.CompilerParams(dimension_semantics=sem, vmem_limit_bytes=VMEM_LIMIT_BYTES)


def _cast_rows(src_ref, dst_ref, rows):
    n = src_ref.shape[0] // rows

    def body(r, c):
        sl = pl.ds(pl.multiple_of(r * rows, rows), rows)
        dst_ref[sl, :] = src_ref[sl, :].astype(dst_ref.dtype)
        return c

    lax.fori_loop(0, n, body, 0)


def _rms(x, gain):
    ms = jnp.mean(x * x, axis=-1, keepdims=True)
    return x * lax.rsqrt(ms + EPS) * gain


def _norm1_kernel(xp_ref, xs_ref, g_ref, h_ref, *, n_prompt_blocks):
    i = pl.program_id(0)

    @pl.when(i < n_prompt_blocks)
    def _():
        h_ref[...] = _rms(xp_ref[...], g_ref[...]).astype(h_ref.dtype)

    @pl.when(i >= n_prompt_blocks)
    def _():
        h_ref[...] = _rms(xs_ref[...], g_ref[...]).astype(h_ref.dtype)


def _norm1(xp, xs, gain):
    s, d = xp.shape
    t = xs.shape[0]
    br = min(NORM_ROWS, t)
    nbp, nbs = s // br, t // br
    return pl.pallas_call(
        functools.partial(_norm1_kernel, n_prompt_blocks=nbp),
        grid=(nbp + nbs,),
        in_specs=[
            pl.BlockSpec((br, d), lambda i: (jnp.minimum(i, nbp - 1), 0)),
            pl.BlockSpec((br, d), lambda i: (jnp.maximum(i - nbp, 0), 0)),
            pl.BlockSpec((1, d), lambda i: (0, 0)),
        ],
        out_specs=pl.BlockSpec((br, d), lambda i: (i, 0)),
        out_shape=jax.ShapeDtypeStruct((s + t, d), BF16),
        compiler_params=_cparams(("arbitrary",)),
        name="norm1",
    )(xp, xs, gain)


def _inproj_act_kernel(h_ref, w_ref, o_ref, wb_ref, *, act):
    @pl.when(pl.program_id(1) == 0)
    def _():
        _cast_rows(w_ref, wb_ref, 256)

    z = jnp.dot(h_ref[...], wb_ref[...], preferred_element_type=F32)
    o_ref[...] = act(z).astype(o_ref.dtype)


def _inproj_act(h, w, col0, ncols, act, out_dtype, name):
    m, d = h.shape
    bm = WIDE_ROW_TILE if m % WIDE_ROW_TILE == 0 else min(ROW_TILE, m)
    bn = min(COL_TILE, ncols)
    assert col0 % bn == 0 and ncols % bn == 0 and m % bm == 0
    off = col0 // bn
    return pl.pallas_call(
        functools.partial(_inproj_act_kernel, act=act),
        grid=(ncols // bn, m // bm),
        in_specs=[
            pl.BlockSpec((bm, d), lambda n, i: (i, 0)),
            pl.BlockSpec((d, bn), lambda n, i: (0, n + off)),
        ],
        out_specs=pl.BlockSpec((bm, bn), lambda n, i: (i, n)),
        out_shape=jax.ShapeDtypeStruct((m, ncols), out_dtype),
        scratch_shapes=[pltpu.VMEM((d, bn), BF16)],
        compiler_params=_cparams(("arbitrary", "arbitrary")),
        name=name,
    )(h, w)


def _inproj_qkv_kernel(h_ref, w_ref, gain_ref, cos_ref, sa_ref, sb_ref, o_ref, wb_ref, *,
                       n_norm_tiles, head_dim, rot_half):
    n = pl.program_id(0)

    @pl.when(pl.program_id(1) == 0)
    def _():
        _cast_rows(w_ref, wb_ref, 256)

    z = jnp.dot(h_ref[...], wb_ref[...], preferred_element_type=F32)
    bn = z.shape[1]

    @pl.when(n < n_norm_tiles)
    def _():
        r = lax.broadcasted_iota(I32, (bn, bn), 0) // head_dim
        c = lax.broadcasted_iota(I32, (bn, bn), 1) // head_dim
        seg = (r == c).astype(BF16)
        ssq = jnp.dot((z * z).astype(BF16), seg, preferred_element_type=F32)
        y = z * lax.rsqrt(ssq * (1.0 / head_dim) + EPS) * gain_ref[...]
        reps = bn // cos_ref.shape[1]
        cosv = jnp.tile(cos_ref[...], (1, reps))
        sa = jnp.tile(sa_ref[...], (1, reps))
        sb = jnp.tile(sb_ref[...], (1, reps))
        y = y * cosv + pltpu.roll(y, bn - rot_half, 1) * sa + pltpu.roll(y, rot_half, 1) * sb
        o_ref[...] = y

    @pl.when(n >= n_norm_tiles)
    def _():
        o_ref[...] = z


def _inproj_qkv(h, w, col0, q_dim, kv_dim, gain_row, cos_t, sa_t, sb_t, head_dim, rot_half):
    m, d = h.shape
    ncols = q_dim + 2 * kv_dim
    bm, bn = min(ROW_TILE, m), kv_dim
    assert col0 % bn == 0 and q_dim % bn == 0 and bn % LANES == 0 and m % bm == 0
    off = col0 // bn
    n_norm_tiles = (q_dim + kv_dim) // bn
    tw = cos_t.shape[1]
    return pl.pallas_call(
        functools.partial(_inproj_qkv_kernel, n_norm_tiles=n_norm_tiles, head_dim=head_dim,
                          rot_half=rot_half),
        grid=(ncols // bn, m // bm),
        in_specs=[
            pl.BlockSpec((bm, d), lambda n, i: (i, 0)),
            pl.BlockSpec((d, bn), lambda n, i: (0, n + off)),
            pl.BlockSpec((1, bn), lambda n, i: (0, n)),
            pl.BlockSpec((bm, tw), lambda n, i: (i, 0)),
            pl.BlockSpec((bm, tw), lambda n, i: (i, 0)),
            pl.BlockSpec((bm, tw), lambda n, i: (i, 0)),
        ],
        out_specs=pl.BlockSpec((bm, bn), lambda n, i: (i, n)),
        out_shape=jax.ShapeDtypeStruct((m, ncols), F32),
        scratch_shapes=[pltpu.VMEM((d, bn), BF16)],
        compiler_params=_cparams(("arbitrary", "arbitrary")),
        name="inproj_qkv",
    )(h, w, gain_row, cos_t, sa_t, sb_t)


def _gmlp_kernel(u_ref, va_ref, wp_ref, ws_ref, bp_ref, bs_ref, gain_ref, o_ref, vn_ref, *,
                 n_prompt_blocks, groups, sample_len):
    i = pl.program_id(0)
    rows, d_a = u_ref.shape
    gw = d_a // groups
    ri = lax.broadcasted_iota(I32, (rows, rows), 0)
    ci = lax.broadcasted_iota(I32, (rows, rows), 1)

    def run(w_ref, b_ref, sub, emit_vn):
        vn = _rms(va_ref[...].astype(F32), gain_ref[...])
        if emit_vn:
            vn_ref[...] = vn
        vb = vn.astype(BF16)
        mask = (ci <= ri) & ((ri // sub) == (ci // sub))
        for g in range(groups):
            sl = slice(g * gw, (g + 1) * gw)
            wg = jnp.where(mask, w_ref[g], 0.0).astype(BF16)
            s = jnp.dot(wg, vb[:, sl], preferred_element_type=F32) + b_ref[:, g:g + 1]
            o_ref[:, sl] = (u_ref[:, sl].astype(F32) * s).astype(o_ref.dtype)

    @pl.when(i < n_prompt_blocks)
    def _():
        run(wp_ref, bp_ref, rows, False)

    @pl.when(i >= n_prompt_blocks)
    def _():
        run(ws_ref, bs_ref, sample_len, True)


def _gmlp(ua, w_prompt, w_sample, b_prompt, b_sample, gain, n_prompt_rows, sample_len):
    m, two_da = ua.shape
    d_a = two_da // 2
    groups, rows, _ = w_prompt.shape
    nbp = n_prompt_rows // rows
    nb = m // rows
    full = lambda i: (0, 0, 0)
    return pl.pallas_call(
        functools.partial(_gmlp_kernel, n_prompt_blocks=nbp, groups=groups, sample_len=sample_len),
        grid=(nb,),
        in_specs=[
            pl.BlockSpec((rows, d_a), lambda i: (i, 0)),
            pl.BlockSpec((rows, d_a), lambda i: (i, 1)),
            pl.BlockSpec((groups, rows, rows), full),
            pl.BlockSpec((groups, rows, rows), full),
            pl.BlockSpec((rows, groups), lambda i: (0, 0)),
            pl.BlockSpec((rows, groups), lambda i: (0, 0)),
            pl.BlockSpec((1, d_a), lambda i: (0, 0)),
        ],
        out_specs=[
            pl.BlockSpec((rows, d_a), lambda i: (i, 0)),
            pl.BlockSpec((rows, d_a), lambda i: (jnp.maximum(i - nbp, 0), 0)),
        ],
        out_shape=[
            jax.ShapeDtypeStruct((m, d_a), BF16),
            jax.ShapeDtypeStruct((m - n_prompt_rows, d_a), F32),
        ],
        compiler_params=_cparams(("arbitrary",)),
        name="gmlp_spatial",
    )(ua, ua, w_prompt, w_sample, b_prompt, b_sample, gain)


def _attn_kernel(bias_ref, q_ref, ka_ref, kb_ref, kc_ref, kd_ref, va_ref, vb_ref, vc_ref, vd_ref,
                 ck_ref, cv_ref, o_ref, s_ref, e_ref, *, n_prompt_steps, tiles_per_kv, head_dim):
    i = pl.program_id(0)
    t = q_ref.shape[0] // 2
    kv_dim = kc_ref.shape[1]
    lkp = bias_ref.shape[1] // 2
    scale = head_dim ** -0.5
    nt = (((1,), (1,)), ((), ()))
    n_kv = kv_dim // head_dim
    tiles_of = lambda j: [j * tiles_per_kv + a for a in range(tiles_per_kv)]

    def run(halves):
        prepared = []
        for k_parts, v_parts, first_key_chunk in halves:
            k = jnp.concatenate(k_parts, axis=0)
            v = jnp.concatenate(v_parts, axis=0)
            chunk_bias = None
            if first_key_chunk is not None:
                col = lax.broadcasted_iota(I32, (1, 2 * lkp), 1)
                key_chunk = first_key_chunk + (col % lkp) // t
                chunk_bias = jnp.where(key_chunk >= 0, 0.0, -jnp.inf)
            prepared.append((k, v, chunk_bias))
        lk = prepared[0][0].shape[0]
        lower = lax.broadcasted_iota(I32, (lk, LANES), 1) < head_dim
        zpad = jnp.zeros((lkp - lk, LANES), F32)
        rr = lax.broadcasted_iota(I32, (2 * lkp, LANES), 0) < lkp
        rl = lax.broadcasted_iota(I32, (2 * lkp, LANES), 1) < head_dim
        ones_sel = (rr == rl).astype(BF16)

        def doubled(tile, rolled, x):
            lo_src, hi_src = (tile, rolled) if x == 0 else (rolled, tile)
            return jnp.concatenate([jnp.where(lower, lo_src, 0.0), zpad,
                                    jnp.where(lower, 0.0, hi_src), zpad], axis=0).astype(BF16)

        for h, (k, _, chunk_bias) in enumerate(prepared):
            rows_h = slice(h * t, (h + 1) * t)
            for b in range(kv_dim // LANES):
                kt = k[:, b * LANES:(b + 1) * LANES]
                kr = pltpu.roll(kt, head_dim, 1)
                for x in range(2):
                    j = 2 * b + x
                    q2 = jnp.concatenate([q_ref[rows_h, a * LANES:(a + 1) * LANES] for a in tiles_of(j)],
                                         axis=0)
                    q2 = (q2 * scale).astype(BF16)
                    rows = []
                    for a in tiles_of(j):
                        row = bias_ref[a:a + 1, :]
                        if chunk_bias is not None:
                            row = row + chunk_bias
                        rows.append(jnp.broadcast_to(row, (t, 2 * lkp)))
                    s_ref[h, j] = (lax.dot_general(q2, doubled(kt, kr, x), nt, preferred_element_type=F32)
                                   + jnp.concatenate(rows, axis=0))

        for h in range(len(prepared)):
            for j in range(n_kv):
                for hs in (slice(0, lkp), slice(lkp, 2 * lkp)):
                    sh = s_ref[h, j, :, hs]
                    e_ref[h, j, :, hs] = jnp.exp(sh - jnp.max(sh, axis=-1, keepdims=True)).astype(BF16)

        for h, (_, v, _) in enumerate(prepared):
            for b in range(kv_dim // LANES):
                vt = v[:, b * LANES:(b + 1) * LANES]
                vr = pltpu.roll(vt, head_dim, 1)
                for x in range(2):
                    j = 2 * b + x
                    rhs = jnp.concatenate([doubled(vt, vr, x), ones_sel], axis=1)
                    r = jnp.dot(e_ref[h, j], rhs, preferred_element_type=F32)
                    o = r[:, :LANES] / r[:, LANES:]
                    for n, a in enumerate(tiles_of(j)):
                        o_ref[h * t:(h + 1) * t, a * LANES:(a + 1) * LANES] = (
                            o[n * t:(n + 1) * t].astype(o_ref.dtype))

    @pl.when(i < n_prompt_steps)
    def _():
        ka, kb, kc, kd = ka_ref[...], kb_ref[...], kc_ref[...], kd_ref[...]
        va, vb, vc, vd = va_ref[...], vb_ref[...], vc_ref[...], vd_ref[...]
        run([([ka, kb, kc], [va, vb, vc], 2 * i - 2), ([kb, kc, kd], [vb, vc, vd], 2 * i - 1)])

    @pl.when(i >= n_prompt_steps)
    def _():
        w = ck_ref.shape[0] // 2
        run([([ck_ref[:w, :], kc_ref[...]], [cv_ref[:w, :], vc_ref[...]], None),
             ([ck_ref[w:, :], kd_ref[...]], [cv_ref[w:, :], vd_ref[...]], None)])


def _attention(qkv, cache_k, cache_v, sinks, n_prompt_rows, q_dim, kv_dim, head_dim, n_heads):
    m = qkv.shape[0]
    t = CHUNK
    nc = n_prompt_rows // t
    n_kv_heads = kv_dim // head_dim
    window = cache_k.shape[0] // ((m - n_prompt_rows) // t)
    kcol, vcol = q_dim // kv_dim, q_dim // kv_dim + 1
    gqa = n_heads // n_kv_heads
    assert 2 * head_dim == LANES and gqa % 2 == 0 and kv_dim % LANES == 0

    lk = window + t
    lkp = -(-(lk + 1) // LANES) * LANES
    pair = sinks.reshape(n_heads // 2, 2, 1)
    bias = jnp.concatenate([jnp.zeros((n_heads // 2, 2, lk), F32), pair,
                            jnp.full((n_heads // 2, 2, lkp - lk - 1), -jnp.inf, F32)], axis=2)
    bias = bias.reshape(n_heads // 2, 2 * lkp)

    assert nc % 2 == 0 and (m // t - nc) % 2 == 0, "query chunks are processed in pairs"
    nps = nc // 2

    def kv_spec(off, col):
        return pl.BlockSpec((t, kv_dim), lambda i: (jnp.maximum(2 * i + off, 0), col))

    cache_spec = pl.BlockSpec((2 * window, kv_dim), lambda i: (jnp.maximum(i - nps, 0), 0))
    scratch = (2, n_kv_heads, gqa // 2 * t, 2 * lkp)
    return pl.pallas_call(
        functools.partial(_attn_kernel, n_prompt_steps=nps, tiles_per_kv=gqa // 2, head_dim=head_dim),
        grid=(m // (2 * t),),
        in_specs=[
            pl.BlockSpec((n_heads // 2, 2 * lkp), lambda i: (0, 0)),
            pl.BlockSpec((2 * t, q_dim), lambda i: (i, 0)),
            kv_spec(-2, kcol), kv_spec(-1, kcol), kv_spec(0, kcol), kv_spec(1, kcol),
            kv_spec(-2, vcol), kv_spec(-1, vcol), kv_spec(0, vcol), kv_spec(1, vcol),
            cache_spec, cache_spec,
        ],
        out_specs=pl.BlockSpec((2 * t, q_dim), lambda i: (i, 0)),
        out_shape=jax.ShapeDtypeStruct((m, q_dim), BF16),
        scratch_shapes=[pltpu.VMEM(scratch, F32), pltpu.VMEM(scratch, BF16)],
        compiler_params=_cparams(("arbitrary",)),
        name="window_attention",
    )(bias, qkv, qkv, qkv, qkv, qkv, qkv, qkv, qkv, qkv, cache_k, cache_v)


def _merge_kernel(oa_ref, ob_ref, pa_ref, pb_ref, ga_ref, gb_ref, o_ref, pab_ref, pbb_ref):
    @pl.when(pl.program_id(1) == 0)
    def _():
        _cast_rows(pa_ref, pab_ref, 256)
        _cast_rows(pb_ref, pbb_ref, 256)

    a = jnp.dot(oa_ref[...], pab_ref[...], preferred_element_type=F32)
    b = jnp.dot(ob_ref[...], pbb_ref[...], preferred_element_type=F32)
    o_ref[...] = (ga_ref[...].astype(F32) * a + gb_ref[...].astype(F32) * b).astype(o_ref.dtype)


def _merge(o_a, o_b, p_a, p_b, gates):
    m, d_a = o_a.shape
    q_dim = o_b.shape[1]
    d = p_a.shape[1]
    bm, bn = min(ROW_TILE, m), min(COL_TILE, d)
    goff = d // bn
    return pl.pallas_call(
        _merge_kernel,
        grid=(d // bn, m // bm),
        in_specs=[
            pl.BlockSpec((bm, d_a), lambda n, i: (i, 0)),
            pl.BlockSpec((bm, q_dim), lambda n, i: (i, 0)),
            pl.BlockSpec((d_a, bn), lambda n, i: (0, n)),
            pl.BlockSpec((q_dim, bn), lambda n, i: (0, n)),
            pl.BlockSpec((bm, bn), lambda n, i: (i, n)),
            pl.BlockSpec((bm, bn), lambda n, i: (i, n + goff)),
        ],
        out_specs=pl.BlockSpec((bm, bn), lambda n, i: (i, n)),
        out_shape=jax.ShapeDtypeStruct((m, d), BF16),
        scratch_shapes=[pltpu.VMEM((d_a, bn), BF16), pltpu.VMEM((q_dim, bn), BF16)],
        compiler_params=_cparams(("arbitrary", "arbitrary")),
        name="merge_proj",
    )(o_a, o_b, p_a, p_b, gates, gates)


def _outproj_kernel(t_ref, w_ref, xp_ref, xs_ref, o_ref, wb_ref, *, n_prompt_tiles):
    i = pl.program_id(1)

    @pl.when(i == 0)
    def _():
        _cast_rows(w_ref, wb_ref, 256)

    z = jnp.dot(t_ref[...], wb_ref[...], preferred_element_type=F32)

    @pl.when(i < n_prompt_tiles)
    def _():
        o_ref[...] = xp_ref[...] + z

    @pl.when(i >= n_prompt_tiles)
    def _():
        o_ref[...] = xs_ref[...] + z


def _outproj(tm, w_out, xp, xs):
    m, d = tm.shape
    s, t = xp.shape[0], xs.shape[0]
    bm, bn = min(ROW_TILE, t), min(COL_TILE, d)
    npt = s // bm
    return pl.pallas_call(
        functools.partial(_outproj_kernel, n_prompt_tiles=npt),
        grid=(d // bn, m // bm),
        in_specs=[
            pl.BlockSpec((bm, d), lambda n, i: (i, 0)),
            pl.BlockSpec((d, bn), lambda n, i: (0, n)),
            pl.BlockSpec((bm, bn), lambda n, i: (jnp.minimum(i, npt - 1), n)),
            pl.BlockSpec((bm, bn), lambda n, i: (jnp.maximum(i - npt, 0), n)),
        ],
        out_specs=pl.BlockSpec((bm, bn), lambda n, i: (i, n)),
        out_shape=jax.ShapeDtypeStruct((m, d), F32),
        scratch_shapes=[pltpu.VMEM((d, bn), BF16)],
        compiler_params=_cparams(("arbitrary", "arbitrary")),
        name="out_proj",
    )(tm, w_out, xp, xs)


def _router_kernel(x_ref, g_ref, wr_ref, br_ref, route_ref, cnt_ref, wcat_ref, carry_ref, *,
                   n_groups, per_group):
    i = pl.program_id(0)

    @pl.when(i == 0)
    def _():
        w = wr_ref[...]
        hi = w.astype(BF16)
        wcat_ref[:, :LANES] = hi
        wcat_ref[:, LANES:] = (w - hi.astype(F32)).astype(BF16)
        carry_ref[...] = jnp.zeros_like(carry_ref)

    hn = _rms(x_ref[...], g_ref[...])
    rows = hn.shape[0]
    hi = hn.astype(BF16)
    lo = (hn - hi.astype(F32)).astype(BF16)
    prod = jnp.dot(jnp.concatenate([hi, lo], axis=0), wcat_ref[...], preferred_element_type=F32)
    logits = prod[:rows, :LANES] + prod[:rows, LANES:] + prod[rows:, :LANES] + br_ref[...]
    lane = lax.broadcasted_iota(I32, logits.shape, 1)
    big = jnp.int32(LANES)

    lg = jnp.where(lane < n_groups, logits, -jnp.inf)
    mg = jnp.max(lg, axis=-1, keepdims=True)
    pg_sel = 1.0 / jnp.sum(jnp.exp(lg - mg), axis=-1, keepdims=True)
    gsel = jnp.min(jnp.where(lg == mg, lane, big), axis=-1, keepdims=True)

    first = n_groups + gsel * per_group
    emask = (lane >= first) & (lane < first + per_group)
    le = jnp.where(emask, logits, -jnp.inf)
    me = jnp.max(le, axis=-1, keepdims=True)
    ee = jnp.exp(le - me)
    pe = jnp.where(emask, ee / jnp.sum(ee, axis=-1, keepdims=True), -1.0)
    p1 = jnp.max(pe, axis=-1, keepdims=True)
    i1 = jnp.min(jnp.where(pe == p1, lane, big), axis=-1, keepdims=True)
    pe2 = jnp.where(lane == i1, -1.0, pe)
    p2 = jnp.max(pe2, axis=-1, keepdims=True)
    i2 = jnp.min(jnp.where(pe2 == p2, lane, big), axis=-1, keepdims=True)
    psum = p1 + p2
    w1 = p1 / psum * pg_sel
    w2 = p2 / psum * pg_sel
    e1 = i1 - n_groups
    e2 = i2 - n_groups

    oh1 = (lane == e1).astype(F32)
    oh2 = (lane == e2).astype(F32)
    ohs = oh1 + oh2
    ri = lax.broadcasted_iota(I32, (rows, rows), 0)
    ci = lax.broadcasted_iota(I32, (rows, rows), 1)
    below = (ci < ri).astype(BF16)
    before = jnp.dot(below, ohs.astype(BF16), preferred_element_type=F32) + carry_ref[...]
    r1 = jnp.sum(before * oh1, axis=-1, keepdims=True)
    r2 = jnp.sum(before * oh2, axis=-1, keepdims=True)
    carry_ref[...] = carry_ref[...] + jnp.sum(ohs, axis=0, keepdims=True)
    cnt_ref[...] = carry_ref[...]

    route = jnp.where(lane == 0, e1.astype(F32), 0.0)
    route = jnp.where(lane == 1, e2.astype(F32), route)
    route = jnp.where(lane == 2, w1, route)
    route = jnp.where(lane == 3, w2, route)
    route = jnp.where(lane == 4, r1, route)
    route = jnp.where(lane == 5, r2, route)
    route_ref[...] = route


def _router(x2, gain, wr, br, n_groups, per_group):
    m, d = x2.shape
    br_rows = min(ROUTE_ROWS, m)
    return pl.pallas_call(
        functools.partial(_router_kernel, n_groups=n_groups, per_group=per_group),
        grid=(m // br_rows,),
        in_specs=[
            pl.BlockSpec((br_rows, d), lambda i: (i, 0)),
            pl.BlockSpec((1, d), lambda i: (0, 0)),
            pl.BlockSpec((d, LANES), lambda i: (0, 0)),
            pl.BlockSpec((1, LANES), lambda i: (0, 0)),
        ],
        out_specs=[
            pl.BlockSpec((br_rows, LANES), lambda i: (i, 0)),
            pl.BlockSpec((1, LANES), lambda i: (0, 0)),
        ],
        out_shape=[
            jax.ShapeDtypeStruct((m, LANES), F32),
            jax.ShapeDtypeStruct((1, LANES), F32),
        ],
        scratch_shapes=[pltpu.VMEM((d, 2 * LANES), BF16), pltpu.VMEM((1, LANES), F32)],
        compiler_params=_cparams(("arbitrary",)),
        name="norm2_router",
    )(x2, gain, wr, br)


def _pack_pair(first, second):
    hi = lax.bitcast_convert_type(first.astype(BF16).astype(F32), U32)
    lo = lax.bitcast_convert_type(second.astype(BF16).astype(F32), U32)
    return hi | (lo >> 16)


def _pack_halves(x):
    half = x.shape[1] // 2
    return _pack_pair(x[:, :half], x[:, half:])


def _unpack_halves(p):
    first = lax.bitcast_convert_type(p & jnp.uint32(0xFFFF0000), F32)
    second = lax.bitcast_convert_type(p << 16, F32)
    return first, second


def _dispatch_kernel(dest_ref, x_ref, g_ref, xs_ref, buf_ref, sem, *, n_tiles):
    i = pl.program_id(0)
    tm = x_ref.shape[0]
    n_tokens = n_tiles * tm
    slot = i % 2

    def wait_rows(s):
        for k in range(TOP_K):
            pltpu.make_async_copy(buf_ref.at[s], xs_ref.at[pl.ds(0, tm)], sem.at[s]).wait()

    buf_ref[slot] = _pack_halves(_rms(x_ref[...], g_ref[...]))

    def start(r, c):
        for k in range(TOP_K):
            d = dest_ref[k * n_tokens + i * tm + r]
            pltpu.make_async_copy(buf_ref.at[slot, pl.ds(r, 1)], xs_ref.at[pl.ds(d, 1)],
                                  sem.at[slot]).start()
        return c

    lax.fori_loop(0, tm, start, 0, unroll=DMA_UNROLL)

    @pl.when(i > 0)
    def _():
        wait_rows(1 - slot)

    @pl.when(i == n_tiles - 1)
    def _():
        wait_rows(slot)


def _dispatch(dest_flat, x2, gain, n_rows):
    m, d = x2.shape
    tm = min(DISPATCH_ROWS, m)
    return pl.pallas_call(
        functools.partial(_dispatch_kernel, n_tiles=m // tm),
        grid_spec=pltpu.PrefetchScalarGridSpec(
            num_scalar_prefetch=1,
            grid=(m // tm,),
            in_specs=[
                pl.BlockSpec((tm, d), lambda i, dest: (i, 0)),
                pl.BlockSpec((1, d), lambda i, dest: (0, 0)),
            ],
            out_specs=pl.BlockSpec(memory_space=pl.ANY),
            scratch_shapes=[pltpu.VMEM((2, tm, d // 2), U32), pltpu.SemaphoreType.DMA((2,))],
        ),
        out_shape=jax.ShapeDtypeStruct((n_rows, d // 2), U32),
        compiler_params=_cparams(("arbitrary",)),
        name="moe_dispatch",
    )(dest_flat, x2, gain)


def _moe_kernel(be_ref, bidx_ref, nrows_ref, nv_ref, xs_ref, wg_ref, wu_ref, wd_ref, y_ref,
                x1_ref, x2_ref, act_ref, *, n_split):
    i = pl.program_id(0)
    j = pl.program_id(1)

    @pl.when(i < nv_ref[0])
    def _():
        rows, half = xs_ref.shape
        dh = wg_ref.shape[2]

        @pl.when(j == 0)
        def _():
            live = lax.broadcasted_iota(I32, (rows, 1), 0) < nrows_ref[i]
            x1, x2 = _unpack_halves(xs_ref[...])
            x1_ref[...] = jnp.where(live, x1, 0.0).astype(BF16)
            x2_ref[...] = jnp.where(live, x2, 0.0).astype(BF16)

        def proj(w_ref):
            return (jnp.dot(x1_ref[...], w_ref[0, :half, :].astype(BF16), preferred_element_type=F32)
                    + jnp.dot(x2_ref[...], w_ref[0, half:, :].astype(BF16), preferred_element_type=F32))

        act = (jax.nn.silu(proj(wg_ref)) * proj(wu_ref)).astype(BF16)
        for q in range(n_split):
            @pl.when(j == q)
            def _(q=q):
                act_ref[:, q * dh:(q + 1) * dh] = act

        @pl.when(j == n_split - 1)
        def _():
            a = act_ref[...]
            first = jnp.dot(a, wd_ref[0, :, :half].astype(BF16), preferred_element_type=F32)
            second = jnp.dot(a, wd_ref[0, :, half:].astype(BF16), preferred_element_type=F32)
            y_ref[...] = _pack_pair(first, second)


def _moe_experts(block_e, block_idx, block_rows, n_valid, xs, w_gate, w_up, w_down, n_blocks):
    half = xs.shape[1]
    d = 2 * half
    d_e = w_gate.shape[2]
    n_split = 2
    dh = d_e // n_split
    br = MOE_ROWS

    def col(i, j, nv):
        return jnp.where(i < nv[0], j, n_split - 1)

    return pl.pallas_call(
        functools.partial(_moe_kernel, n_split=n_split),
        grid_spec=pltpu.PrefetchScalarGridSpec(
            num_scalar_prefetch=4,
            grid=(n_blocks, n_split),
            in_specs=[
                pl.BlockSpec((br, half), lambda i, j, be, bi, nr, nv: (bi[i], 0)),
                pl.BlockSpec((1, d, dh), lambda i, j, be, bi, nr, nv: (be[i], 0, col(i, j, nv))),
                pl.BlockSpec((1, d, dh), lambda i, j, be, bi, nr, nv: (be[i], 0, col(i, j, nv))),
                pl.BlockSpec((1, d_e, d), lambda i, j, be, bi, nr, nv: (be[i], 0, 0)),
            ],
            out_specs=pl.BlockSpec((br, half), lambda i, j, be, bi, nr, nv: (bi[i], 0)),
            scratch_shapes=[pltpu.VMEM((br, half), BF16), pltpu.VMEM((br, half), BF16),
                            pltpu.VMEM((br, d_e), BF16)],
        ),
        out_shape=jax.ShapeDtypeStruct(xs.shape, U32),
        compiler_params=_cparams(("arbitrary", "arbitrary")),
        name="moe_experts",
    )(block_e, block_idx, block_rows, n_valid, xs, w_gate, w_up, w_down)


def _combine_kernel(dest_ref, x_ref, route_ref, y_ref, op_ref, os_ref, buf_ref, sem, *,
                    n_prompt_tiles, n_tiles):
    i = pl.program_id(0)
    tm = x_ref.shape[0]
    n_tokens = n_tiles * tm
    slot = i % 2

    def gather(tile, to_slot):
        def start(r, c):
            for k in range(TOP_K):
                d = dest_ref[k * n_tokens + tile * tm + r]
                pltpu.make_async_copy(y_ref.at[pl.ds(d, 1)], buf_ref.at[to_slot, k, pl.ds(r, 1)],
                                      sem.at[to_slot]).start()
            return c

        lax.fori_loop(0, tm, start, 0, unroll=DMA_UNROLL)

    @pl.when(i == 0)
    def _():
        gather(0, 0)

    @pl.when(i + 1 < n_tiles)
    def _():
        gather(i + 1, 1 - slot)

    for k in range(TOP_K):
        pltpu.make_async_copy(y_ref.at[pl.ds(0, tm)], buf_ref.at[slot, k], sem.at[slot]).wait()

    half = x_ref.shape[1] // 2
    first, second = x_ref[:, :half], x_ref[:, half:]
    for k in range(TOP_K):
        y1, y2 = _unpack_halves(buf_ref[slot, k])
        w = route_ref[:, 2 + k:3 + k]
        first = first + w * y1
        second = second + w * y2
    out = jnp.concatenate([first, second], axis=1)

    @pl.when(i < n_prompt_tiles)
    def _():
        op_ref[...] = out

    @pl.when(i >= n_prompt_tiles)
    def _():
        os_ref[...] = out


def _combine(dest_flat, x2, route, y, n_prompt_rows):
    m, d = x2.shape
    tm = COMBINE_ROWS
    npt = n_prompt_rows // tm
    return pl.pallas_call(
        functools.partial(_combine_kernel, n_prompt_tiles=npt, n_tiles=m // tm),
        grid_spec=pltpu.PrefetchScalarGridSpec(
            num_scalar_prefetch=1,
            grid=(m // tm,),
            in_specs=[
                pl.BlockSpec((tm, d), lambda i, dest: (i, 0)),
                pl.BlockSpec((tm, LANES), lambda i, dest: (i, 0)),
                pl.BlockSpec(memory_space=pl.ANY),
            ],
            out_specs=[
                pl.BlockSpec((tm, d), lambda i, dest: (jnp.minimum(i, npt - 1), 0)),
                pl.BlockSpec((tm, d), lambda i, dest: (jnp.maximum(i - npt, 0), 0)),
            ],
            scratch_shapes=[pltpu.VMEM((2, TOP_K, tm, d // 2), U32), pltpu.SemaphoreType.DMA((2,))],
        ),
        out_shape=[
            jax.ShapeDtypeStruct((n_prompt_rows, d), F32),
            jax.ShapeDtypeStruct((m - n_prompt_rows, d), F32),
        ],
        compiler_params=_cparams(("arbitrary",)),
        name="moe_combine",
    )(dest_flat, x2, route, y)


def _rope_tables(positions, head_dim):
    rot_dim = head_dim // 4
    half = rot_dim // 2
    inv_freq = jnp.power(ROPE_THETA, -jnp.arange(half, dtype=F32) * 2.0 / rot_dim)
    ang = positions.astype(F32)[:, None] * inv_freq[None, :]
    cos, sin = lax.optimization_barrier((jnp.cos(ang), jnp.sin(ang)))
    m = positions.shape[0]
    zeros = lambda n: jnp.zeros((m, n), F32)
    cos_h = jnp.concatenate([cos, cos, jnp.ones((m, head_dim - rot_dim), F32)], axis=1)
    sa_h = jnp.concatenate([-sin, zeros(head_dim - half)], axis=1)
    sb_h = jnp.concatenate([zeros(half), sin, zeros(head_dim - rot_dim)], axis=1)
    reps = LANES // head_dim
    return tuple(jnp.tile(a, (1, reps)) for a in (cos_h, sa_h, sb_h)), half


def _layer(xp, xs, cache_k, cache_v, norm1_g, w_in, gmlp_norm_g, w_s, b_s, q_norm_g, k_norm_g,
           sinks, p_a, p_b, w_out, norm2_g, w_rg, b_rg, w_re, b_re, w_gate_e, w_up_e, w_down_e):
    s, d = xp.shape
    dec_rows = xs.shape[0]
    dec_batch = cache_k.shape[0]
    t = dec_rows // dec_batch
    m = s + dec_rows
    d_a = gmlp_norm_g.shape[0]
    head_dim = q_norm_g.shape[0]
    n_heads = sinks.shape[0]
    q_dim = n_heads * head_dim
    kv_dim = cache_k.shape[2] * cache_k.shape[3]
    n_groups, per_group = w_re.shape[1], w_re.shape[2]
    n_experts = n_groups * per_group

    h = _norm1(xp, xs, norm1_g[None, :])
    ua = _inproj_act(h, w_in, 0, 2 * d_a, jax.nn.gelu, BF16, "inproj_gelu")
    positions = jnp.concatenate(
        [jnp.arange(s, dtype=I32), jnp.tile(PAST_LEN + jnp.arange(t, dtype=I32), dec_batch)])
    (cos_t, sa_t, sb_t), rot_half = _rope_tables(positions, head_dim)
    gain_row = jnp.concatenate([jnp.tile(q_norm_g, n_heads), jnp.tile(k_norm_g, kv_dim // head_dim),
                                jnp.ones((kv_dim,), F32)])[None, :]
    qkv = _inproj_qkv(h, w_in, 2 * d_a, q_dim, kv_dim, gain_row, cos_t, sa_t, sb_t, head_dim, rot_half)
    gates = _inproj_act(h, w_in, 2 * d_a + q_dim + 2 * kv_dim, 2 * d, jax.nn.sigmoid, BF16,
                        "inproj_gate")

    rows = w_s.shape[1]
    reps = rows // t
    w_sample = jnp.tile(w_s[:, :t, :t], (1, reps, reps))
    b_sample = jnp.tile(b_s[:, :t], (1, reps))
    o_a, vn_s = _gmlp(ua, w_s, w_sample, b_s.T, b_sample.T, gmlp_norm_g[None, :], s, t)
    o_b = _attention(qkv, cache_k.reshape(-1, kv_dim), cache_v.reshape(-1, kv_dim), sinks[None, :],
                     s, q_dim, kv_dim, head_dim, n_heads)

    merged = _merge(o_a, o_b, p_a, p_b, gates)
    x2 = _outproj(merged, w_out, xp, xs)

    pad = LANES - n_groups - n_experts
    wr = jnp.concatenate([w_rg, w_re.reshape(d, n_experts), jnp.zeros((d, pad), F32)], axis=1)
    br = jnp.concatenate([b_rg, b_re.reshape(n_experts), jnp.zeros((pad,), F32)])[None, :]
    route, counts = _router(x2, norm2_g[None, :], wr, br, n_groups, per_group)
    e_idx = route[:, 0:TOP_K].T.astype(I32)
    rank = route[:, 4:4 + TOP_K].T.astype(I32)
    counts = counts[0, :n_experts].astype(I32)

    blk = MOE_ROWS
    n_blocks = -(-(m * TOP_K) // blk) + n_experts
    padded = ((counts + blk - 1) // blk) * blk
    pend = jnp.cumsum(padded)
    pstart = pend - padded
    hit = e_idx[:, :, None] == jnp.arange(n_experts, dtype=I32)
    dest = (jnp.sum(jnp.where(hit, pstart, 0), axis=-1) + rank).reshape(-1).astype(I32)
    n_valid = (pend[-1] // blk).astype(I32)
    ids = jnp.arange(n_blocks, dtype=I32)
    block_idx = jnp.minimum(ids, n_valid - 1)
    first_row = block_idx * blk
    block_e = jnp.minimum(jnp.sum((pend[None, :] <= first_row[:, None]).astype(I32), axis=1), n_experts - 1)
    block_rows = jnp.clip(counts[block_e] - (first_row - pstart[block_e]), 0, blk).astype(I32)

    xs_sorted = _dispatch(dest, x2, norm2_g[None, :], n_blocks * blk)
    y_sorted = _moe_experts(block_e, block_idx, block_rows, n_valid[None], xs_sorted,
                            w_gate_e, w_up_e, w_down_e, n_blocks)
    yp, ys = _combine(dest, x2, route, y_sorted, s)
    return yp, ys, qkv, vn_s


def kernel(x_prompt, x_sample, cache_k_win, cache_v_win, norm1_g, w_in, gmlp_norm_g, w_s, b_s,
           q_norm_g, k_norm_g, sinks, p_a, p_b, w_out, norm2_g, w_rg, b_rg, w_re, b_re,
           w_gate_e, w_up_e, w_down_e):
    depth = norm1_g.shape[0]
    assert depth == 1, "weights of one layer are expected"
    batch, s, d = x_prompt.shape
    assert batch == 1
    dec_batch, t, _ = x_sample.shape
    head_dim = q_norm_g.shape[-1]
    n_kv = cache_k_win.shape[3]
    q_dim = sinks.shape[-1] * head_dim
    kv_dim = n_kv * head_dim
    keep = min(cache_k_win.shape[2], s)

    l = 0
    yp, ys, qkv, vn_s = _layer(
        x_prompt.reshape(s, d), x_sample.reshape(dec_batch * t, d), cache_k_win[l], cache_v_win[l],
        norm1_g[l], w_in[l], gmlp_norm_g[l], w_s[l], b_s[l], q_norm_g[l], k_norm_g[l], sinks[l],
        p_a[l], p_b[l], w_out[l], norm2_g[l], w_rg[l], b_rg[l], w_re[l], b_re[l],
        w_gate_e[l], w_up_e[l], w_down_e[l])

    k_all = qkv[:, q_dim:q_dim + kv_dim]
    v_all = qkv[:, q_dim + kv_dim:]
    k_win_p = k_all[s - keep:s].reshape(1, batch, keep, n_kv, head_dim)
    v_win_p = v_all[s - keep:s].reshape(1, batch, keep, n_kv, head_dim)
    k_new_s = k_all[s:].reshape(1, dec_batch, t, n_kv, head_dim)
    v_new_s = v_all[s:].reshape(1, dec_batch, t, n_kv, head_dim)
    gv_s = vn_s.reshape(1, dec_batch, t, -1)
    return (yp.reshape(batch, s, d), ys.reshape(dec_batch, t, d), k_win_p, v_win_p, k_new_s, v_new_s, gv_s)
```

```python
import functools

import jax
import jax.numpy as jnp
from jax import lax
from jax.experimental import pallas as pl
from jax.experimental.pallas import tpu as pltpu

F32 = jnp.float32
BF16 = jnp.bfloat16
I32 = jnp.int32
U32 = jnp.uint32

EPS = 1e-6
PAST_LEN = 1024
CHUNK = 64
ROPE_THETA = 500000.0
TOP_K = 2
LANES = 128

VMEM_LIMIT_BYTES = 56 * 1024 * 1024
FUSED_VMEM_LIMIT_BYTES = 58 * 1024 * 1024

ROW_TILE = 1024
WIDE_ROW_TILE = 1536
COL_TILE = 512
GMLP_ROWS = 128
NORM_ROWS = 256
ROUTE_ROWS = 256
MOE_ROWS = 320
COMBINE_ROWS = 256
DISPATCH_ROWS = 256
DMA_UNROLL = 8


def _cparams(sem, vmem_limit_bytes=VMEM_LIMIT_BYTES):
    return pltpu.CompilerParams(dimension_semantics=sem, vmem_limit_bytes=vmem_limit_bytes)


def _cast_rows(src_ref, dst_ref, rows):
    n = src_ref.shape[0] // rows

    def body(r, c):
        sl = pl.ds(pl.multiple_of(r * rows, rows), rows)
        dst_ref[sl, :] = src_ref[sl, :].astype(dst_ref.dtype)
        return c

    lax.fori_loop(0, n, body, 0)


def _rms(x, gain):
    ms = jnp.mean(x * x, axis=-1, keepdims=True)
    return x * lax.rsqrt(ms + EPS) * gain


def _norm1_kernel(xp_ref, xs_ref, g_ref, h_ref, *, n_prompt_blocks):
    i = pl.program_id(0)

    @pl.when(i < n_prompt_blocks)
    def _():
        h_ref[...] = _rms(xp_ref[...], g_ref[...]).astype(h_ref.dtype)

    @pl.when(i >= n_prompt_blocks)
    def _():
        h_ref[...] = _rms(xs_ref[...], g_ref[...]).astype(h_ref.dtype)


def _norm1(xp, xs, gain):
    s, d = xp.shape
    t = xs.shape[0]
    br = min(NORM_ROWS, t)
    nbp, nbs = s // br, t // br
    return pl.pallas_call(
        functools.partial(_norm1_kernel, n_prompt_blocks=nbp),
        grid=(nbp + nbs,),
        in_specs=[
            pl.BlockSpec((br, d), lambda i: (jnp.minimum(i, nbp - 1), 0)),
            pl.BlockSpec((br, d), lambda i: (jnp.maximum(i - nbp, 0), 0)),
            pl.BlockSpec((1, d), lambda i: (0, 0)),
        ],
        out_specs=pl.BlockSpec((br, d), lambda i: (i, 0)),
        out_shape=jax.ShapeDtypeStruct((s + t, d), BF16),
        compiler_params=_cparams(("arbitrary",)),
        name="norm1",
    )(xp, xs, gain)


def _host_row_tile(m, n_col_tiles, n_side_steps, preferred):
    bm = preferred if m % preferred == 0 else min(ROW_TILE, m)
    while n_col_tiles * (m // bm) < n_side_steps:
        assert bm % 16 == 0
        bm //= 2
    assert m % bm == 0
    return bm


def _inproj_qkv_kernel(h_ref, w_ref, gain_ref, cos_ref, sa_ref, sb_ref, o_ref, wb_ref, *,
                       n_norm_tiles, head_dim, rot_half):
    n = pl.program_id(0)

    @pl.when(pl.program_id(1) == 0)
    def _():
        _cast_rows(w_ref, wb_ref, 256)

    z = jnp.dot(h_ref[...], wb_ref[...], preferred_element_type=F32)
    bn = z.shape[1]

    @pl.when(n < n_norm_tiles)
    def _():
        r = lax.broadcasted_iota(I32, (bn, bn), 0) // head_dim
        c = lax.broadcasted_iota(I32, (bn, bn), 1) // head_dim
        seg = (r == c).astype(BF16)
        ssq = jnp.dot((z * z).astype(BF16), seg, preferred_element_type=F32)
        y = z * lax.rsqrt(ssq * (1.0 / head_dim) + EPS) * gain_ref[...]
        reps = bn // cos_ref.shape[1]
        cosv = jnp.tile(cos_ref[...], (1, reps))
        sa = jnp.tile(sa_ref[...], (1, reps))
        sb = jnp.tile(sb_ref[...], (1, reps))
        y = y * cosv + pltpu.roll(y, bn - rot_half, 1) * sa + pltpu.roll(y, rot_half, 1) * sb
        o_ref[...] = y

    @pl.when(n >= n_norm_tiles)
    def _():
        o_ref[...] = z


def _inproj_qkv(h, w, col0, q_dim, kv_dim, gain_row, cos_t, sa_t, sb_t, head_dim, rot_half):
    m, d = h.shape
    ncols = q_dim + 2 * kv_dim
    bm, bn = min(ROW_TILE, m), kv_dim
    assert col0 % bn == 0 and q_dim % bn == 0 and bn % LANES == 0 and m % bm == 0
    off = col0 // bn
    n_norm_tiles = (q_dim + kv_dim) // bn
    tw = cos_t.shape[1]
    return pl.pallas_call(
        functools.partial(_inproj_qkv_kernel, n_norm_tiles=n_norm_tiles, head_dim=head_dim,
                          rot_half=rot_half),
        grid=(ncols // bn, m // bm),
        in_specs=[
            pl.BlockSpec((bm, d), lambda n, i: (i, 0)),
            pl.BlockSpec((d, bn), lambda n, i: (0, n + off)),
            pl.BlockSpec((1, bn), lambda n, i: (0, n)),
            pl.BlockSpec((bm, tw), lambda n, i: (i, 0)),
            pl.BlockSpec((bm, tw), lambda n, i: (i, 0)),
            pl.BlockSpec((bm, tw), lambda n, i: (i, 0)),
        ],
        out_specs=pl.BlockSpec((bm, bn), lambda n, i: (i, n)),
        out_shape=jax.ShapeDtypeStruct((m, ncols), F32),
        scratch_shapes=[pltpu.VMEM((d, bn), BF16)],
        compiler_params=_cparams(("arbitrary", "arbitrary")),
        name="inproj_qkv",
    )(h, w, gain_row, cos_t, sa_t, sb_t)


def _gmlp_rows(u_ref, va_ref, w_ref, b_ref, gain_ref, o_ref, vn_ref, r0, sub):
    groups, rows, _ = w_ref.shape
    gw = u_ref.shape[1] // groups
    rs = slice(r0, r0 + rows)
    ri = lax.broadcasted_iota(I32, (rows, rows), 0)
    ci = lax.broadcasted_iota(I32, (rows, rows), 1)
    mask = (ci <= ri) & ((ri // sub) == (ci // sub))
    vn = _rms(va_ref[rs, :].astype(F32), gain_ref[...])
    if vn_ref is not None:
        vn_ref[rs, :] = vn
    vb = vn.astype(BF16)
    for g in range(groups):
        sl = slice(g * gw, (g + 1) * gw)
        wg = jnp.where(mask, w_ref[g], 0.0).astype(BF16)
        s = jnp.dot(wg, vb[:, sl], preferred_element_type=F32) + b_ref[:, g:g + 1]
        o_ref[rs, sl] = (u_ref[rs, sl].astype(F32) * s).astype(o_ref.dtype)


def _gate_gmlp_kernel(h_ref, w_ref, u_ref, va_ref, wp_ref, ws_ref, bp_ref, bs_ref, gain_ref,
                      g_ref, oa_ref, vn_ref, wb_ref, *, steps_per_col, n_prompt_steps, n_side_steps,
                      sample_len):
    i = pl.program_id(1)
    s = pl.program_id(0) * steps_per_col + i
    chunk = wp_ref.shape[1]

    @pl.when(i == 0)
    def _():
        _cast_rows(w_ref, wb_ref, 256)

    def gate():
        z = jnp.dot(h_ref[...], wb_ref[...], preferred_element_type=F32)
        g_ref[...] = jax.nn.sigmoid(z).astype(g_ref.dtype)

    @pl.when(s < n_prompt_steps)
    def _():
        gate()
        for r0 in range(0, u_ref.shape[0], chunk):
            _gmlp_rows(u_ref, va_ref, wp_ref, bp_ref, gain_ref, oa_ref, None, r0, chunk)

    @pl.when((s >= n_prompt_steps) & (s < n_side_steps))
    def _():
        gate()
        for r0 in range(0, u_ref.shape[0], chunk):
            _gmlp_rows(u_ref, va_ref, ws_ref, bs_ref, gain_ref, oa_ref, vn_ref, r0, sample_len)

    @pl.when(s >= n_side_steps)
    def _():
        gate()


def _gate_gmlp(h, w, col0, ncols, ua, w_prompt, w_sample, b_prompt, b_sample, gain, n_prompt_rows,
               sample_len):
    m, d = h.shape
    d_a = ua.shape[1] // 2
    groups, chunk, _ = w_prompt.shape
    bn = min(COL_TILE, ncols)
    side = min(GMLP_ROWS, m - n_prompt_rows)
    assert col0 % bn == 0 and ncols % bn == 0
    assert n_prompt_rows % side == 0 and (m - n_prompt_rows) % side == 0 and side % chunk == 0
    nps, nss = n_prompt_rows // side, m // side
    bm = _host_row_tile(m, ncols // bn, nss, WIDE_ROW_TILE)
    off = col0 // bn
    spc = m // bm
    step = lambda n, i: n * spc + i
    side_blk = lambda n, i: jnp.minimum(step(n, i), nss - 1)
    full3 = lambda n, i: (0, 0, 0)
    full2 = lambda n, i: (0, 0)
    return pl.pallas_call(
        functools.partial(_gate_gmlp_kernel, steps_per_col=spc, n_prompt_steps=nps, n_side_steps=nss,
                          sample_len=sample_len),
        grid=(ncols // bn, spc),
        in_specs=[
            pl.BlockSpec((bm, d), lambda n, i: (i, 0)),
            pl.BlockSpec((d, bn), lambda n, i: (0, n + off)),
            pl.BlockSpec((side, d_a), lambda n, i: (side_blk(n, i), 0)),
            pl.BlockSpec((side, d_a), lambda n, i: (side_blk(n, i), 1)),
            pl.BlockSpec((groups, chunk, chunk), full3),
            pl.BlockSpec((groups, chunk, chunk), full3),
            pl.BlockSpec((chunk, groups), full2),
            pl.BlockSpec((chunk, groups), full2),
            pl.BlockSpec((1, d_a), full2),
        ],
        out_specs=[
            pl.BlockSpec((bm, bn), lambda n, i: (i, n)),
            pl.BlockSpec((side, d_a), lambda n, i: (side_blk(n, i), 0)),
            pl.BlockSpec((side, d_a), lambda n, i: (jnp.clip(step(n, i) - nps, 0, nss - nps - 1), 0)),
        ],
        out_shape=[
            jax.ShapeDtypeStruct((m, ncols), BF16),
            jax.ShapeDtypeStruct((m, d_a), BF16),
            jax.ShapeDtypeStruct((m - n_prompt_rows, d_a), F32),
        ],
        scratch_shapes=[pltpu.VMEM((d, bn), BF16)],
        compiler_params=_cparams(("arbitrary", "arbitrary"), FUSED_VMEM_LIMIT_BYTES),
        name="inproj_gate_gmlp",
    )(h, w, ua, ua, w_prompt, w_sample, b_prompt, b_sample, gain)


def _gelu_attn_kernel(h_ref, w_ref, bias_ref, q_ref, ka_ref, kb_ref, kc_ref, kd_ref, va_ref, vb_ref,
                      vc_ref, vd_ref, ck_ref, cv_ref, ua_ref, o_ref, wb_ref, s_ref, e_ref, *,
                      steps_per_col, n_prompt_steps, n_side_steps, tiles_per_kv, head_dim):
    i = pl.program_id(0) * steps_per_col + pl.program_id(1)

    @pl.when(pl.program_id(1) == 0)
    def _():
        _cast_rows(w_ref, wb_ref, 256)

    def project():
        z = jnp.dot(h_ref[...], wb_ref[...], preferred_element_type=F32)
        ua_ref[...] = jax.nn.gelu(z).astype(ua_ref.dtype)

    t = q_ref.shape[0] // 2
    kv_dim = kc_ref.shape[1]
    lkp = bias_ref.shape[1] // 2
    scale = head_dim ** -0.5
    nt = (((1,), (1,)), ((), ()))
    n_kv = kv_dim // head_dim
    tiles_of = lambda j: [j * tiles_per_kv + a for a in range(tiles_per_kv)]

    def run(halves):
        prepared = []
        for k_parts, v_parts, first_key_chunk in halves:
            k = jnp.concatenate(k_parts, axis=0)
            v = jnp.concatenate(v_parts, axis=0)
            chunk_bias = None
            if first_key_chunk is not None:
                col = lax.broadcasted_iota(I32, (1, 2 * lkp), 1)
                key_chunk = first_key_chunk + (col % lkp) // t
                chunk_bias = jnp.where(key_chunk >= 0, 0.0, -jnp.inf)
            prepared.append((k, v, chunk_bias))
        lk = prepared[0][0].shape[0]
        lower = lax.broadcasted_iota(I32, (lk, LANES), 1) < head_dim
        zpad = jnp.zeros((lkp - lk, LANES), F32)
        rr = lax.broadcasted_iota(I32, (2 * lkp, LANES), 0) < lkp
        rl = lax.broadcasted_iota(I32, (2 * lkp, LANES), 1) < head_dim
        ones_sel = (rr == rl).astype(BF16)

        def doubled(tile, rolled, x):
            lo_src, hi_src = (tile, rolled) if x == 0 else (rolled, tile)
            return jnp.concatenate([jnp.where(lower, lo_src, 0.0), zpad,
                                    jnp.where(lower, 0.0, hi_src), zpad], axis=0).astype(BF16)

        for h, (k, _, chunk_bias) in enumerate(prepared):
            rows_h = slice(h * t, (h + 1) * t)
            for b in range(kv_dim // LANES):
                kt = k[:, b * LANES:(b + 1) * LANES]
                kr = pltpu.roll(kt, head_dim, 1)
                for x in range(2):
                    j = 2 * b + x
                    q2 = jnp.concatenate([q_ref[rows_h, a * LANES:(a + 1) * LANES] for a in tiles_of(j)],
                                         axis=0)
                    q2 = (q2 * scale).astype(BF16)
                    rows = []
                    for a in tiles_of(j):
                        row = bias_ref[a:a + 1, :]
                        if chunk_bias is not None:
                            row = row + chunk_bias
                        rows.append(jnp.broadcast_to(row, (t, 2 * lkp)))
                    s_ref[h, j] = (lax.dot_general(q2, doubled(kt, kr, x), nt, preferred_element_type=F32)
                                   + jnp.concatenate(rows, axis=0))

        for h in range(len(prepared)):
            for j in range(n_kv):
                for hs in (slice(0, lkp), slice(lkp, 2 * lkp)):
                    sh = s_ref[h, j, :, hs]
                    e_ref[h, j, :, hs] = jnp.exp(sh - jnp.max(sh, axis=-1, keepdims=True)).astype(BF16)

        for h, (_, v, _) in enumerate(prepared):
            for b in range(kv_dim // LANES):
                vt = v[:, b * LANES:(b + 1) * LANES]
                vr = pltpu.roll(vt, head_dim, 1)
                for x in range(2):
                    j = 2 * b + x
                    rhs = jnp.concatenate([doubled(vt, vr, x), ones_sel], axis=1)
                    r = jnp.dot(e_ref[h, j], rhs, preferred_element_type=F32)
                    o = r[:, :LANES] / r[:, LANES:]
                    for n, a in enumerate(tiles_of(j)):
                        o_ref[h * t:(h + 1) * t, a * LANES:(a + 1) * LANES] = (
                            o[n * t:(n + 1) * t].astype(o_ref.dtype))

    @pl.when(i < n_prompt_steps)
    def _():
        project()
        ka, kb, kc, kd = ka_ref[...], kb_ref[...], kc_ref[...], kd_ref[...]
        va, vb, vc, vd = va_ref[...], vb_ref[...], vc_ref[...], vd_ref[...]
        run([([ka, kb, kc], [va, vb, vc], 2 * i - 2), ([kb, kc, kd], [vb, vc, vd], 2 * i - 1)])

    @pl.when((i >= n_prompt_steps) & (i < n_side_steps))
    def _():
        project()
        w = ck_ref.shape[0] // 2
        run([([ck_ref[:w, :], kc_ref[...]], [cv_ref[:w, :], vc_ref[...]], None),
             ([ck_ref[w:, :], kd_ref[...]], [cv_ref[w:, :], vd_ref[...]], None)])

    @pl.when(i >= n_side_steps)
    def _():
        project()


def _gelu_attention(h, w, col0, ncols, qkv, cache_k, cache_v, sinks, n_prompt_rows, q_dim, kv_dim,
                    head_dim, n_heads):
    m, d = h.shape
    t = CHUNK
    nc = n_prompt_rows // t
    n_kv_heads = kv_dim // head_dim
    window = cache_k.shape[0] // ((m - n_prompt_rows) // t)
    kcol, vcol = q_dim // kv_dim, q_dim // kv_dim + 1
    gqa = n_heads // n_kv_heads
    assert 2 * head_dim == LANES and gqa % 2 == 0 and kv_dim % LANES == 0

    lk = window + t
    lkp = -(-(lk + 1) // LANES) * LANES
    pair = sinks.reshape(n_heads // 2, 2, 1)
    bias = jnp.concatenate([jnp.zeros((n_heads // 2, 2, lk), F32), pair,
                            jnp.full((n_heads // 2, 2, lkp - lk - 1), -jnp.inf, F32)], axis=2)
    bias = bias.reshape(n_heads // 2, 2 * lkp)

    assert nc % 2 == 0 and (m // t - nc) % 2 == 0, "query chunks are processed in pairs"
    nps, nss = nc // 2, m // (2 * t)

    bn = min(COL_TILE, ncols)
    assert col0 % bn == 0 and ncols % bn == 0
    bm = _host_row_tile(m, ncols // bn, nss, ROW_TILE)
    spc = m // bm
    off = col0 // bn
    pair_of = lambda n, i: jnp.minimum(n * spc + i, nss - 1)

    def kv_spec(back, col):
        return pl.BlockSpec((t, kv_dim), lambda n, i: (jnp.maximum(2 * pair_of(n, i) + back, 0), col))

    cache_spec = pl.BlockSpec((2 * window, kv_dim), lambda n, i: (jnp.maximum(pair_of(n, i) - nps, 0), 0))
    scratch = (2, n_kv_heads, gqa // 2 * t, 2 * lkp)
    return pl.pallas_call(
        functools.partial(_gelu_attn_kernel, steps_per_col=spc, n_prompt_steps=nps, n_side_steps=nss,
                          tiles_per_kv=gqa // 2, head_dim=head_dim),
        grid=(ncols // bn, spc),
        in_specs=[
            pl.BlockSpec((bm, d), lambda n, i: (i, 0)),
            pl.BlockSpec((d, bn), lambda n, i: (0, n + off)),
            pl.BlockSpec((n_heads // 2, 2 * lkp), lambda n, i: (0, 0)),
            pl.BlockSpec((2 * t, q_dim), lambda n, i: (pair_of(n, i), 0)),
            kv_spec(-2, kcol), kv_spec(-1, kcol), kv_spec(0, kcol), kv_spec(1, kcol),
            kv_spec(-2, vcol), kv_spec(-1, vcol), kv_spec(0, vcol), kv_spec(1, vcol),
            cache_spec, cache_spec,
        ],
        out_specs=[
            pl.BlockSpec((bm, bn), lambda n, i: (i, n)),
            pl.BlockSpec((2 * t, q_dim), lambda n, i: (pair_of(n, i), 0)),
        ],
        out_shape=[
            jax.ShapeDtypeStruct((m, ncols), BF16),
            jax.ShapeDtypeStruct((m, q_dim), BF16),
        ],
        scratch_shapes=[pltpu.VMEM((d, bn), BF16), pltpu.VMEM(scratch, F32), pltpu.VMEM(scratch, BF16)],
        compiler_params=_cparams(("arbitrary", "arbitrary")),
        name="inproj_gelu_attention",
    )(h, w, bias, qkv, qkv, qkv, qkv, qkv, qkv, qkv, qkv, qkv, cache_k, cache_v)


def _merge_kernel(oa_ref, ob_ref, pa_ref, pb_ref, ga_ref, gb_ref, o_ref, pab_ref, pbb_ref):
    @pl.when(pl.program_id(1) == 0)
    def _():
        _cast_rows(pa_ref, pab_ref, 256)
        _cast_rows(pb_ref, pbb_ref, 256)

    a = jnp.dot(oa_ref[...], pab_ref[...], preferred_element_type=F32)
    b = jnp.dot(ob_ref[...], pbb_ref[...], preferred_element_type=F32)
    o_ref[...] = (ga_ref[...].astype(F32) * a + gb_ref[...].astype(F32) * b).astype(o_ref.dtype)


def _merge(o_a, o_b, p_a, p_b, gates):
    m, d_a = o_a.shape
    q_dim = o_b.shape[1]
    d = p_a.shape[1]
    bm, bn = min(ROW_TILE, m), min(COL_TILE, d)
    goff = d // bn
    return pl.pallas_call(
        _merge_kernel,
        grid=(d // bn, m // bm),
        in_specs=[
            pl.BlockSpec((bm, d_a), lambda n, i: (i, 0)),
            pl.BlockSpec((bm, q_dim), lambda n, i: (i, 0)),
            pl.BlockSpec((d_a, bn), lambda n, i: (0, n)),
            pl.BlockSpec((q_dim, bn), lambda n, i: (0, n)),
            pl.BlockSpec((bm, bn), lambda n, i: (i, n)),
            pl.BlockSpec((bm, bn), lambda n, i: (i, n + goff)),
        ],
        out_specs=pl.BlockSpec((bm, bn), lambda n, i: (i, n)),
        out_shape=jax.ShapeDtypeStruct((m, d), BF16),
        scratch_shapes=[pltpu.VMEM((d_a, bn), BF16), pltpu.VMEM((q_dim, bn), BF16)],
        compiler_params=_cparams(("arbitrary", "arbitrary")),
        name="merge_proj",
    )(o_a, o_b, p_a, p_b, gates, gates)


def _outproj_kernel(t_ref, w_ref, xp_ref, xs_ref, o_ref, wb_ref, *, n_prompt_tiles):
    i = pl.program_id(1)

    @pl.when(i == 0)
    def _():
        _cast_rows(w_ref, wb_ref, 256)

    z = jnp.dot(t_ref[...], wb_ref[...], preferred_element_type=F32)

    @pl.when(i < n_prompt_tiles)
    def _():
        o_ref[...] = xp_ref[...] + z

    @pl.when(i >= n_prompt_tiles)
    def _():
        o_ref[...] = xs_ref[...] + z


def _outproj(tm, w_out, xp, xs):
    m, d = tm.shape
    s, t = xp.shape[0], xs.shape[0]
    bm, bn = min(ROW_TILE, t), min(COL_TILE, d)
    npt = s // bm
    return pl.pallas_call(
        functools.partial(_outproj_kernel, n_prompt_tiles=npt),
        grid=(d // bn, m // bm),
        in_specs=[
            pl.BlockSpec((bm, d), lambda n, i: (i, 0)),
            pl.BlockSpec((d, bn), lambda n, i: (0, n)),
            pl.BlockSpec((bm, bn), lambda n, i: (jnp.minimum(i, npt - 1), n)),
            pl.BlockSpec((bm, bn), lambda n, i: (jnp.maximum(i - npt, 0), n)),
        ],
        out_specs=pl.BlockSpec((bm, bn), lambda n, i: (i, n)),
        out_shape=jax.ShapeDtypeStruct((m, d), F32),
        scratch_shapes=[pltpu.VMEM((d, bn), BF16)],
        compiler_params=_cparams(("arbitrary", "arbitrary")),
        name="out_proj",
    )(tm, w_out, xp, xs)


def _router_kernel(x_ref, g_ref, wr_ref, br_ref, route_ref, cnt_ref, wcat_ref, carry_ref, *,
                   n_groups, per_group):
    i = pl.program_id(0)

    @pl.when(i == 0)
    def _():
        w = wr_ref[...]
        hi = w.astype(BF16)
        wcat_ref[:, :LANES] = hi
        wcat_ref[:, LANES:] = (w - hi.astype(F32)).astype(BF16)
        carry_ref[...] = jnp.zeros_like(carry_ref)

    hn = _rms(x_ref[...], g_ref[...])
    rows = hn.shape[0]
    hi = hn.astype(BF16)
    lo = (hn - hi.astype(F32)).astype(BF16)
    prod = jnp.dot(jnp.concatenate([hi, lo], axis=0), wcat_ref[...], preferred_element_type=F32)
    logits = prod[:rows, :LANES] + prod[:rows, LANES:] + prod[rows:, :LANES] + br_ref[...]
    lane = lax.broadcasted_iota(I32, logits.shape, 1)
    big = jnp.int32(LANES)

    lg = jnp.where(lane < n_groups, logits, -jnp.inf)
    mg = jnp.max(lg, axis=-1, keepdims=True)
    pg_sel = 1.0 / jnp.sum(jnp.exp(lg - mg), axis=-1, keepdims=True)
    gsel = jnp.min(jnp.where(lg == mg, lane, big), axis=-1, keepdims=True)

    first = n_groups + gsel * per_group
    emask = (lane >= first) & (lane < first + per_group)
    le = jnp.where(emask, logits, -jnp.inf)
    me = jnp.max(le, axis=-1, keepdims=True)
    ee = jnp.exp(le - me)
    pe = jnp.where(emask, ee / jnp.sum(ee, axis=-1, keepdims=True), -1.0)
    p1 = jnp.max(pe, axis=-1, keepdims=True)
    i1 = jnp.min(jnp.where(pe == p1, lane, big), axis=-1, keepdims=True)
    pe2 = jnp.where(lane == i1, -1.0, pe)
    p2 = jnp.max(pe2, axis=-1, keepdims=True)
    i2 = jnp.min(jnp.where(pe2 == p2, lane, big), axis=-1, keepdims=True)
    psum = p1 + p2
    w1 = p1 / psum * pg_sel
    w2 = p2 / psum * pg_sel
    e1 = i1 - n_groups
    e2 = i2 - n_groups

    oh1 = (lane == e1).astype(F32)
    oh2 = (lane == e2).astype(F32)
    ohs = oh1 + oh2
    ri = lax.broadcasted_iota(I32, (rows, rows), 0)
    ci = lax.broadcasted_iota(I32, (rows, rows), 1)
    below = (ci < ri).astype(BF16)
    before = jnp.dot(below, ohs.astype(BF16), preferred_element_type=F32) + carry_ref[...]
    r1 = jnp.sum(before * oh1, axis=-1, keepdims=True)
    r2 = jnp.sum(before * oh2, axis=-1, keepdims=True)
    carry_ref[...] = carry_ref[...] + jnp.sum(ohs, axis=0, keepdims=True)
    cnt_ref[...] = carry_ref[...]

    route = jnp.where(lane == 0, e1.astype(F32), 0.0)
    route = jnp.where(lane == 1, e2.astype(F32), route)
    route = jnp.where(lane == 2, w1, route)
    route = jnp.where(lane == 3, w2, route)
    route = jnp.where(lane == 4, r1, route)
    route = jnp.where(lane == 5, r2, route)
    route_ref[...] = route


def _router(x2, gain, wr, br, n_groups, per_group):
    m, d = x2.shape
    br_rows = min(ROUTE_ROWS, m)
    return pl.pallas_call(
        functools.partial(_router_kernel, n_groups=n_groups, per_group=per_group),
        grid=(m // br_rows,),
        in_specs=[
            pl.BlockSpec((br_rows, d), lambda i: (i, 0)),
            pl.BlockSpec((1, d), lambda i: (0, 0)),
            pl.BlockSpec((d, LANES), lambda i: (0, 0)),
            pl.BlockSpec((1, LANES), lambda i: (0, 0)),
        ],
        out_specs=[
            pl.BlockSpec((br_rows, LANES), lambda i: (i, 0)),
            pl.BlockSpec((1, LANES), lambda i: (0, 0)),
        ],
        out_shape=[
            jax.ShapeDtypeStruct((m, LANES), F32),
            jax.ShapeDtypeStruct((1, LANES), F32),
        ],
        scratch_shapes=[pltpu.VMEM((d, 2 * LANES), BF16), pltpu.VMEM((1, LANES), F32)],
        compiler_params=_cparams(("arbitrary",)),
        name="norm2_router",
    )(x2, gain, wr, br)


def _pack_pair(first, second):
    hi = lax.bitcast_convert_type(first.astype(BF16).astype(F32), U32)
    lo = lax.bitcast_convert_type(second.astype(BF16).astype(F32), U32)
    return hi | (lo >> 16)


def _pack_halves(x):
    half = x.shape[1] // 2
    return _pack_pair(x[:, :half], x[:, half:])


def _unpack_halves(p):
    first = lax.bitcast_convert_type(p & jnp.uint32(0xFFFF0000), F32)
    second = lax.bitcast_convert_type(p << 16, F32)
    return first, second


def _dispatch_kernel(dest_ref, x_ref, g_ref, xs_ref, buf_ref, sem, *, n_tiles):
    i = pl.program_id(0)
    tm = x_ref.shape[0]
    n_tokens = n_tiles * tm
    slot = i % 2

    def wait_rows(s):
        for k in range(TOP_K):
            pltpu.make_async_copy(buf_ref.at[s], xs_ref.at[pl.ds(0, tm)], sem.at[s]).wait()

    buf_ref[slot] = _pack_halves(_rms(x_ref[...], g_ref[...]))

    def start(r, c):
        for k in range(TOP_K):
            d = dest_ref[k * n_tokens + i * tm + r]
            pltpu.make_async_copy(buf_ref.at[slot, pl.ds(r, 1)], xs_ref.at[pl.ds(d, 1)],
                                  sem.at[slot]).start()
        return c

    lax.fori_loop(0, tm, start, 0, unroll=DMA_UNROLL)

    @pl.when(i > 0)
    def _():
        wait_rows(1 - slot)

    @pl.when(i == n_tiles - 1)
    def _():
        wait_rows(slot)


def _dispatch(dest_flat, x2, gain, n_rows):
    m, d = x2.shape
    tm = min(DISPATCH_ROWS, m)
    return pl.pallas_call(
        functools.partial(_dispatch_kernel, n_tiles=m // tm),
        grid_spec=pltpu.PrefetchScalarGridSpec(
            num_scalar_prefetch=1,
            grid=(m // tm,),
            in_specs=[
                pl.BlockSpec((tm, d), lambda i, dest: (i, 0)),
                pl.BlockSpec((1, d), lambda i, dest: (0, 0)),
            ],
            out_specs=pl.BlockSpec(memory_space=pl.ANY),
            scratch_shapes=[pltpu.VMEM((2, tm, d // 2), U32), pltpu.SemaphoreType.DMA((2,))],
        ),
        out_shape=jax.ShapeDtypeStruct((n_rows, d // 2), U32),
        compiler_params=_cparams(("arbitrary",)),
        name="moe_dispatch",
    )(dest_flat, x2, gain)


def _moe_kernel(be_ref, bidx_ref, nrows_ref, nv_ref, xs_ref, wg_ref, wu_ref, wd_ref, y_ref,
                x1_ref, x2_ref, act_ref, *, n_split):
    i = pl.program_id(0)
    j = pl.program_id(1)

    @pl.when(i < nv_ref[0])
    def _():
        rows, half = xs_ref.shape
        dh = wg_ref.shape[2]

        @pl.when(j == 0)
        def _():
            live = lax.broadcasted_iota(I32, (rows, 1), 0) < nrows_ref[i]
            x1, x2 = _unpack_halves(xs_ref[...])
            x1_ref[...] = jnp.where(live, x1, 0.0).astype(BF16)
            x2_ref[...] = jnp.where(live, x2, 0.0).astype(BF16)

        def proj(w_ref):
            return (jnp.dot(x1_ref[...], w_ref[0, :half, :].astype(BF16), preferred_element_type=F32)
                    + jnp.dot(x2_ref[...], w_ref[0, half:, :].astype(BF16), preferred_element_type=F32))

        act = (jax.nn.silu(proj(wg_ref)) * proj(wu_ref)).astype(BF16)
        for q in range(n_split):
            @pl.when(j == q)
            def _(q=q):
                act_ref[:, q * dh:(q + 1) * dh] = act

        @pl.when(j == n_split - 1)
        def _():
            a = act_ref[...]
            first = jnp.dot(a, wd_ref[0, :, :half].astype(BF16), preferred_element_type=F32)
            second = jnp.dot(a, wd_ref[0, :, half:].astype(BF16), preferred_element_type=F32)
            y_ref[...] = _pack_pair(first, second)


def _moe_experts(block_e, block_idx, block_rows, n_valid, xs, w_gate, w_up, w_down, n_blocks):
    half = xs.shape[1]
    d = 2 * half
    d_e = w_gate.shape[2]
    n_split = 2
    dh = d_e // n_split
    br = MOE_ROWS

    def col(i, j, nv):
        return jnp.where(i < nv[0], j, n_split - 1)

    return pl.pallas_call(
        functools.partial(_moe_kernel, n_split=n_split),
        grid_spec=pltpu.PrefetchScalarGridSpec(
            num_scalar_prefetch=4,
            grid=(n_blocks, n_split),
            in_specs=[
                pl.BlockSpec((br, half), lambda i, j, be, bi, nr, nv: (bi[i], 0)),
                pl.BlockSpec((1, d, dh), lambda i, j, be, bi, nr, nv: (be[i], 0, col(i, j, nv))),
                pl.BlockSpec((1, d, dh), lambda i, j, be, bi, nr, nv: (be[i], 0, col(i, j, nv))),
                pl.BlockSpec((1, d_e, d), lambda i, j, be, bi, nr, nv: (be[i], 0, 0)),
            ],
            out_specs=pl.BlockSpec((br, half), lambda i, j, be, bi, nr, nv: (bi[i], 0)),
            scratch_shapes=[pltpu.VMEM((br, half), BF16), pltpu.VMEM((br, half), BF16),
                            pltpu.VMEM((br, d_e), BF16)],
        ),
        out_shape=jax.ShapeDtypeStruct(xs.shape, U32),
        compiler_params=_cparams(("arbitrary", "arbitrary")),
        name="moe_experts",
    )(block_e, block_idx, block_rows, n_valid, xs, w_gate, w_up, w_down)


def _combine_kernel(dest_ref, x_ref, route_ref, y_ref, op_ref, os_ref, buf_ref, sem, *,
                    n_prompt_tiles, n_tiles):
    i = pl.program_id(0)
    tm = x_ref.shape[0]
    n_tokens = n_tiles * tm
    slot = i % 2

    def gather(tile, to_slot):
        def start(r, c):
            for k in range(TOP_K):
                d = dest_ref[k * n_tokens + tile * tm + r]
                pltpu.make_async_copy(y_ref.at[pl.ds(d, 1)], buf_ref.at[to_slot, k, pl.ds(r, 1)],
                                      sem.at[to_slot]).start()
            return c

        lax.fori_loop(0, tm, start, 0, unroll=DMA_UNROLL)

    @pl.when(i == 0)
    def _():
        gather(0, 0)

    @pl.when(i + 1 < n_tiles)
    def _():
        gather(i + 1, 1 - slot)

    for k in range(TOP_K):
        pltpu.make_async_copy(y_ref.at[pl.ds(0, tm)], buf_ref.at[slot, k], sem.at[slot]).wait()

    half = x_ref.shape[1] // 2
    first, second = x_ref[:, :half], x_ref[:, half:]
    for k in range(TOP_K):
        y1, y2 = _unpack_halves(buf_ref[slot, k])
        w = route_ref[:, 2 + k:3 + k]
        first = first + w * y1
        second = second + w * y2
    out = jnp.concatenate([first, second], axis=1)

    @pl.when(i < n_prompt_tiles)
    def _():
        op_ref[...] = out

    @pl.when(i >= n_prompt_tiles)
    def _():
        os_ref[...] = out


def _combine(dest_flat, x2, route, y, n_prompt_rows):
    m, d = x2.shape
    tm = COMBINE_ROWS
    npt = n_prompt_rows // tm
    return pl.pallas_call(
        functools.partial(_combine_kernel, n_prompt_tiles=npt, n_tiles=m // tm),
        grid_spec=pltpu.PrefetchScalarGridSpec(
            num_scalar_prefetch=1,
            grid=(m // tm,),
            in_specs=[
                pl.BlockSpec((tm, d), lambda i, dest: (i, 0)),
                pl.BlockSpec((tm, LANES), lambda i, dest: (i, 0)),
                pl.BlockSpec(memory_space=pl.ANY),
            ],
            out_specs=[
                pl.BlockSpec((tm, d), lambda i, dest: (jnp.minimum(i, npt - 1), 0)),
                pl.BlockSpec((tm, d), lambda i, dest: (jnp.maximum(i - npt, 0), 0)),
            ],
            scratch_shapes=[pltpu.VMEM((2, TOP_K, tm, d // 2), U32), pltpu.SemaphoreType.DMA((2,))],
        ),
        out_shape=[
            jax.ShapeDtypeStruct((n_prompt_rows, d), F32),
            jax.ShapeDtypeStruct((m - n_prompt_rows, d), F32),
        ],
        compiler_params=_cparams(("arbitrary",)),
        name="moe_combine",
    )(dest_flat, x2, route, y)


def _rope_tables(positions, head_dim):
    rot_dim = head_dim // 4
    half = rot_dim // 2
    inv_freq = jnp.power(ROPE_THETA, -jnp.arange(half, dtype=F32) * 2.0 / rot_dim)
    ang = positions.astype(F32)[:, None] * inv_freq[None, :]
    cos, sin = lax.optimization_barrier((jnp.cos(ang), jnp.sin(ang)))
    m = positions.shape[0]
    zeros = lambda n: jnp.zeros((m, n), F32)
    cos_h = jnp.concatenate([cos, cos, jnp.ones((m, head_dim - rot_dim), F32)], axis=1)
    sa_h = jnp.concatenate([-sin, zeros(head_dim - half)], axis=1)
    sb_h = jnp.concatenate([zeros(half), sin, zeros(head_dim - rot_dim)], axis=1)
    reps = LANES // head_dim
    return tuple(jnp.tile(a, (1, reps)) for a in (cos_h, sa_h, sb_h)), half


def _layer(xp, xs, cache_k, cache_v, norm1_g, w_in, gmlp_norm_g, w_s, b_s, q_norm_g, k_norm_g,
           sinks, p_a, p_b, w_out, norm2_g, w_rg, b_rg, w_re, b_re, w_gate_e, w_up_e, w_down_e):
    s, d = xp.shape
    dec_rows = xs.shape[0]
    dec_batch = cache_k.shape[0]
    t = dec_rows // dec_batch
    m = s + dec_rows
    d_a = gmlp_norm_g.shape[0]
    head_dim = q_norm_g.shape[0]
    n_heads = sinks.shape[0]
    q_dim = n_heads * head_dim
    kv_dim = cache_k.shape[2] * cache_k.shape[3]
    n_groups, per_group = w_re.shape[1], w_re.shape[2]
    n_experts = n_groups * per_group

    h = _norm1(xp, xs, norm1_g[None, :])
    positions = jnp.concatenate(
        [jnp.arange(s, dtype=I32), jnp.tile(PAST_LEN + jnp.arange(t, dtype=I32), dec_batch)])
    (cos_t, sa_t, sb_t), rot_half = _rope_tables(positions, head_dim)
    gain_row = jnp.concatenate([jnp.tile(q_norm_g, n_heads), jnp.tile(k_norm_g, kv_dim // head_dim),
                                jnp.ones((kv_dim,), F32)])[None, :]
    qkv = _inproj_qkv(h, w_in, 2 * d_a, q_dim, kv_dim, gain_row, cos_t, sa_t, sb_t, head_dim, rot_half)

    ua, o_b = _gelu_attention(h, w_in, 0, 2 * d_a, qkv, cache_k.reshape(-1, kv_dim),
                              cache_v.reshape(-1, kv_dim), sinks[None, :], s, q_dim, kv_dim, head_dim,
                              n_heads)
    rows = w_s.shape[1]
    reps = rows // t
    w_sample = jnp.tile(w_s[:, :t, :t], (1, reps, reps))
    b_sample = jnp.tile(b_s[:, :t], (1, reps))
    gates, o_a, vn_s = _gate_gmlp(h, w_in, 2 * d_a + q_dim + 2 * kv_dim, 2 * d, ua, w_s, w_sample,
                                  b_s.T, b_sample.T, gmlp_norm_g[None, :], s, t)

    merged = _merge(o_a, o_b, p_a, p_b, gates)
    x2 = _outproj(merged, w_out, xp, xs)

    pad = LANES - n_groups - n_experts
    wr = jnp.concatenate([w_rg, w_re.reshape(d, n_experts), jnp.zeros((d, pad), F32)], axis=1)
    br = jnp.concatenate([b_rg, b_re.reshape(n_experts), jnp.zeros((pad,), F32)])[None, :]
    route, counts = _router(x2, norm2_g[None, :], wr, br, n_groups, per_group)
    e_idx = route[:, 0:TOP_K].T.astype(I32)
    rank = route[:, 4:4 + TOP_K].T.astype(I32)
    counts = counts[0, :n_experts].astype(I32)

    blk = MOE_ROWS
    n_blocks = -(-(m * TOP_K) // blk) + n_experts
    padded = ((counts + blk - 1) // blk) * blk
    pend = jnp.cumsum(padded)
    pstart = pend - padded
    hit = e_idx[:, :, None] == jnp.arange(n_experts, dtype=I32)
    dest = (jnp.sum(jnp.where(hit, pstart, 0), axis=-1) + rank).reshape(-1).astype(I32)
    n_valid = (pend[-1] // blk).astype(I32)
    ids = jnp.arange(n_blocks, dtype=I32)
    block_idx = jnp.minimum(ids, n_valid - 1)
    first_row = block_idx * blk
    block_e = jnp.minimum(jnp.sum((pend[None, :] <= first_row[:, None]).astype(I32), axis=1), n_experts - 1)
    block_rows = jnp.clip(counts[block_e] - (first_row - pstart[block_e]), 0, blk).astype(I32)

    xs_sorted = _dispatch(dest, x2, norm2_g[None, :], n_blocks * blk)
    y_sorted = _moe_experts(block_e, block_idx, block_rows, n_valid[None], xs_sorted,
                            w_gate_e, w_up_e, w_down_e, n_blocks)
    yp, ys = _combine(dest, x2, route, y_sorted, s)
    return yp, ys, qkv, vn_s


def kernel(x_prompt, x_sample, cache_k_win, cache_v_win, norm1_g, w_in, gmlp_norm_g, w_s, b_s,
           q_norm_g, k_norm_g, sinks, p_a, p_b, w_out, norm2_g, w_rg, b_rg, w_re, b_re,
           w_gate_e, w_up_e, w_down_e):
    depth = norm1_g.shape[0]
    assert depth == 1, "weights of one layer are expected"
    batch, s, d = x_prompt.shape
    assert batch == 1
    dec_batch, t, _ = x_sample.shape
    head_dim = q_norm_g.shape[-1]
    n_kv = cache_k_win.shape[3]
    q_dim = sinks.shape[-1] * head_dim
    kv_dim = n_kv * head_dim
    keep = min(cache_k_win.shape[2], s)

    l = 0
    yp, ys, qkv, vn_s = _layer(
        x_prompt.reshape(s, d), x_sample.reshape(dec_batch * t, d), cache_k_win[l], cache_v_win[l],
        norm1_g[l], w_in[l], gmlp_norm_g[l], w_s[l], b_s[l], q_norm_g[l], k_norm_g[l], sinks[l],
        p_a[l], p_b[l], w_out[l], norm2_g[l], w_rg[l], b_rg[l], w_re[l], b_re[l],
        w_gate_e[l], w_up_e[l], w_down_e[l])

    k_all = qkv[:, q_dim:q_dim + kv_dim]
    v_all = qkv[:, q_dim + kv_dim:]
    k_win_p = k_all[s - keep:s].reshape(1, batch, keep, n_kv, head_dim)
    v_win_p = v_all[s - keep:s].reshape(1, batch, keep, n_kv, head_dim)
    k_new_s = k_all[s:].reshape(1, dec_batch, t, n_kv, head_dim)
    v_new_s = v_all[s:].reshape(1, dec_batch, t, n_kv, head_dim)
    gv_s = vn_s.reshape(1, dec_batch, t, -1)
    return (yp.reshape(batch, s, d), ys.reshape(dec_batch, t, d), k_win_p, v_win_p, k_new_s, v_new_s, gv_s)
```

```python
import functools

import jax
import jax.numpy as jnp
from jax import lax
from jax.experimental import pallas as pl
from jax.experimental.pallas import tpu as pltpu

F32 = jnp.float32
BF16 = jnp.bfloat16
I32 = jnp.int32
U32 = jnp.uint32

EPS = 1e-6
PAST_LEN = 1024
CHUNK = 64
ROPE_THETA = 500000.0
TOP_K = 2
LANES = 128

VMEM_LIMIT_BYTES = 56 * 1024 * 1024
FUSED_VMEM_LIMIT_BYTES = 58 * 1024 * 1024

ROW_TILE = 1024
WIDE_ROW_TILE = 1536
COL_TILE = 512
GMLP_ROWS = 128
NORM_ROWS = 512
ROUTE_ROWS = 512
MOE_ROWS = 320
COMBINE_ROWS = 256
DISPATCH_ROWS = 512
DMA_UNROLL = 8


def _cparams(sem, vmem_limit_bytes=VMEM_LIMIT_BYTES):
    return pltpu.CompilerParams(dimension_semantics=sem, vmem_limit_bytes=vmem_limit_bytes)


def _cast_rows(src_ref, dst_ref, rows):
    n = src_ref.shape[0] // rows

    def body(r, c):
        sl = pl.ds(pl.multiple_of(r * rows, rows), rows)
        dst_ref[sl, :] = src_ref[sl, :].astype(dst_ref.dtype)
        return c

    lax.fori_loop(0, n, body, 0)


def _row_tile(preferred, *row_counts):
    tile = preferred
    while any(n % tile for n in row_counts):
        tile //= 2
    assert tile >= 8
    return tile


def _rms(x, gain):
    ms = jnp.mean(x * x, axis=-1, keepdims=True)
    return x * lax.rsqrt(ms + EPS) * gain


def _norm1_kernel(xp_ref, xs_ref, g_ref, h_ref, *, n_prompt_blocks):
    i = pl.program_id(0)

    @pl.when(i < n_prompt_blocks)
    def _():
        h_ref[...] = _rms(xp_ref[...], g_ref[...]).astype(h_ref.dtype)

    @pl.when(i >= n_prompt_blocks)
    def _():
        h_ref[...] = _rms(xs_ref[...], g_ref[...]).astype(h_ref.dtype)


def _norm1(xp, xs, gain):
    s, d = xp.shape
    t = xs.shape[0]
    br = _row_tile(NORM_ROWS, s, t)
    nbp, nbs = s // br, t // br
    return pl.pallas_call(
        functools.partial(_norm1_kernel, n_prompt_blocks=nbp),
        grid=(nbp + nbs,),
        in_specs=[
            pl.BlockSpec((br, d), lambda i: (jnp.minimum(i, nbp - 1), 0)),
            pl.BlockSpec((br, d), lambda i: (jnp.maximum(i - nbp, 0), 0)),
            pl.BlockSpec((1, d), lambda i: (0, 0)),
        ],
        out_specs=pl.BlockSpec((br, d), lambda i: (i, 0)),
        out_shape=jax.ShapeDtypeStruct((s + t, d), BF16),
        compiler_params=_cparams(("arbitrary",)),
        name="norm1",
    )(xp, xs, gain)


def _host_row_tile(m, n_col_tiles, n_side_steps, preferred):
    bm = preferred if m % preferred == 0 else min(ROW_TILE, m)
    while n_col_tiles * (m // bm) < n_side_steps:
        assert bm % 16 == 0
        bm //= 2
    assert m % bm == 0
    return bm


def _inproj_qkv_kernel(h_ref, w_ref, gain_ref, cos_ref, sa_ref, sb_ref, o_ref, wb_ref, *,
                       n_norm_tiles, head_dim, rot_half):
    n = pl.program_id(0)

    @pl.when(pl.program_id(1) == 0)
    def _():
        _cast_rows(w_ref, wb_ref, 256)

    z = jnp.dot(h_ref[...], wb_ref[...], preferred_element_type=F32)
    bn = z.shape[1]

    @pl.when(n < n_norm_tiles)
    def _():
        r = lax.broadcasted_iota(I32, (bn, bn), 0) // head_dim
        c = lax.broadcasted_iota(I32, (bn, bn), 1) // head_dim
        seg = (r == c).astype(BF16)
        ssq = jnp.dot((z * z).astype(BF16), seg, preferred_element_type=F32)
        y = z * lax.rsqrt(ssq * (1.0 / head_dim) + EPS) * gain_ref[...]
        reps = bn // cos_ref.shape[1]
        cosv = jnp.tile(cos_ref[...], (1, reps))
        sa = jnp.tile(sa_ref[...], (1, reps))
        sb = jnp.tile(sb_ref[...], (1, reps))
        y = y * cosv + pltpu.roll(y, bn - rot_half, 1) * sa + pltpu.roll(y, rot_half, 1) * sb
        o_ref[...] = y

    @pl.when(n >= n_norm_tiles)
    def _():
        o_ref[...] = z


def _inproj_qkv(h, w, col0, q_dim, kv_dim, gain_row, cos_t, sa_t, sb_t, head_dim, rot_half):
    m, d = h.shape
    ncols = q_dim + 2 * kv_dim
    bm, bn = min(ROW_TILE, m), kv_dim
    assert col0 % bn == 0 and q_dim % bn == 0 and bn % LANES == 0 and m % bm == 0
    off = col0 // bn
    n_norm_tiles = (q_dim + kv_dim) // bn
    tw = cos_t.shape[1]
    return pl.pallas_call(
        functools.partial(_inproj_qkv_kernel, n_norm_tiles=n_norm_tiles, head_dim=head_dim,
                          rot_half=rot_half),
        grid=(ncols // bn, m // bm),
        in_specs=[
            pl.BlockSpec((bm, d), lambda n, i: (i, 0)),
            pl.BlockSpec((d, bn), lambda n, i: (0, n + off)),
            pl.BlockSpec((1, bn), lambda n, i: (0, n)),
            pl.BlockSpec((bm, tw), lambda n, i: (i, 0)),
            pl.BlockSpec((bm, tw), lambda n, i: (i, 0)),
            pl.BlockSpec((bm, tw), lambda n, i: (i, 0)),
        ],
        out_specs=pl.BlockSpec((bm, bn), lambda n, i: (i, n)),
        out_shape=jax.ShapeDtypeStruct((m, ncols), F32),
        scratch_shapes=[pltpu.VMEM((d, bn), BF16)],
        compiler_params=_cparams(("arbitrary", "arbitrary")),
        name="inproj_qkv",
    )(h, w, gain_row, cos_t, sa_t, sb_t)


def _gmlp_rows(u_ref, va_ref, w_ref, b_ref, gain_ref, o_ref, vn_ref, r0, sub):
    groups, rows, _ = w_ref.shape
    gw = u_ref.shape[1] // groups
    rs = slice(r0, r0 + rows)
    ri = lax.broadcasted_iota(I32, (rows, rows), 0)
    ci = lax.broadcasted_iota(I32, (rows, rows), 1)
    mask = (ci <= ri) & ((ri // sub) == (ci // sub))
    vn = _rms(va_ref[rs, :].astype(F32), gain_ref[...])
    if vn_ref is not None:
        vn_ref[rs, :] = vn
    vb = vn.astype(BF16)
    for g in range(groups):
        sl = slice(g * gw, (g + 1) * gw)
        wg = jnp.where(mask, w_ref[g], 0.0).astype(BF16)
        s = jnp.dot(wg, vb[:, sl], preferred_element_type=F32) + b_ref[:, g:g + 1]
        o_ref[rs, sl] = (u_ref[rs, sl].astype(F32) * s).astype(o_ref.dtype)


def _gate_gmlp_kernel(h_ref, w_ref, u_ref, va_ref, wp_ref, ws_ref, bp_ref, bs_ref, gain_ref,
                      g_ref, oa_ref, vn_ref, wb_ref, *, steps_per_col, n_prompt_steps, n_side_steps,
                      sample_len):
    i = pl.program_id(1)
    s = pl.program_id(0) * steps_per_col + i
    chunk = wp_ref.shape[1]

    @pl.when(i == 0)
    def _():
        _cast_rows(w_ref, wb_ref, 256)

    def gate():
        z = jnp.dot(h_ref[...], wb_ref[...], preferred_element_type=F32)
        g_ref[...] = jax.nn.sigmoid(z).astype(g_ref.dtype)

    @pl.when(s < n_prompt_steps)
    def _():
        gate()
        for r0 in range(0, u_ref.shape[0], chunk):
            _gmlp_rows(u_ref, va_ref, wp_ref, bp_ref, gain_ref, oa_ref, None, r0, chunk)

    @pl.when((s >= n_prompt_steps) & (s < n_side_steps))
    def _():
        gate()
        for r0 in range(0, u_ref.shape[0], chunk):
            _gmlp_rows(u_ref, va_ref, ws_ref, bs_ref, gain_ref, oa_ref, vn_ref, r0, sample_len)

    @pl.when(s >= n_side_steps)
    def _():
        gate()


def _gate_gmlp(h, w, col0, ncols, ua, w_prompt, w_sample, b_prompt, b_sample, gain, n_prompt_rows,
               sample_len):
    m, d = h.shape
    d_a = ua.shape[1] // 2
    groups, chunk, _ = w_prompt.shape
    bn = min(COL_TILE, ncols)
    side = min(GMLP_ROWS, m - n_prompt_rows)
    assert col0 % bn == 0 and ncols % bn == 0
    assert n_prompt_rows % side == 0 and (m - n_prompt_rows) % side == 0 and side % chunk == 0
    nps, nss = n_prompt_rows // side, m // side
    bm = _host_row_tile(m, ncols // bn, nss, WIDE_ROW_TILE)
    off = col0 // bn
    spc = m // bm
    step = lambda n, i: n * spc + i
    side_blk = lambda n, i: jnp.minimum(step(n, i), nss - 1)
    full3 = lambda n, i: (0, 0, 0)
    full2 = lambda n, i: (0, 0)
    return pl.pallas_call(
        functools.partial(_gate_gmlp_kernel, steps_per_col=spc, n_prompt_steps=nps, n_side_steps=nss,
                          sample_len=sample_len),
        grid=(ncols // bn, spc),
        in_specs=[
            pl.BlockSpec((bm, d), lambda n, i: (i, 0)),
            pl.BlockSpec((d, bn), lambda n, i: (0, n + off)),
            pl.BlockSpec((side, d_a), lambda n, i: (side_blk(n, i), 0)),
            pl.BlockSpec((side, d_a), lambda n, i: (side_blk(n, i), 1)),
            pl.BlockSpec((groups, chunk, chunk), full3),
            pl.BlockSpec((groups, chunk, chunk), full3),
            pl.BlockSpec((chunk, groups), full2),
            pl.BlockSpec((chunk, groups), full2),
            pl.BlockSpec((1, d_a), full2),
        ],
        out_specs=[
            pl.BlockSpec((bm, bn), lambda n, i: (i, n)),
            pl.BlockSpec((side, d_a), lambda n, i: (side_blk(n, i), 0)),
            pl.BlockSpec((side, d_a), lambda n, i: (jnp.clip(step(n, i) - nps, 0, nss - nps - 1), 0)),
        ],
        out_shape=[
            jax.ShapeDtypeStruct((m, ncols), BF16),
            jax.ShapeDtypeStruct((m, d_a), BF16),
            jax.ShapeDtypeStruct((m - n_prompt_rows, d_a), F32),
        ],
        scratch_shapes=[pltpu.VMEM((d, bn), BF16)],
        compiler_params=_cparams(("arbitrary", "arbitrary"), FUSED_VMEM_LIMIT_BYTES),
        name="inproj_gate_gmlp",
    )(h, w, ua, ua, w_prompt, w_sample, b_prompt, b_sample, gain)


def _gelu_attn_kernel(h_ref, w_ref, bias_ref, q_ref, ka_ref, kb_ref, kc_ref, kd_ref, va_ref, vb_ref,
                      vc_ref, vd_ref, ck_ref, cv_ref, ua_ref, o_ref, wb_ref, s_ref, e_ref, *,
                      steps_per_col, n_prompt_steps, n_side_steps, tiles_per_kv, head_dim):
    i = pl.program_id(0) * steps_per_col + pl.program_id(1)

    @pl.when(pl.program_id(1) == 0)
    def _():
        _cast_rows(w_ref, wb_ref, 256)

    def project():
        z = jnp.dot(h_ref[...], wb_ref[...], preferred_element_type=F32)
        ua_ref[...] = jax.nn.gelu(z).astype(ua_ref.dtype)

    t = q_ref.shape[0] // 2
    kv_dim = kc_ref.shape[1]
    lkp = bias_ref.shape[1] // 2
    scale = head_dim ** -0.5
    nt = (((1,), (1,)), ((), ()))
    n_kv = kv_dim // head_dim
    tiles_of = lambda j: [j * tiles_per_kv + a for a in range(tiles_per_kv)]

    def run(halves):
        prepared = []
        for k_parts, v_parts, first_key_chunk in halves:
            k = jnp.concatenate(k_parts, axis=0)
            v = jnp.concatenate(v_parts, axis=0)
            chunk_bias = None
            if first_key_chunk is not None:
                col = lax.broadcasted_iota(I32, (1, 2 * lkp), 1)
                key_chunk = first_key_chunk + (col % lkp) // t
                chunk_bias = jnp.where(key_chunk >= 0, 0.0, -jnp.inf)
            prepared.append((k, v, chunk_bias))
        lk = prepared[0][0].shape[0]
        lower = lax.broadcasted_iota(I32, (lk, LANES), 1) < head_dim
        zpad = jnp.zeros((lkp - lk, LANES), F32)
        rr = lax.broadcasted_iota(I32, (2 * lkp, LANES), 0) < lkp
        rl = lax.broadcasted_iota(I32, (2 * lkp, LANES), 1) < head_dim
        ones_sel = (rr == rl).astype(BF16)

        def doubled(tile, rolled, x):
            lo_src, hi_src = (tile, rolled) if x == 0 else (rolled, tile)
            return jnp.concatenate([jnp.where(lower, lo_src, 0.0), zpad,
                                    jnp.where(lower, 0.0, hi_src), zpad], axis=0).astype(BF16)

        for h, (k, _, chunk_bias) in enumerate(prepared):
            rows_h = slice(h * t, (h + 1) * t)
            for b in range(kv_dim // LANES):
                kt = k[:, b * LANES:(b + 1) * LANES]
                kr = pltpu.roll(kt, head_dim, 1)
                for x in range(2):
                    j = 2 * b + x
                    q2 = jnp.concatenate([q_ref[rows_h, a * LANES:(a + 1) * LANES] for a in tiles_of(j)],
                                         axis=0)
                    q2 = (q2 * scale).astype(BF16)
                    rows = []
                    for a in tiles_of(j):
                        row = bias_ref[a:a + 1, :]
                        if chunk_bias is not None:
                            row = row + chunk_bias
                        rows.append(jnp.broadcast_to(row, (t, 2 * lkp)))
                    s_ref[h, j] = (lax.dot_general(q2, doubled(kt, kr, x), nt, preferred_element_type=F32)
                                   + jnp.concatenate(rows, axis=0))

        for h in range(len(prepared)):
            for j in range(n_kv):
                for hs in (slice(0, lkp), slice(lkp, 2 * lkp)):
                    sh = s_ref[h, j, :, hs]
                    e_ref[h, j, :, hs] = jnp.exp(sh - jnp.max(sh, axis=-1, keepdims=True)).astype(BF16)

        for h, (_, v, _) in enumerate(prepared):
            for b in range(kv_dim // LANES):
                vt = v[:, b * LANES:(b + 1) * LANES]
                vr = pltpu.roll(vt, head_dim, 1)
                for x in range(2):
                    j = 2 * b + x
                    rhs = jnp.concatenate([doubled(vt, vr, x), ones_sel], axis=1)
                    r = jnp.dot(e_ref[h, j], rhs, preferred_element_type=F32)
                    o = r[:, :LANES] / r[:, LANES:]
                    for n, a in enumerate(tiles_of(j)):
                        o_ref[h * t:(h + 1) * t, a * LANES:(a + 1) * LANES] = (
                            o[n * t:(n + 1) * t].astype(o_ref.dtype))

    @pl.when(i < n_prompt_steps)
    def _():
        project()
        ka, kb, kc, kd = ka_ref[...], kb_ref[...], kc_ref[...], kd_ref[...]
        va, vb, vc, vd = va_ref[...], vb_ref[...], vc_ref[...], vd_ref[...]
        run([([ka, kb, kc], [va, vb, vc], 2 * i - 2), ([kb, kc, kd], [vb, vc, vd], 2 * i - 1)])

    @pl.when((i >= n_prompt_steps) & (i < n_side_steps))
    def _():
        project()
        w = ck_ref.shape[0] // 2
        run([([ck_ref[:w, :], kc_ref[...]], [cv_ref[:w, :], vc_ref[...]], None),
             ([ck_ref[w:, :], kd_ref[...]], [cv_ref[w:, :], vd_ref[...]], None)])

    @pl.when(i >= n_side_steps)
    def _():
        project()


def _gelu_attention(h, w, col0, ncols, qkv, cache_k, cache_v, sinks, n_prompt_rows, q_dim, kv_dim,
                    head_dim, n_heads):
    m, d = h.shape
    t = CHUNK
    nc = n_prompt_rows // t
    n_kv_heads = kv_dim // head_dim
    window = cache_k.shape[0] // ((m - n_prompt_rows) // t)
    kcol, vcol = q_dim // kv_dim, q_dim // kv_dim + 1
    gqa = n_heads // n_kv_heads
    assert 2 * head_dim == LANES and gqa % 2 == 0 and kv_dim % LANES == 0

    lk = window + t
    lkp = -(-(lk + 1) // LANES) * LANES
    pair = sinks.reshape(n_heads // 2, 2, 1)
    bias = jnp.concatenate([jnp.zeros((n_heads // 2, 2, lk), F32), pair,
                            jnp.full((n_heads // 2, 2, lkp - lk - 1), -jnp.inf, F32)], axis=2)
    bias = bias.reshape(n_heads // 2, 2 * lkp)

    assert nc % 2 == 0 and (m // t - nc) % 2 == 0, "query chunks are processed in pairs"
    nps, nss = nc // 2, m // (2 * t)

    bn = min(COL_TILE, ncols)
    assert col0 % bn == 0 and ncols % bn == 0
    bm = _host_row_tile(m, ncols // bn, nss, ROW_TILE)
    spc = m // bm
    off = col0 // bn
    pair_of = lambda n, i: jnp.minimum(n * spc + i, nss - 1)

    def kv_spec(back, col):
        return pl.BlockSpec((t, kv_dim), lambda n, i: (jnp.maximum(2 * pair_of(n, i) + back, 0), col))

    cache_spec = pl.BlockSpec((2 * window, kv_dim), lambda n, i: (jnp.maximum(pair_of(n, i) - nps, 0), 0))
    scratch = (2, n_kv_heads, gqa // 2 * t, 2 * lkp)
    return pl.pallas_call(
        functools.partial(_gelu_attn_kernel, steps_per_col=spc, n_prompt_steps=nps, n_side_steps=nss,
                          tiles_per_kv=gqa // 2, head_dim=head_dim),
        grid=(ncols // bn, spc),
        in_specs=[
            pl.BlockSpec((bm, d), lambda n, i: (i, 0)),
            pl.BlockSpec((d, bn), lambda n, i: (0, n + off)),
            pl.BlockSpec((n_heads // 2, 2 * lkp), lambda n, i: (0, 0)),
            pl.BlockSpec((2 * t, q_dim), lambda n, i: (pair_of(n, i), 0)),
            kv_spec(-2, kcol), kv_spec(-1, kcol), kv_spec(0, kcol), kv_spec(1, kcol),
            kv_spec(-2, vcol), kv_spec(-1, vcol), kv_spec(0, vcol), kv_spec(1, vcol),
            cache_spec, cache_spec,
        ],
        out_specs=[
            pl.BlockSpec((bm, bn), lambda n, i: (i, n)),
            pl.BlockSpec((2 * t, q_dim), lambda n, i: (pair_of(n, i), 0)),
        ],
        out_shape=[
            jax.ShapeDtypeStruct((m, ncols), BF16),
            jax.ShapeDtypeStruct((m, q_dim), BF16),
        ],
        scratch_shapes=[pltpu.VMEM((d, bn), BF16), pltpu.VMEM(scratch, F32), pltpu.VMEM(scratch, BF16)],
        compiler_params=_cparams(("arbitrary", "arbitrary")),
        name="inproj_gelu_attention",
    )(h, w, bias, qkv, qkv, qkv, qkv, qkv, qkv, qkv, qkv, qkv, cache_k, cache_v)


def _merge_kernel(oa_ref, ob_ref, pa_ref, pb_ref, ga_ref, gb_ref, o_ref, pab_ref, pbb_ref):
    @pl.when(pl.program_id(1) == 0)
    def _():
        _cast_rows(pa_ref, pab_ref, 256)
        _cast_rows(pb_ref, pbb_ref, 256)

    a = jnp.dot(oa_ref[...], pab_ref[...], preferred_element_type=F32)
    b = jnp.dot(ob_ref[...], pbb_ref[...], preferred_element_type=F32)
    o_ref[...] = (ga_ref[...].astype(F32) * a + gb_ref[...].astype(F32) * b).astype(o_ref.dtype)


def _merge(o_a, o_b, p_a, p_b, gates):
    m, d_a = o_a.shape
    q_dim = o_b.shape[1]
    d = p_a.shape[1]
    bm, bn = min(ROW_TILE, m), min(COL_TILE, d)
    goff = d // bn
    return pl.pallas_call(
        _merge_kernel,
        grid=(d // bn, m // bm),
        in_specs=[
            pl.BlockSpec((bm, d_a), lambda n, i: (i, 0)),
            pl.BlockSpec((bm, q_dim), lambda n, i: (i, 0)),
            pl.BlockSpec((d_a, bn), lambda n, i: (0, n)),
            pl.BlockSpec((q_dim, bn), lambda n, i: (0, n)),
            pl.BlockSpec((bm, bn), lambda n, i: (i, n)),
            pl.BlockSpec((bm, bn), lambda n, i: (i, n + goff)),
        ],
        out_specs=pl.BlockSpec((bm, bn), lambda n, i: (i, n)),
        out_shape=jax.ShapeDtypeStruct((m, d), BF16),
        scratch_shapes=[pltpu.VMEM((d_a, bn), BF16), pltpu.VMEM((q_dim, bn), BF16)],
        compiler_params=_cparams(("arbitrary", "arbitrary")),
        name="merge_proj",
    )(o_a, o_b, p_a, p_b, gates, gates)


def _outproj_kernel(t_ref, w_ref, xp_ref, xs_ref, o_ref, wb_ref, *, n_prompt_tiles):
    i = pl.program_id(1)

    @pl.when(i == 0)
    def _():
        _cast_rows(w_ref, wb_ref, 256)

    z = jnp.dot(t_ref[...], wb_ref[...], preferred_element_type=F32)

    @pl.when(i < n_prompt_tiles)
    def _():
        o_ref[...] = xp_ref[...] + z

    @pl.when(i >= n_prompt_tiles)
    def _():
        o_ref[...] = xs_ref[...] + z


def _outproj(tm, w_out, xp, xs):
    m, d = tm.shape
    s, t = xp.shape[0], xs.shape[0]
    bm, bn = min(ROW_TILE, t), min(COL_TILE, d)
    npt = s // bm
    return pl.pallas_call(
        functools.partial(_outproj_kernel, n_prompt_tiles=npt),
        grid=(d // bn, m // bm),
        in_specs=[
            pl.BlockSpec((bm, d), lambda n, i: (i, 0)),
            pl.BlockSpec((d, bn), lambda n, i: (0, n)),
            pl.BlockSpec((bm, bn), lambda n, i: (jnp.minimum(i, npt - 1), n)),
            pl.BlockSpec((bm, bn), lambda n, i: (jnp.maximum(i - npt, 0), n)),
        ],
        out_specs=pl.BlockSpec((bm, bn), lambda n, i: (i, n)),
        out_shape=jax.ShapeDtypeStruct((m, d), F32),
        scratch_shapes=[pltpu.VMEM((d, bn), BF16)],
        compiler_params=_cparams(("arbitrary", "arbitrary")),
        name="out_proj",
    )(tm, w_out, xp, xs)


def _router_kernel(x_ref, g_ref, wr_ref, br_ref, route_ref, cnt_ref, wcat_ref, carry_ref, *,
                   n_groups, per_group):
    i = pl.program_id(0)

    @pl.when(i == 0)
    def _():
        w = wr_ref[...]
        hi = w.astype(BF16)
        wcat_ref[:, :LANES] = hi
        wcat_ref[:, LANES:] = (w - hi.astype(F32)).astype(BF16)
        carry_ref[...] = jnp.zeros_like(carry_ref)

    hn = _rms(x_ref[...], g_ref[...])
    rows = hn.shape[0]
    hi = hn.astype(BF16)
    lo = (hn - hi.astype(F32)).astype(BF16)
    prod = jnp.dot(jnp.concatenate([hi, lo], axis=0), wcat_ref[...], preferred_element_type=F32)
    logits = prod[:rows, :LANES] + prod[:rows, LANES:] + prod[rows:, :LANES] + br_ref[...]
    lane = lax.broadcasted_iota(I32, logits.shape, 1)
    big = jnp.int32(LANES)

    lg = jnp.where(lane < n_groups, logits, -jnp.inf)
    mg = jnp.max(lg, axis=-1, keepdims=True)
    pg_sel = 1.0 / jnp.sum(jnp.exp(lg - mg), axis=-1, keepdims=True)
    gsel = jnp.min(jnp.where(lg == mg, lane, big), axis=-1, keepdims=True)

    first = n_groups + gsel * per_group
    emask = (lane >= first) & (lane < first + per_group)
    le = jnp.where(emask, logits, -jnp.inf)
    me = jnp.max(le, axis=-1, keepdims=True)
    ee = jnp.exp(le - me)
    pe = jnp.where(emask, ee / jnp.sum(ee, axis=-1, keepdims=True), -1.0)
    p1 = jnp.max(pe, axis=-1, keepdims=True)
    i1 = jnp.min(jnp.where(pe == p1, lane, big), axis=-1, keepdims=True)
    pe2 = jnp.where(lane == i1, -1.0, pe)
    p2 = jnp.max(pe2, axis=-1, keepdims=True)
    i2 = jnp.min(jnp.where(pe2 == p2, lane, big), axis=-1, keepdims=True)
    psum = p1 + p2
    w1 = p1 / psum * pg_sel
    w2 = p2 / psum * pg_sel
    e1 = i1 - n_groups
    e2 = i2 - n_groups

    oh1 = (lane == e1).astype(F32)
    oh2 = (lane == e2).astype(F32)
    ohs = oh1 + oh2
    ri = lax.broadcasted_iota(I32, (rows, rows), 0)
    ci = lax.broadcasted_iota(I32, (rows, rows), 1)
    below = (ci < ri).astype(BF16)
    before = jnp.dot(below, ohs.astype(BF16), preferred_element_type=F32) + carry_ref[...]
    r1 = jnp.sum(before * oh1, axis=-1, keepdims=True)
    r2 = jnp.sum(before * oh2, axis=-1, keepdims=True)
    carry_ref[...] = carry_ref[...] + jnp.sum(ohs, axis=0, keepdims=True)
    cnt_ref[...] = carry_ref[...]

    route = jnp.where(lane == 0, e1.astype(F32), 0.0)
    route = jnp.where(lane == 1, e2.astype(F32), route)
    route = jnp.where(lane == 2, w1, route)
    route = jnp.where(lane == 3, w2, route)
    route = jnp.where(lane == 4, r1, route)
    route = jnp.where(lane == 5, r2, route)
    route_ref[...] = route


def _router(x2, gain, wr, br, n_groups, per_group):
    m, d = x2.shape
    br_rows = _row_tile(ROUTE_ROWS, m)
    return pl.pallas_call(
        functools.partial(_router_kernel, n_groups=n_groups, per_group=per_group),
        grid=(m // br_rows,),
        in_specs=[
            pl.BlockSpec((br_rows, d), lambda i: (i, 0)),
            pl.BlockSpec((1, d), lambda i: (0, 0)),
            pl.BlockSpec((d, LANES), lambda i: (0, 0)),
            pl.BlockSpec((1, LANES), lambda i: (0, 0)),
        ],
        out_specs=[
            pl.BlockSpec((br_rows, LANES), lambda i: (i, 0)),
            pl.BlockSpec((1, LANES), lambda i: (0, 0)),
        ],
        out_shape=[
            jax.ShapeDtypeStruct((m, LANES), F32),
            jax.ShapeDtypeStruct((1, LANES), F32),
        ],
        scratch_shapes=[pltpu.VMEM((d, 2 * LANES), BF16), pltpu.VMEM((1, LANES), F32)],
        compiler_params=_cparams(("arbitrary",)),
        name="norm2_router",
    )(x2, gain, wr, br)


def _pack_pair(first, second):
    hi = lax.bitcast_convert_type(first.astype(BF16).astype(F32), U32)
    lo = lax.bitcast_convert_type(second.astype(BF16).astype(F32), U32)
    return hi | (lo >> 16)


def _pack_halves(x):
    half = x.shape[1] // 2
    return _pack_pair(x[:, :half], x[:, half:])


def _unpack_halves(p):
    first = lax.bitcast_convert_type(p & jnp.uint32(0xFFFF0000), F32)
    second = lax.bitcast_convert_type(p << 16, F32)
    return first, second


def _dispatch_kernel(dest_ref, x_ref, g_ref, xs_ref, buf_ref, sem, *, n_tiles):
    i = pl.program_id(0)
    tm = x_ref.shape[0]
    n_tokens = n_tiles * tm
    slot = i % 2

    def wait_rows(s):
        for k in range(TOP_K):
            pltpu.make_async_copy(buf_ref.at[s], xs_ref.at[pl.ds(0, tm)], sem.at[s]).wait()

    buf_ref[slot] = _pack_halves(_rms(x_ref[...], g_ref[...]))

    def start(r, c):
        for k in range(TOP_K):
            d = dest_ref[k * n_tokens + i * tm + r]
            pltpu.make_async_copy(buf_ref.at[slot, pl.ds(r, 1)], xs_ref.at[pl.ds(d, 1)],
                                  sem.at[slot]).start()
        return c

    lax.fori_loop(0, tm, start, 0, unroll=DMA_UNROLL)

    @pl.when(i > 0)
    def _():
        wait_rows(1 - slot)

    @pl.when(i == n_tiles - 1)
    def _():
        wait_rows(slot)


def _dispatch(dest_flat, x2, gain, n_rows):
    m, d = x2.shape
    tm = _row_tile(DISPATCH_ROWS, m)
    return pl.pallas_call(
        functools.partial(_dispatch_kernel, n_tiles=m // tm),
        grid_spec=pltpu.PrefetchScalarGridSpec(
            num_scalar_prefetch=1,
            grid=(m // tm,),
            in_specs=[
                pl.BlockSpec((tm, d), lambda i, dest: (i, 0)),
                pl.BlockSpec((1, d), lambda i, dest: (0, 0)),
            ],
            out_specs=pl.BlockSpec(memory_space=pl.ANY),
            scratch_shapes=[pltpu.VMEM((2, tm, d // 2), U32), pltpu.SemaphoreType.DMA((2,))],
        ),
        out_shape=jax.ShapeDtypeStruct((n_rows, d // 2), U32),
        compiler_params=_cparams(("arbitrary",)),
        name="moe_dispatch",
    )(dest_flat, x2, gain)


def _moe_kernel(be_ref, nrows_ref, xs_ref, wg_ref, wu_ref, wd_ref, y_ref, x1_ref, x2_ref, act_ref, *,
                n_split):
    i = pl.program_id(0)
    j = pl.program_id(1)
    rows, half = xs_ref.shape
    dh = wg_ref.shape[2]

    @pl.when(j == 0)
    def _():
        live = lax.broadcasted_iota(I32, (rows, 1), 0) < nrows_ref[i]
        x1, x2 = _unpack_halves(xs_ref[...])
        x1_ref[...] = jnp.where(live, x1, 0.0).astype(BF16)
        x2_ref[...] = jnp.where(live, x2, 0.0).astype(BF16)

    def proj(w_ref):
        return (jnp.dot(x1_ref[...], w_ref[0, :half, :].astype(BF16), preferred_element_type=F32)
                + jnp.dot(x2_ref[...], w_ref[0, half:, :].astype(BF16), preferred_element_type=F32))

    act = (jax.nn.silu(proj(wg_ref)) * proj(wu_ref)).astype(BF16)
    for q in range(n_split):
        @pl.when(j == q)
        def _(q=q):
            act_ref[:, q * dh:(q + 1) * dh] = act

    @pl.when(j == n_split - 1)
    def _():
        a = act_ref[...]
        first = jnp.dot(a, wd_ref[0, :, :half].astype(BF16), preferred_element_type=F32)
        second = jnp.dot(a, wd_ref[0, :, half:].astype(BF16), preferred_element_type=F32)
        y_ref[...] = _pack_pair(first, second)


def _moe_experts(block_e, block_rows, n_occupied, xs, w_gate, w_up, w_down):
    half = xs.shape[1]
    d = 2 * half
    d_e = w_gate.shape[2]
    n_split = 2
    dh = d_e // n_split
    br = MOE_ROWS
    return pl.pallas_call(
        functools.partial(_moe_kernel, n_split=n_split),
        grid_spec=pltpu.PrefetchScalarGridSpec(
            num_scalar_prefetch=2,
            grid=(n_occupied, n_split),
            in_specs=[
                pl.BlockSpec((br, half), lambda i, j, be, nr: (i, 0)),
                pl.BlockSpec((1, d, dh), lambda i, j, be, nr: (be[i], 0, j)),
                pl.BlockSpec((1, d, dh), lambda i, j, be, nr: (be[i], 0, j)),
                pl.BlockSpec((1, d_e, d), lambda i, j, be, nr: (be[i], 0, 0)),
            ],
            out_specs=pl.BlockSpec((br, half), lambda i, j, be, nr: (i, 0)),
            scratch_shapes=[pltpu.VMEM((br, half), BF16), pltpu.VMEM((br, half), BF16),
                            pltpu.VMEM((br, d_e), BF16)],
        ),
        out_shape=jax.ShapeDtypeStruct(xs.shape, U32),
        compiler_params=_cparams(("arbitrary", "arbitrary")),
        name="moe_experts",
    )(block_e, block_rows, xs, w_gate, w_up, w_down)


def _combine_kernel(dest_ref, x_ref, route_ref, y_ref, op_ref, os_ref, buf_ref, sem, *,
                    n_prompt_tiles, n_tiles):
    i = pl.program_id(0)
    tm = x_ref.shape[0]
    n_tokens = n_tiles * tm
    slot = i % 2

    def gather(tile, to_slot):
        def start(r, c):
            for k in range(TOP_K):
                d = dest_ref[k * n_tokens + tile * tm + r]
                pltpu.make_async_copy(y_ref.at[pl.ds(d, 1)], buf_ref.at[to_slot, k, pl.ds(r, 1)],
                                      sem.at[to_slot]).start()
            return c

        lax.fori_loop(0, tm, start, 0, unroll=DMA_UNROLL)

    @pl.when(i == 0)
    def _():
        gather(0, 0)

    @pl.when(i + 1 < n_tiles)
    def _():
        gather(i + 1, 1 - slot)

    for k in range(TOP_K):
        pltpu.make_async_copy(y_ref.at[pl.ds(0, tm)], buf_ref.at[slot, k], sem.at[slot]).wait()

    half = x_ref.shape[1] // 2
    first, second = x_ref[:, :half], x_ref[:, half:]
    for k in range(TOP_K):
        y1, y2 = _unpack_halves(buf_ref[slot, k])
        w = route_ref[:, 2 + k:3 + k]
        first = first + w * y1
        second = second + w * y2
    out = jnp.concatenate([first, second], axis=1)

    @pl.when(i < n_prompt_tiles)
    def _():
        op_ref[...] = out

    @pl.when(i >= n_prompt_tiles)
    def _():
        os_ref[...] = out


def _combine(dest_flat, x2, route, y, n_prompt_rows):
    m, d = x2.shape
    tm = _row_tile(COMBINE_ROWS, n_prompt_rows, m - n_prompt_rows)
    npt = n_prompt_rows // tm
    return pl.pallas_call(
        functools.partial(_combine_kernel, n_prompt_tiles=npt, n_tiles=m // tm),
        grid_spec=pltpu.PrefetchScalarGridSpec(
            num_scalar_prefetch=1,
            grid=(m // tm,),
            in_specs=[
                pl.BlockSpec((tm, d), lambda i, dest: (i, 0)),
                pl.BlockSpec((tm, LANES), lambda i, dest: (i, 0)),
                pl.BlockSpec(memory_space=pl.ANY),
            ],
            out_specs=[
                pl.BlockSpec((tm, d), lambda i, dest: (jnp.minimum(i, npt - 1), 0)),
                pl.BlockSpec((tm, d), lambda i, dest: (jnp.maximum(i - npt, 0), 0)),
            ],
            scratch_shapes=[pltpu.VMEM((2, TOP_K, tm, d // 2), U32), pltpu.SemaphoreType.DMA((2,))],
        ),
        out_shape=[
            jax.ShapeDtypeStruct((n_prompt_rows, d), F32),
            jax.ShapeDtypeStruct((m - n_prompt_rows, d), F32),
        ],
        compiler_params=_cparams(("arbitrary",)),
        name="moe_combine",
    )(dest_flat, x2, route, y)


def _rope_tables(positions, head_dim):
    rot_dim = head_dim // 4
    half = rot_dim // 2
    inv_freq = jnp.power(ROPE_THETA, -jnp.arange(half, dtype=F32) * 2.0 / rot_dim)
    ang = positions.astype(F32)[:, None] * inv_freq[None, :]
    cos, sin = lax.optimization_barrier((jnp.cos(ang), jnp.sin(ang)))
    m = positions.shape[0]
    zeros = lambda n: jnp.zeros((m, n), F32)
    cos_h = jnp.concatenate([cos, cos, jnp.ones((m, head_dim - rot_dim), F32)], axis=1)
    sa_h = jnp.concatenate([-sin, zeros(head_dim - half)], axis=1)
    sb_h = jnp.concatenate([zeros(half), sin, zeros(head_dim - rot_dim)], axis=1)
    reps = LANES // head_dim
    return tuple(jnp.tile(a, (1, reps)) for a in (cos_h, sa_h, sb_h)), half


def _layer(xp, xs, cache_k, cache_v, norm1_g, w_in, gmlp_norm_g, w_s, b_s, q_norm_g, k_norm_g,
           sinks, p_a, p_b, w_out, norm2_g, w_rg, b_rg, w_re, b_re, w_gate_e, w_up_e, w_down_e):
    s, d = xp.shape
    dec_rows = xs.shape[0]
    dec_batch = cache_k.shape[0]
    t = dec_rows // dec_batch
    m = s + dec_rows
    d_a = gmlp_norm_g.shape[0]
    head_dim = q_norm_g.shape[0]
    n_heads = sinks.shape[0]
    q_dim = n_heads * head_dim
    kv_dim = cache_k.shape[2] * cache_k.shape[3]
    n_groups, per_group = w_re.shape[1], w_re.shape[2]
    n_experts = n_groups * per_group

    h = _norm1(xp, xs, norm1_g[None, :])
    positions = jnp.concatenate(
        [jnp.arange(s, dtype=I32), jnp.tile(PAST_LEN + jnp.arange(t, dtype=I32), dec_batch)])
    (cos_t, sa_t, sb_t), rot_half = _rope_tables(positions, head_dim)
    gain_row = jnp.concatenate([jnp.tile(q_norm_g, n_heads), jnp.tile(k_norm_g, kv_dim // head_dim),
                                jnp.ones((kv_dim,), F32)])[None, :]
    qkv = _inproj_qkv(h, w_in, 2 * d_a, q_dim, kv_dim, gain_row, cos_t, sa_t, sb_t, head_dim, rot_half)

    ua, o_b = _gelu_attention(h, w_in, 0, 2 * d_a, qkv, cache_k.reshape(-1, kv_dim),
                              cache_v.reshape(-1, kv_dim), sinks[None, :], s, q_dim, kv_dim, head_dim,
                              n_heads)
    rows = w_s.shape[1]
    reps = rows // t
    w_sample = jnp.tile(w_s[:, :t, :t], (1, reps, reps))
    b_sample = jnp.tile(b_s[:, :t], (1, reps))
    gates, o_a, vn_s = _gate_gmlp(h, w_in, 2 * d_a + q_dim + 2 * kv_dim, 2 * d, ua, w_s, w_sample,
                                  b_s.T, b_sample.T, gmlp_norm_g[None, :], s, t)

    merged = _merge(o_a, o_b, p_a, p_b, gates)
    x2 = _outproj(merged, w_out, xp, xs)

    pad = LANES - n_groups - n_experts
    wr = jnp.concatenate([w_rg, w_re.reshape(d, n_experts), jnp.zeros((d, pad), F32)], axis=1)
    br = jnp.concatenate([b_rg, b_re.reshape(n_experts), jnp.zeros((pad,), F32)])[None, :]
    route, counts = _router(x2, norm2_g[None, :], wr, br, n_groups, per_group)
    e_idx = route[:, 0:TOP_K].T.astype(I32)
    rank = route[:, 4:4 + TOP_K].T.astype(I32)
    counts = counts[0, :n_experts].astype(I32)

    blk = MOE_ROWS
    n_blocks = -(-(m * TOP_K) // blk) + n_experts
    padded = ((counts + blk - 1) // blk) * blk
    pend = jnp.cumsum(padded)
    pstart = pend - padded
    hit = e_idx[:, :, None] == jnp.arange(n_experts, dtype=I32)
    dest = (jnp.sum(jnp.where(hit, pstart, 0), axis=-1) + rank).reshape(-1).astype(I32)
    n_occupied = (pend[-1] // blk).astype(I32)
    first_row = jnp.arange(n_blocks, dtype=I32) * blk
    block_e = jnp.minimum(jnp.sum((pend[None, :] <= first_row[:, None]).astype(I32), axis=1), n_experts - 1)
    block_rows = jnp.clip(counts[block_e] - (first_row - pstart[block_e]), 0, blk).astype(I32)

    xs_sorted = _dispatch(dest, x2, norm2_g[None, :], n_blocks * blk)
    y_sorted = _moe_experts(block_e, block_rows, n_occupied, xs_sorted, w_gate_e, w_up_e, w_down_e)
    yp, ys = _combine(dest, x2, route, y_sorted, s)
    return yp, ys, qkv, vn_s


def kernel(x_prompt, x_sample, cache_k_win, cache_v_win, norm1_g, w_in, gmlp_norm_g, w_s, b_s,
           q_norm_g, k_norm_g, sinks, p_a, p_b, w_out, norm2_g, w_rg, b_rg, w_re, b_re,
           w_gate_e, w_up_e, w_down_e):
    depth = norm1_g.shape[0]
    assert depth == 1, "weights of one layer are expected"
    batch, s, d = x_prompt.shape
    assert batch == 1
    dec_batch, t, _ = x_sample.shape
    head_dim = q_norm_g.shape[-1]
    n_kv = cache_k_win.shape[3]
    q_dim = sinks.shape[-1] * head_dim
    kv_dim = n_kv * head_dim
    keep = min(cache_k_win.shape[2], s)

    l = 0
    yp, ys, qkv, vn_s = _layer(
        x_prompt.reshape(s, d), x_sample.reshape(dec_batch * t, d), cache_k_win[l], cache_v_win[l],
        norm1_g[l], w_in[l], gmlp_norm_g[l], w_s[l], b_s[l], q_norm_g[l], k_norm_g[l], sinks[l],
        p_a[l], p_b[l], w_out[l], norm2_g[l], w_rg[l], b_rg[l], w_re[l], b_re[l],
        w_gate_e[l], w_up_e[l], w_down_e[l])

    k_all = qkv[:, q_dim:q_dim + kv_dim]
    v_all = qkv[:, q_dim + kv_dim:]
    k_win_p = k_all[s - keep:s].reshape(1, batch, keep, n_kv, head_dim)
    v_win_p = v_all[s - keep:s].reshape(1, batch, keep, n_kv, head_dim)
    k_new_s = k_all[s:].reshape(1, dec_batch, t, n_kv, head_dim)
    v_new_s = v_all[s:].reshape(1, dec_batch, t, n_kv, head_dim)
    gv_s = vn_s.reshape(1, dec_batch, t, -1)
    return (yp.reshape(batch, s, d), ys.reshape(dec_batch, t, d), k_win_p, v_win_p, k_new_s, v_new_s, gv_s)
```

```python
import functools

import jax
import jax.numpy as jnp
from jax import lax
from jax.experimental import pallas as pl
from jax.experimental.pallas import tpu as pltpu

F32 = jnp.float32
BF16 = jnp.bfloat16
I32 = jnp.int32
U32 = jnp.uint32

EPS = 1e-6
PAST_LEN = 1024
CHUNK = 64
ROPE_THETA = 500000.0
TOP_K = 2
LANES = 128

VMEM_LIMIT_BYTES = 56 * 1024 * 1024
FUSED_VMEM_LIMIT_BYTES = 58 * 1024 * 1024

ROW_TILE = 1024
WIDE_ROW_TILE = 1536
COL_TILE = 512
SEG_WIDTH = 256
GMLP_ROWS = 128
NORM_ROWS = 512
ROUTE_ROWS = 512
MOE_ROWS = 320
COMBINE_ROWS = 256
DISPATCH_ROWS = 512
DMA_UNROLL = 8


def _cparams(sem, vmem_limit_bytes=VMEM_LIMIT_BYTES):
    return pltpu.CompilerParams(dimension_semantics=sem, vmem_limit_bytes=vmem_limit_bytes)


def _cast_rows(src_ref, dst_ref, rows):
    n = src_ref.shape[0] // rows

    def body(r, c):
        sl = pl.ds(pl.multiple_of(r * rows, rows), rows)
        dst_ref[sl, :] = src_ref[sl, :].astype(dst_ref.dtype)
        return c

    lax.fori_loop(0, n, body, 0)


def _row_tile(preferred, *row_counts):
    tile = preferred
    while any(n % tile for n in row_counts):
        tile //= 2
    assert tile >= 8
    return tile


def _rms(x, gain):
    ms = jnp.mean(x * x, axis=-1, keepdims=True)
    return x * lax.rsqrt(ms + EPS) * gain


def _norm1_kernel(xp_ref, xs_ref, g_ref, h_ref, *, n_prompt_blocks):
    i = pl.program_id(0)

    @pl.when(i < n_prompt_blocks)
    def _():
        h_ref[...] = _rms(xp_ref[...], g_ref[...]).astype(h_ref.dtype)

    @pl.when(i >= n_prompt_blocks)
    def _():
        h_ref[...] = _rms(xs_ref[...], g_ref[...]).astype(h_ref.dtype)


def _norm1(xp, xs, gain):
    s, d = xp.shape
    t = xs.shape[0]
    br = _row_tile(NORM_ROWS, s, t)
    nbp, nbs = s // br, t // br
    return pl.pallas_call(
        functools.partial(_norm1_kernel, n_prompt_blocks=nbp),
        grid=(nbp + nbs,),
        in_specs=[
            pl.BlockSpec((br, d), lambda i: (jnp.minimum(i, nbp - 1), 0)),
            pl.BlockSpec((br, d), lambda i: (jnp.maximum(i - nbp, 0), 0)),
            pl.BlockSpec((1, d), lambda i: (0, 0)),
        ],
        out_specs=pl.BlockSpec((br, d), lambda i: (i, 0)),
        out_shape=jax.ShapeDtypeStruct((s + t, d), BF16),
        compiler_params=_cparams(("arbitrary",)),
        name="norm1",
    )(xp, xs, gain)


def _host_row_tile(m, n_col_tiles, n_side_steps, preferred):
    bm = preferred if m % preferred == 0 else min(ROW_TILE, m)
    while n_col_tiles * (m // bm) < n_side_steps:
        assert bm % 16 == 0
        bm //= 2
    assert m % bm == 0
    return bm


def _inproj_qkv_kernel(h_ref, w_ref, gain_ref, cos_ref, sa_ref, sb_ref, o_ref, wb_ref, *,
                       n_norm_tiles, head_dim, rot_half):
    n = pl.program_id(0)

    @pl.when(pl.program_id(1) == 0)
    def _():
        _cast_rows(w_ref, wb_ref, 256)

    z = jnp.dot(h_ref[...], wb_ref[...], preferred_element_type=F32)
    bn = z.shape[1]

    @pl.when(n < n_norm_tiles)
    def _():
        sw = min(SEG_WIDTH, bn)
        r = lax.broadcasted_iota(I32, (sw, sw), 0) // head_dim
        c = lax.broadcasted_iota(I32, (sw, sw), 1) // head_dim
        seg = (r == c).astype(BF16)
        zz = (z * z).astype(BF16)
        ssq = jnp.concatenate([jnp.dot(zz[:, c0:c0 + sw], seg, preferred_element_type=F32)
                               for c0 in range(0, bn, sw)], axis=1)
        y = z * lax.rsqrt(ssq * (1.0 / head_dim) + EPS) * gain_ref[...]
        reps = bn // cos_ref.shape[1]
        cosv = jnp.tile(cos_ref[...], (1, reps))
        sa = jnp.tile(sa_ref[...], (1, reps))
        sb = jnp.tile(sb_ref[...], (1, reps))
        y = y * cosv + pltpu.roll(y, bn - rot_half, 1) * sa + pltpu.roll(y, rot_half, 1) * sb
        o_ref[...] = y

    @pl.when(n >= n_norm_tiles)
    def _():
        o_ref[...] = z


def _inproj_qkv(h, w, col0, q_dim, kv_dim, gain_row, cos_t, sa_t, sb_t, head_dim, rot_half):
    m, d = h.shape
    ncols = q_dim + 2 * kv_dim
    bm, bn = min(ROW_TILE, m), kv_dim
    assert col0 % bn == 0 and q_dim % bn == 0 and bn % LANES == 0 and m % bm == 0
    off = col0 // bn
    n_norm_tiles = (q_dim + kv_dim) // bn
    tw = cos_t.shape[1]
    return pl.pallas_call(
        functools.partial(_inproj_qkv_kernel, n_norm_tiles=n_norm_tiles, head_dim=head_dim,
                          rot_half=rot_half),
        grid=(ncols // bn, m // bm),
        in_specs=[
            pl.BlockSpec((bm, d), lambda n, i: (i, 0)),
            pl.BlockSpec((d, bn), lambda n, i: (0, n + off)),
            pl.BlockSpec((1, bn), lambda n, i: (0, n)),
            pl.BlockSpec((bm, tw), lambda n, i: (i, 0)),
            pl.BlockSpec((bm, tw), lambda n, i: (i, 0)),
            pl.BlockSpec((bm, tw), lambda n, i: (i, 0)),
        ],
        out_specs=pl.BlockSpec((bm, bn), lambda n, i: (i, n)),
        out_shape=jax.ShapeDtypeStruct((m, ncols), F32),
        scratch_shapes=[pltpu.VMEM((d, bn), BF16)],
        compiler_params=_cparams(("arbitrary", "arbitrary")),
        name="inproj_qkv",
    )(h, w, gain_row, cos_t, sa_t, sb_t)


def _gmlp_rows(u_ref, va_ref, w_ref, b_ref, gain_ref, o_ref, vn_ref, r0, sub):
    groups, rows, _ = w_ref.shape
    gw = u_ref.shape[1] // groups
    rs = slice(r0, r0 + rows)
    ri = lax.broadcasted_iota(I32, (rows, rows), 0)
    ci = lax.broadcasted_iota(I32, (rows, rows), 1)
    mask = (ci <= ri) & ((ri // sub) == (ci // sub))
    vn = _rms(va_ref[rs, :].astype(F32), gain_ref[...])
    if vn_ref is not None:
        vn_ref[rs, :] = vn
    vb = vn.astype(BF16)
    for g in range(groups):
        sl = slice(g * gw, (g + 1) * gw)
        wg = jnp.where(mask, w_ref[g], 0.0).astype(BF16)
        s = jnp.dot(wg, vb[:, sl], preferred_element_type=F32) + b_ref[:, g:g + 1]
        o_ref[rs, sl] = (u_ref[rs, sl].astype(F32) * s).astype(o_ref.dtype)


def _gate_gmlp_kernel(h_ref, w_ref, u_ref, va_ref, wp_ref, ws_ref, bp_ref, bs_ref, gain_ref,
                      g_ref, oa_ref, vn_ref, wb_ref, *, steps_per_col, n_prompt_steps, n_side_steps,
                      sample_len):
    i = pl.program_id(1)
    s = pl.program_id(0) * steps_per_col + i
    chunk = wp_ref.shape[1]

    @pl.when(i == 0)
    def _():
        _cast_rows(w_ref, wb_ref, 256)

    def gate():
        z = jnp.dot(h_ref[...], wb_ref[...], preferred_element_type=F32)
        g_ref[...] = jax.nn.sigmoid(z).astype(g_ref.dtype)

    @pl.when(s < n_prompt_steps)
    def _():
        gate()
        for r0 in range(0, u_ref.shape[0], chunk):
            _gmlp_rows(u_ref, va_ref, wp_ref, bp_ref, gain_ref, oa_ref, None, r0, chunk)

    @pl.when((s >= n_prompt_steps) & (s < n_side_steps))
    def _():
        gate()
        for r0 in range(0, u_ref.shape[0], chunk):
            _gmlp_rows(u_ref, va_ref, ws_ref, bs_ref, gain_ref, oa_ref, vn_ref, r0, sample_len)

    @pl.when(s >= n_side_steps)
    def _():
        gate()


def _gate_gmlp(h, w, col0, ncols, ua, w_prompt, w_sample, b_prompt, b_sample, gain, n_prompt_rows,
               sample_len):
    m, d = h.shape
    d_a = ua.shape[1] // 2
    groups, chunk, _ = w_prompt.shape
    bn = min(COL_TILE, ncols)
    side = min(GMLP_ROWS, m - n_prompt_rows)
    assert col0 % bn == 0 and ncols % bn == 0
    assert n_prompt_rows % side == 0 and (m - n_prompt_rows) % side == 0 and side % chunk == 0
    nps, nss = n_prompt_rows // side, m // side
    bm = _host_row_tile(m, ncols // bn, nss, WIDE_ROW_TILE)
    off = col0 // bn
    spc = m // bm
    step = lambda n, i: n * spc + i
    side_blk = lambda n, i: jnp.minimum(step(n, i), nss - 1)
    full3 = lambda n, i: (0, 0, 0)
    full2 = lambda n, i: (0, 0)
    return pl.pallas_call(
        functools.partial(_gate_gmlp_kernel, steps_per_col=spc, n_prompt_steps=nps, n_side_steps=nss,
                          sample_len=sample_len),
        grid=(ncols // bn, spc),
        in_specs=[
            pl.BlockSpec((bm, d), lambda n, i: (i, 0)),
            pl.BlockSpec((d, bn), lambda n, i: (0, n + off)),
            pl.BlockSpec((side, d_a), lambda n, i: (side_blk(n, i), 0)),
            pl.BlockSpec((side, d_a), lambda n, i: (side_blk(n, i), 1)),
            pl.BlockSpec((groups, chunk, chunk), full3),
            pl.BlockSpec((groups, chunk, chunk), full3),
            pl.BlockSpec((chunk, groups), full2),
            pl.BlockSpec((chunk, groups), full2),
            pl.BlockSpec((1, d_a), full2),
        ],
        out_specs=[
            pl.BlockSpec((bm, bn), lambda n, i: (i, n)),
            pl.BlockSpec((side, d_a), lambda n, i: (side_blk(n, i), 0)),
            pl.BlockSpec((side, d_a), lambda n, i: (jnp.clip(step(n, i) - nps, 0, nss - nps - 1), 0)),
        ],
        out_shape=[
            jax.ShapeDtypeStruct((m, ncols), BF16),
            jax.ShapeDtypeStruct((m, d_a), BF16),
            jax.ShapeDtypeStruct((m - n_prompt_rows, d_a), F32),
        ],
        scratch_shapes=[pltpu.VMEM((d, bn), BF16)],
        compiler_params=_cparams(("arbitrary", "arbitrary"), FUSED_VMEM_LIMIT_BYTES),
        name="inproj_gate_gmlp",
    )(h, w, ua, ua, w_prompt, w_sample, b_prompt, b_sample, gain)


def _gelu_attn_kernel(h_ref, w_ref, bias_ref, q_ref, ka_ref, kb_ref, kc_ref, kd_ref, va_ref, vb_ref,
                      vc_ref, vd_ref, ck_ref, cv_ref, ua_ref, o_ref, wb_ref, s_ref, e_ref, *,
                      steps_per_col, n_prompt_steps, n_side_steps, tiles_per_kv, head_dim):
    i = pl.program_id(0) * steps_per_col + pl.program_id(1)

    @pl.when(pl.program_id(1) == 0)
    def _():
        _cast_rows(w_ref, wb_ref, 256)

    def project():
        z = jnp.dot(h_ref[...], wb_ref[...], preferred_element_type=F32)
        ua_ref[...] = jax.nn.gelu(z).astype(ua_ref.dtype)

    t = q_ref.shape[0] // 2
    kv_dim = kc_ref.shape[1]
    lkp = bias_ref.shape[1] // 2
    scale = head_dim ** -0.5
    nt = (((1,), (1,)), ((), ()))
    n_kv = kv_dim // head_dim
    tiles_of = lambda j: [j * tiles_per_kv + a for a in range(tiles_per_kv)]

    def run(halves):
        prepared = []
        for k_parts, v_parts, first_key_chunk in halves:
            k = jnp.concatenate(k_parts, axis=0)
            v = jnp.concatenate(v_parts, axis=0)
            chunk_bias = None
            if first_key_chunk is not None:
                col = lax.broadcasted_iota(I32, (1, 2 * lkp), 1)
                key_chunk = first_key_chunk + (col % lkp) // t
                chunk_bias = jnp.where(key_chunk >= 0, 0.0, -jnp.inf)
            prepared.append((k, v, chunk_bias))
        lk = prepared[0][0].shape[0]
        lower = lax.broadcasted_iota(I32, (lk, LANES), 1) < head_dim
        zpad = jnp.zeros((lkp - lk, LANES), F32)
        rr = lax.broadcasted_iota(I32, (2 * lkp, LANES), 0) < lkp
        rl = lax.broadcasted_iota(I32, (2 * lkp, LANES), 1) < head_dim
        ones_sel = (rr == rl).astype(BF16)

        def doubled(tile, rolled, x):
            lo_src, hi_src = (tile, rolled) if x == 0 else (rolled, tile)
            return jnp.concatenate([jnp.where(lower, lo_src, 0.0), zpad,
                                    jnp.where(lower, 0.0, hi_src), zpad], axis=0).astype(BF16)

        for h, (k, _, chunk_bias) in enumerate(prepared):
            rows_h = slice(h * t, (h + 1) * t)
            for b in range(kv_dim // LANES):
                kt = k[:, b * LANES:(b + 1) * LANES]
                kr = pltpu.roll(kt, head_dim, 1)
                for x in range(2):
                    j = 2 * b + x
                    q2 = jnp.concatenate([q_ref[rows_h, a * LANES:(a + 1) * LANES] for a in tiles_of(j)],
                                         axis=0)
                    q2 = (q2 * scale).astype(BF16)
                    rows = []
                    for a in tiles_of(j):
                        row = bias_ref[a:a + 1, :]
                        if chunk_bias is not None:
                            row = row + chunk_bias
                        rows.append(jnp.broadcast_to(row, (t, 2 * lkp)))
                    s_ref[h, j] = (lax.dot_general(q2, doubled(kt, kr, x), nt, preferred_element_type=F32)
                                   + jnp.concatenate(rows, axis=0))

        for h in range(len(prepared)):
            for j in range(n_kv):
                for hs in (slice(0, lkp), slice(lkp, 2 * lkp)):
                    sh = s_ref[h, j, :, hs]
                    e_ref[h, j, :, hs] = jnp.exp(sh - jnp.max(sh, axis=-1, keepdims=True)).astype(BF16)

        for h, (_, v, _) in enumerate(prepared):
            for b in range(kv_dim // LANES):
                vt = v[:, b * LANES:(b + 1) * LANES]
                vr = pltpu.roll(vt, head_dim, 1)
                for x in range(2):
                    j = 2 * b + x
                    rhs = jnp.concatenate([doubled(vt, vr, x), ones_sel], axis=1)
                    r = jnp.dot(e_ref[h, j], rhs, preferred_element_type=F32)
                    o = r[:, :LANES] / r[:, LANES:]
                    for n, a in enumerate(tiles_of(j)):
                        o_ref[h * t:(h + 1) * t, a * LANES:(a + 1) * LANES] = (
                            o[n * t:(n + 1) * t].astype(o_ref.dtype))

    @pl.when(i < n_prompt_steps)
    def _():
        project()
        ka, kb, kc, kd = ka_ref[...], kb_ref[...], kc_ref[...], kd_ref[...]
        va, vb, vc, vd = va_ref[...], vb_ref[...], vc_ref[...], vd_ref[...]
        run([([ka, kb, kc], [va, vb, vc], 2 * i - 2), ([kb, kc, kd], [vb, vc, vd], 2 * i - 1)])

    @pl.when((i >= n_prompt_steps) & (i < n_side_steps))
    def _():
        project()
        w = ck_ref.shape[0] // 2
        run([([ck_ref[:w, :], kc_ref[...]], [cv_ref[:w, :], vc_ref[...]], None),
             ([ck_ref[w:, :], kd_ref[...]], [cv_ref[w:, :], vd_ref[...]], None)])

    @pl.when(i >= n_side_steps)
    def _():
        project()


def _gelu_attention(h, w, col0, ncols, qkv, cache_k, cache_v, sinks, n_prompt_rows, q_dim, kv_dim,
                    head_dim, n_heads):
    m, d = h.shape
    t = CHUNK
    nc = n_prompt_rows // t
    n_kv_heads = kv_dim // head_dim
    window = cache_k.shape[0] // ((m - n_prompt_rows) // t)
    kcol, vcol = q_dim // kv_dim, q_dim // kv_dim + 1
    gqa = n_heads // n_kv_heads
    assert 2 * head_dim == LANES and gqa % 2 == 0 and kv_dim % LANES == 0

    lk = window + t
    lkp = -(-(lk + 1) // LANES) * LANES
    pair = sinks.reshape(n_heads // 2, 2, 1)
    bias = jnp.concatenate([jnp.zeros((n_heads // 2, 2, lk), F32), pair,
                            jnp.full((n_heads // 2, 2, lkp - lk - 1), -jnp.inf, F32)], axis=2)
    bias = bias.reshape(n_heads // 2, 2 * lkp)

    assert nc % 2 == 0 and (m // t - nc) % 2 == 0, "query chunks are processed in pairs"
    nps, nss = nc // 2, m // (2 * t)

    bn = min(COL_TILE, ncols)
    assert col0 % bn == 0 and ncols % bn == 0
    bm = _host_row_tile(m, ncols // bn, nss, ROW_TILE)
    spc = m // bm
    off = col0 // bn
    pair_of = lambda n, i: jnp.minimum(n * spc + i, nss - 1)

    def kv_spec(back, col):
        return pl.BlockSpec((t, kv_dim), lambda n, i: (jnp.maximum(2 * pair_of(n, i) + back, 0), col))

    cache_spec = pl.BlockSpec((2 * window, kv_dim), lambda n, i: (jnp.maximum(pair_of(n, i) - nps, 0), 0))
    scratch = (2, n_kv_heads, gqa // 2 * t, 2 * lkp)
    return pl.pallas_call(
        functools.partial(_gelu_attn_kernel, steps_per_col=spc, n_prompt_steps=nps, n_side_steps=nss,
                          tiles_per_kv=gqa // 2, head_dim=head_dim),
        grid=(ncols // bn, spc),
        in_specs=[
            pl.BlockSpec((bm, d), lambda n, i: (i, 0)),
            pl.BlockSpec((d, bn), lambda n, i: (0, n + off)),
            pl.BlockSpec((n_heads // 2, 2 * lkp), lambda n, i: (0, 0)),
            pl.BlockSpec((2 * t, q_dim), lambda n, i: (pair_of(n, i), 0)),
            kv_spec(-2, kcol), kv_spec(-1, kcol), kv_spec(0, kcol), kv_spec(1, kcol),
            kv_spec(-2, vcol), kv_spec(-1, vcol), kv_spec(0, vcol), kv_spec(1, vcol),
            cache_spec, cache_spec,
        ],
        out_specs=[
            pl.BlockSpec((bm, bn), lambda n, i: (i, n)),
            pl.BlockSpec((2 * t, q_dim), lambda n, i: (pair_of(n, i), 0)),
        ],
        out_shape=[
            jax.ShapeDtypeStruct((m, ncols), BF16),
            jax.ShapeDtypeStruct((m, q_dim), BF16),
        ],
        scratch_shapes=[pltpu.VMEM((d, bn), BF16), pltpu.VMEM(scratch, F32), pltpu.VMEM(scratch, BF16)],
        compiler_params=_cparams(("arbitrary", "arbitrary")),
        name="inproj_gelu_attention",
    )(h, w, bias, qkv, qkv, qkv, qkv, qkv, qkv, qkv, qkv, qkv, cache_k, cache_v)


def _merge_kernel(oa_ref, ob_ref, pa_ref, pb_ref, ga_ref, gb_ref, o_ref, pab_ref, pbb_ref):
    @pl.when(pl.program_id(1) == 0)
    def _():
        _cast_rows(pa_ref, pab_ref, 256)
        _cast_rows(pb_ref, pbb_ref, 256)

    a = jnp.dot(oa_ref[...], pab_ref[...], preferred_element_type=F32)
    b = jnp.dot(ob_ref[...], pbb_ref[...], preferred_element_type=F32)
    o_ref[...] = (ga_ref[...].astype(F32) * a + gb_ref[...].astype(F32) * b).astype(o_ref.dtype)


def _merge(o_a, o_b, p_a, p_b, gates):
    m, d_a = o_a.shape
    q_dim = o_b.shape[1]
    d = p_a.shape[1]
    bm, bn = min(ROW_TILE, m), min(COL_TILE, d)
    goff = d // bn
    return pl.pallas_call(
        _merge_kernel,
        grid=(d // bn, m // bm),
        in_specs=[
            pl.BlockSpec((bm, d_a), lambda n, i: (i, 0)),
            pl.BlockSpec((bm, q_dim), lambda n, i: (i, 0)),
            pl.BlockSpec((d_a, bn), lambda n, i: (0, n)),
            pl.BlockSpec((q_dim, bn), lambda n, i: (0, n)),
            pl.BlockSpec((bm, bn), lambda n, i: (i, n)),
            pl.BlockSpec((bm, bn), lambda n, i: (i, n + goff)),
        ],
        out_specs=pl.BlockSpec((bm, bn), lambda n, i: (i, n)),
        out_shape=jax.ShapeDtypeStruct((m, d), BF16),
        scratch_shapes=[pltpu.VMEM((d_a, bn), BF16), pltpu.VMEM((q_dim, bn), BF16)],
        compiler_params=_cparams(("arbitrary", "arbitrary")),
        name="merge_proj",
    )(o_a, o_b, p_a, p_b, gates, gates)


def _outproj_kernel(t_ref, w_ref, xp_ref, xs_ref, o_ref, wb_ref, *, n_prompt_tiles):
    i = pl.program_id(1)

    @pl.when(i == 0)
    def _():
        _cast_rows(w_ref, wb_ref, 256)

    z = jnp.dot(t_ref[...], wb_ref[...], preferred_element_type=F32)

    @pl.when(i < n_prompt_tiles)
    def _():
        o_ref[...] = xp_ref[...] + z

    @pl.when(i >= n_prompt_tiles)
    def _():
        o_ref[...] = xs_ref[...] + z


def _outproj(tm, w_out, xp, xs):
    m, d = tm.shape
    s, t = xp.shape[0], xs.shape[0]
    bm, bn = min(ROW_TILE, t), min(COL_TILE, d)
    npt = s // bm
    return pl.pallas_call(
        functools.partial(_outproj_kernel, n_prompt_tiles=npt),
        grid=(d // bn, m // bm),
        in_specs=[
            pl.BlockSpec((bm, d), lambda n, i: (i, 0)),
            pl.BlockSpec((d, bn), lambda n, i: (0, n)),
            pl.BlockSpec((bm, bn), lambda n, i: (jnp.minimum(i, npt - 1), n)),
            pl.BlockSpec((bm, bn), lambda n, i: (jnp.maximum(i - npt, 0), n)),
        ],
        out_specs=pl.BlockSpec((bm, bn), lambda n, i: (i, n)),
        out_shape=jax.ShapeDtypeStruct((m, d), F32),
        scratch_shapes=[pltpu.VMEM((d, bn), BF16)],
        compiler_params=_cparams(("arbitrary", "arbitrary")),
        name="out_proj",
    )(tm, w_out, xp, xs)


def _router_kernel(x_ref, g_ref, wr_ref, br_ref, route_ref, cnt_ref, hp_ref, wcat_ref, carry_ref, *,
                   n_groups, per_group):
    i = pl.program_id(0)

    @pl.when(i == 0)
    def _():
        w = wr_ref[...]
        hi = w.astype(BF16)
        wcat_ref[:, :LANES] = hi
        wcat_ref[:, LANES:] = (w - hi.astype(F32)).astype(BF16)
        carry_ref[...] = jnp.zeros_like(carry_ref)

    hn = _rms(x_ref[...], g_ref[...])
    hp_ref[...] = _pack_halves(hn)
    rows = hn.shape[0]
    hi = hn.astype(BF16)
    lo = (hn - hi.astype(F32)).astype(BF16)
    prod = jnp.dot(jnp.concatenate([hi, lo], axis=0), wcat_ref[...], preferred_element_type=F32)
    logits = prod[:rows, :LANES] + prod[:rows, LANES:] + prod[rows:, :LANES] + br_ref[...]
    lane = lax.broadcasted_iota(I32, logits.shape, 1)
    big = jnp.int32(LANES)

    lg = jnp.where(lane < n_groups, logits, -jnp.inf)
    mg = jnp.max(lg, axis=-1, keepdims=True)
    pg_sel = 1.0 / jnp.sum(jnp.exp(lg - mg), axis=-1, keepdims=True)
    gsel = jnp.min(jnp.where(lg == mg, lane, big), axis=-1, keepdims=True)

    first = n_groups + gsel * per_group
    emask = (lane >= first) & (lane < first + per_group)
    le = jnp.where(emask, logits, -jnp.inf)
    me = jnp.max(le, axis=-1, keepdims=True)
    ee = jnp.exp(le - me)
    pe = jnp.where(emask, ee / jnp.sum(ee, axis=-1, keepdims=True), -1.0)
    p1 = jnp.max(pe, axis=-1, keepdims=True)
    i1 = jnp.min(jnp.where(pe == p1, lane, big), axis=-1, keepdims=True)
    pe2 = jnp.where(lane == i1, -1.0, pe)
    p2 = jnp.max(pe2, axis=-1, keepdims=True)
    i2 = jnp.min(jnp.where(pe2 == p2, lane, big), axis=-1, keepdims=True)
    psum = p1 + p2
    w1 = p1 / psum * pg_sel
    w2 = p2 / psum * pg_sel
    e1 = i1 - n_groups
    e2 = i2 - n_groups

    oh1 = (lane == e1).astype(F32)
    oh2 = (lane == e2).astype(F32)
    ohs = oh1 + oh2
    ri = lax.broadcasted_iota(I32, (rows, rows), 0)
    ci = lax.broadcasted_iota(I32, (rows, rows), 1)
    below = (ci < ri).astype(BF16)
    before = jnp.dot(below, ohs.astype(BF16), preferred_element_type=F32) + carry_ref[...]
    r1 = jnp.sum(before * oh1, axis=-1, keepdims=True)
    r2 = jnp.sum(before * oh2, axis=-1, keepdims=True)
    carry_ref[...] = carry_ref[...] + jnp.sum(ohs, axis=0, keepdims=True)
    cnt_ref[...] = carry_ref[...]

    route = jnp.where(lane == 0, e1.astype(F32), 0.0)
    route = jnp.where(lane == 1, e2.astype(F32), route)
    route = jnp.where(lane == 2, w1, route)
    route = jnp.where(lane == 3, w2, route)
    route = jnp.where(lane == 4, r1, route)
    route = jnp.where(lane == 5, r2, route)
    route_ref[...] = route


def _router(x2, gain, wr, br, n_groups, per_group):
    m, d = x2.shape
    br_rows = _row_tile(ROUTE_ROWS, m)
    return pl.pallas_call(
        functools.partial(_router_kernel, n_groups=n_groups, per_group=per_group),
        grid=(m // br_rows,),
        in_specs=[
            pl.BlockSpec((br_rows, d), lambda i: (i, 0)),
            pl.BlockSpec((1, d), lambda i: (0, 0)),
            pl.BlockSpec((d, LANES), lambda i: (0, 0)),
            pl.BlockSpec((1, LANES), lambda i: (0, 0)),
        ],
        out_specs=[
            pl.BlockSpec((br_rows, LANES), lambda i: (i, 0)),
            pl.BlockSpec((1, LANES), lambda i: (0, 0)),
            pl.BlockSpec((br_rows, d // 2), lambda i: (i, 0)),
        ],
        out_shape=[
            jax.ShapeDtypeStruct((m, LANES), F32),
            jax.ShapeDtypeStruct((1, LANES), F32),
            jax.ShapeDtypeStruct((m, d // 2), U32),
        ],
        scratch_shapes=[pltpu.VMEM((d, 2 * LANES), BF16), pltpu.VMEM((1, LANES), F32)],
        compiler_params=_cparams(("arbitrary",)),
        name="norm2_router",
    )(x2, gain, wr, br)


def _pack_pair(first, second):
    hi = lax.bitcast_convert_type(first.astype(BF16).astype(F32), U32)
    lo = lax.bitcast_convert_type(second.astype(BF16).astype(F32), U32)
    return hi | (lo >> 16)


def _pack_halves(x):
    half = x.shape[1] // 2
    return _pack_pair(x[:, :half], x[:, half:])


def _unpack_halves(p):
    first = lax.bitcast_convert_type(p & jnp.uint32(0xFFFF0000), F32)
    second = lax.bitcast_convert_type(p << 16, F32)
    return first, second


def _dispatch_kernel(dest_ref, hp_ref, xs_ref, buf_ref, sem, *, n_tiles):
    i = pl.program_id(0)
    tm = hp_ref.shape[0]
    n_tokens = n_tiles * tm
    slot = i % 2

    def wait_rows(s):
        for k in range(TOP_K):
            pltpu.make_async_copy(buf_ref.at[s], xs_ref.at[pl.ds(0, tm)], sem.at[s]).wait()

    buf_ref[slot] = hp_ref[...]

    def start(r, c):
        for k in range(TOP_K):
            d = dest_ref[k * n_tokens + i * tm + r]
            pltpu.make_async_copy(buf_ref.at[slot, pl.ds(r, 1)], xs_ref.at[pl.ds(d, 1)],
                                  sem.at[slot]).start()
        return c

    lax.fori_loop(0, tm, start, 0, unroll=DMA_UNROLL)

    @pl.when(i > 0)
    def _():
        wait_rows(1 - slot)

    @pl.when(i == n_tiles - 1)
    def _():
        wait_rows(slot)


def _dispatch(dest_flat, hp, n_rows):
    m, half = hp.shape
    tm = _row_tile(DISPATCH_ROWS, m)
    return pl.pallas_call(
        functools.partial(_dispatch_kernel, n_tiles=m // tm),
        grid_spec=pltpu.PrefetchScalarGridSpec(
            num_scalar_prefetch=1,
            grid=(m // tm,),
            in_specs=[pl.BlockSpec((tm, half), lambda i, dest: (i, 0))],
            out_specs=pl.BlockSpec(memory_space=pl.ANY),
            scratch_shapes=[pltpu.VMEM((2, tm, half), U32), pltpu.SemaphoreType.DMA((2,))],
        ),
        out_shape=jax.ShapeDtypeStruct((n_rows, half), U32),
        compiler_params=_cparams(("arbitrary",)),
        name="moe_dispatch",
    )(dest_flat, hp)


def _moe_kernel(be_ref, nrows_ref, xs_ref, wg_ref, wu_ref, wd_ref, y_ref, x1_ref, x2_ref, act_ref, *,
                n_split):
    i = pl.program_id(0)
    j = pl.program_id(1)
    rows, half = xs_ref.shape
    dh = wg_ref.shape[2]

    @pl.when(j == 0)
    def _():
        live = lax.broadcasted_iota(I32, (rows, 1), 0) < nrows_ref[i]
        x1, x2 = _unpack_halves(xs_ref[...])
        x1_ref[...] = jnp.where(live, x1, 0.0).astype(BF16)
        x2_ref[...] = jnp.where(live, x2, 0.0).astype(BF16)

    def proj(w_ref):
        return (jnp.dot(x1_ref[...], w_ref[0, :half, :].astype(BF16), preferred_element_type=F32)
                + jnp.dot(x2_ref[...], w_ref[0, half:, :].astype(BF16), preferred_element_type=F32))

    act = (jax.nn.silu(proj(wg_ref)) * proj(wu_ref)).astype(BF16)
    for q in range(n_split):
        @pl.when(j == q)
        def _(q=q):
            act_ref[:, q * dh:(q + 1) * dh] = act

    @pl.when(j == n_split - 1)
    def _():
        a = act_ref[...]
        first = jnp.dot(a, wd_ref[0, :, :half].astype(BF16), preferred_element_type=F32)
        second = jnp.dot(a, wd_ref[0, :, half:].astype(BF16), preferred_element_type=F32)
        y_ref[...] = _pack_pair(first, second)


def _moe_experts(block_e, block_rows, n_occupied, xs, w_gate, w_up, w_down):
    half = xs.shape[1]
    d = 2 * half
    d_e = w_gate.shape[2]
    n_split = 2
    dh = d_e // n_split
    br = MOE_ROWS
    return pl.pallas_call(
        functools.partial(_moe_kernel, n_split=n_split),
        grid_spec=pltpu.PrefetchScalarGridSpec(
            num_scalar_prefetch=2,
            grid=(n_occupied, n_split),
            in_specs=[
                pl.BlockSpec((br, half), lambda i, j, be, nr: (i, 0)),
                pl.BlockSpec((1, d, dh), lambda i, j, be, nr: (be[i], 0, j)),
                pl.BlockSpec((1, d, dh), lambda i, j, be, nr: (be[i], 0, j)),
                pl.BlockSpec((1, d_e, d), lambda i, j, be, nr: (be[i], 0, 0)),
            ],
            out_specs=pl.BlockSpec((br, half), lambda i, j, be, nr: (i, 0)),
            scratch_shapes=[pltpu.VMEM((br, half), BF16), pltpu.VMEM((br, half), BF16),
                            pltpu.VMEM((br, d_e), BF16)],
        ),
        out_shape=jax.ShapeDtypeStruct(xs.shape, U32),
        compiler_params=_cparams(("arbitrary", "arbitrary")),
        name="moe_experts",
    )(block_e, block_rows, xs, w_gate, w_up, w_down)


def _combine_kernel(dest_ref, x_ref, route_ref, y_ref, op_ref, os_ref, buf_ref, sem, *,
                    n_prompt_tiles, n_tiles):
    i = pl.program_id(0)
    tm = x_ref.shape[0]
    n_tokens = n_tiles * tm
    slot = i % 2

    def gather(tile, to_slot):
        def start(r, c):
            for k in range(TOP_K):
                d = dest_ref[k * n_tokens + tile * tm + r]
                pltpu.make_async_copy(y_ref.at[pl.ds(d, 1)], buf_ref.at[to_slot, k, pl.ds(r, 1)],
                                      sem.at[to_slot]).start()
            return c

        lax.fori_loop(0, tm, start, 0, unroll=DMA_UNROLL)

    @pl.when(i == 0)
    def _():
        gather(0, 0)

    @pl.when(i + 1 < n_tiles)
    def _():
        gather(i + 1, 1 - slot)

    for k in range(TOP_K):
        pltpu.make_async_copy(y_ref.at[pl.ds(0, tm)], buf_ref.at[slot, k], sem.at[slot]).wait()

    half = x_ref.shape[1] // 2
    first, second = x_ref[:, :half], x_ref[:, half:]
    for k in range(TOP_K):
        y1, y2 = _unpack_halves(buf_ref[slot, k])
        w = route_ref[:, 2 + k:3 + k]
        first = first + w * y1
        second = second + w * y2
    out = jnp.concatenate([first, second], axis=1)

    @pl.when(i < n_prompt_tiles)
    def _():
        op_ref[...] = out

    @pl.when(i >= n_prompt_tiles)
    def _():
        os_ref[...] = out


def _combine(dest_flat, x2, route, y, n_prompt_rows):
    m, d = x2.shape
    tm = _row_tile(COMBINE_ROWS, n_prompt_rows, m - n_prompt_rows)
    npt = n_prompt_rows // tm
    return pl.pallas_call(
        functools.partial(_combine_kernel, n_prompt_tiles=npt, n_tiles=m // tm),
        grid_spec=pltpu.PrefetchScalarGridSpec(
            num_scalar_prefetch=1,
            grid=(m // tm,),
            in_specs=[
                pl.BlockSpec((tm, d), lambda i, dest: (i, 0)),
                pl.BlockSpec((tm, LANES), lambda i, dest: (i, 0)),
                pl.BlockSpec(memory_space=pl.ANY),
            ],
            out_specs=[
                pl.BlockSpec((tm, d), lambda i, dest: (jnp.minimum(i, npt - 1), 0)),
                pl.BlockSpec((tm, d), lambda i, dest: (jnp.maximum(i - npt, 0), 0)),
            ],
            scratch_shapes=[pltpu.VMEM((2, TOP_K, tm, d // 2), U32), pltpu.SemaphoreType.DMA((2,))],
        ),
        out_shape=[
            jax.ShapeDtypeStruct((n_prompt_rows, d), F32),
            jax.ShapeDtypeStruct((m - n_prompt_rows, d), F32),
        ],
        compiler_params=_cparams(("arbitrary",)),
        name="moe_combine",
    )(dest_flat, x2, route, y)


def _rope_tables(positions, head_dim):
    rot_dim = head_dim // 4
    half = rot_dim // 2
    inv_freq = jnp.power(ROPE_THETA, -jnp.arange(half, dtype=F32) * 2.0 / rot_dim)
    ang = positions.astype(F32)[:, None] * inv_freq[None, :]
    cos, sin = lax.optimization_barrier((jnp.cos(ang), jnp.sin(ang)))
    m = positions.shape[0]
    zeros = lambda n: jnp.zeros((m, n), F32)
    cos_h = jnp.concatenate([cos, cos, jnp.ones((m, head_dim - rot_dim), F32)], axis=1)
    sa_h = jnp.concatenate([-sin, zeros(head_dim - half)], axis=1)
    sb_h = jnp.concatenate([zeros(half), sin, zeros(head_dim - rot_dim)], axis=1)
    reps = LANES // head_dim
    return tuple(jnp.tile(a, (1, reps)) for a in (cos_h, sa_h, sb_h)), half


def _layer(xp, xs, cache_k, cache_v, norm1_g, w_in, gmlp_norm_g, w_s, b_s, q_norm_g, k_norm_g,
           sinks, p_a, p_b, w_out, norm2_g, w_rg, b_rg, w_re, b_re, w_gate_e, w_up_e, w_down_e):
    s, d = xp.shape
    dec_rows = xs.shape[0]
    dec_batch = cache_k.shape[0]
    t = dec_rows // dec_batch
    m = s + dec_rows
    d_a = gmlp_norm_g.shape[0]
    head_dim = q_norm_g.shape[0]
    n_heads = sinks.shape[0]
    q_dim = n_heads * head_dim
    kv_dim = cache_k.shape[2] * cache_k.shape[3]
    n_groups, per_group = w_re.shape[1], w_re.shape[2]
    n_experts = n_groups * per_group

    h = _norm1(xp, xs, norm1_g[None, :])
    positions = jnp.concatenate(
        [jnp.arange(s, dtype=I32), jnp.tile(PAST_LEN + jnp.arange(t, dtype=I32), dec_batch)])
    (cos_t, sa_t, sb_t), rot_half = _rope_tables(positions, head_dim)
    gain_row = jnp.concatenate([jnp.tile(q_norm_g, n_heads), jnp.tile(k_norm_g, kv_dim // head_dim),
                                jnp.ones((kv_dim,), F32)])[None, :]
    qkv = _inproj_qkv(h, w_in, 2 * d_a, q_dim, kv_dim, gain_row, cos_t, sa_t, sb_t, head_dim, rot_half)

    ua, o_b = _gelu_attention(h, w_in, 0, 2 * d_a, qkv, cache_k.reshape(-1, kv_dim),
                              cache_v.reshape(-1, kv_dim), sinks[None, :], s, q_dim, kv_dim, head_dim,
                              n_heads)
    rows = w_s.shape[1]
    reps = rows // t
    w_sample = jnp.tile(w_s[:, :t, :t], (1, reps, reps))
    b_sample = jnp.tile(b_s[:, :t], (1, reps))
    gates, o_a, vn_s = _gate_gmlp(h, w_in, 2 * d_a + q_dim + 2 * kv_dim, 2 * d, ua, w_s, w_sample,
                                  b_s.T, b_sample.T, gmlp_norm_g[None, :], s, t)

    merged = _merge(o_a, o_b, p_a, p_b, gates)
    x2 = _outproj(merged, w_out, xp, xs)

    pad = LANES - n_groups - n_experts
    wr = jnp.concatenate([w_rg, w_re.reshape(d, n_experts), jnp.zeros((d, pad), F32)], axis=1)
    br = jnp.concatenate([b_rg, b_re.reshape(n_experts), jnp.zeros((pad,), F32)])[None, :]
    route, counts, h2_packed = _router(x2, norm2_g[None, :], wr, br, n_groups, per_group)
    e_idx = route[:, 0:TOP_K].T.astype(I32)
    rank = route[:, 4:4 + TOP_K].T.astype(I32)
    counts = counts[0, :n_experts].astype(I32)

    blk = MOE_ROWS
    n_blocks = -(-(m * TOP_K) // blk) + n_experts
    padded = ((counts + blk - 1) // blk) * blk
    pend = jnp.cumsum(padded)
    pstart = pend - padded
    hit = e_idx[:, :, None] == jnp.arange(n_experts, dtype=I32)
    dest = (jnp.sum(jnp.where(hit, pstart, 0), axis=-1) + rank).reshape(-1).astype(I32)
    n_occupied = (pend[-1] // blk).astype(I32)
    first_row = jnp.arange(n_blocks, dtype=I32) * blk
    block_e = jnp.minimum(jnp.sum((pend[None, :] <= first_row[:, None]).astype(I32), axis=1), n_experts - 1)
    block_rows = jnp.clip(counts[block_e] - (first_row - pstart[block_e]), 0, blk).astype(I32)

    xs_sorted = _dispatch(dest, h2_packed, n_blocks * blk)
    y_sorted = _moe_experts(block_e, block_rows, n_occupied, xs_sorted, w_gate_e, w_up_e, w_down_e)
    yp, ys = _combine(dest, x2, route, y_sorted, s)
    return yp, ys, qkv, vn_s


def kernel(x_prompt, x_sample, cache_k_win, cache_v_win, norm1_g, w_in, gmlp_norm_g, w_s, b_s,
           q_norm_g, k_norm_g, sinks, p_a, p_b, w_out, norm2_g, w_rg, b_rg, w_re, b_re,
           w_gate_e, w_up_e, w_down_e):
    depth = norm1_g.shape[0]
    assert depth == 1, "weights of one layer are expected"
    batch, s, d = x_prompt.shape
    assert batch == 1
    dec_batch, t, _ = x_sample.shape
    head_dim = q_norm_g.shape[-1]
    n_kv = cache_k_win.shape[3]
    q_dim = sinks.shape[-1] * head_dim
    kv_dim = n_kv * head_dim
    keep = min(cache_k_win.shape[2], s)

    l = 0
    yp, ys, qkv, vn_s = _layer(
        x_prompt.reshape(s, d), x_sample.reshape(dec_batch * t, d), cache_k_win[l], cache_v_win[l],
        norm1_g[l], w_in[l], gmlp_norm_g[l], w_s[l], b_s[l], q_norm_g[l], k_norm_g[l], sinks[l],
        p_a[l], p_b[l], w_out[l], norm2_g[l], w_rg[l], b_rg[l], w_re[l], b_re[l],
        w_gate_e[l], w_up_e[l], w_down_e[l])

    k_all = qkv[:, q_dim:q_dim + kv_dim]
    v_all = qkv[:, q_dim + kv_dim:]
    k_win_p = k_all[s - keep:s].reshape(1, batch, keep, n_kv, head_dim)
    v_win_p = v_all[s - keep:s].reshape(1, batch, keep, n_kv, head_dim)
    k_new_s = k_all[s:].reshape(1, dec_batch, t, n_kv, head_dim)
    v_new_s = v_all[s:].reshape(1, dec_batch, t, n_kv, head_dim)
    gv_s = vn_s.reshape(1, dec_batch, t, -1)
    return (yp.reshape(batch, s, d), ys.reshape(dec_batch, t, d), k_win_p, v_win_p, k_new_s, v_new_s, gv_s)
```

```python
import functools

import jax
import jax.numpy as jnp
from jax import lax
from jax.experimental import pallas as pl
from jax.experimental.pallas import tpu as pltpu

F32 = jnp.float32
BF16 = jnp.bfloat16
I32 = jnp.int32
U32 = jnp.uint32

EPS = 1e-6
PAST_LEN = 1024
CHUNK = 64
ROPE_THETA = 500000.0
TOP_K = 2
LANES = 128

VMEM_LIMIT_BYTES = 56 * 1024 * 1024
FUSED_VMEM_LIMIT_BYTES = 58 * 1024 * 1024

ROW_TILE = 1024
WIDE_ROW_TILE = 1536
COL_TILE = 512
SEG_WIDTH = 256
GMLP_ROWS = 128
NORM_ROWS = 512
ROUTE_ROWS = 512
MOE_ROWS = 320
COMBINE_ROWS = 256
DISPATCH_ROWS = 512
DMA_UNROLL = 8


def _cparams(sem, vmem_limit_bytes=VMEM_LIMIT_BYTES):
    return pltpu.CompilerParams(dimension_semantics=sem, vmem_limit_bytes=vmem_limit_bytes)


def _cast_rows(src_ref, dst_ref, rows):
    n = src_ref.shape[0] // rows

    def body(r, c):
        sl = pl.ds(pl.multiple_of(r * rows, rows), rows)
        dst_ref[sl, :] = src_ref[sl, :].astype(dst_ref.dtype)
        return c

    lax.fori_loop(0, n, body, 0)


def _row_tile(preferred, *row_counts):
    tile = preferred
    while any(n % tile for n in row_counts):
        tile //= 2
    assert tile >= 8
    return tile


def _rms(x, gain):
    ms = jnp.mean(x * x, axis=-1, keepdims=True)
    return x * lax.rsqrt(ms + EPS) * gain


def _norm1_kernel(xp_ref, xs_ref, g_ref, h_ref, *, n_prompt_blocks):
    i = pl.program_id(0)

    @pl.when(i < n_prompt_blocks)
    def _():
        h_ref[...] = _rms(xp_ref[...], g_ref[...]).astype(h_ref.dtype)

    @pl.when(i >= n_prompt_blocks)
    def _():
        h_ref[...] = _rms(xs_ref[...], g_ref[...]).astype(h_ref.dtype)


def _norm1(xp, xs, gain):
    s, d = xp.shape
    t = xs.shape[0]
    br = _row_tile(NORM_ROWS, s, t)
    nbp, nbs = s // br, t // br
    return pl.pallas_call(
        functools.partial(_norm1_kernel, n_prompt_blocks=nbp),
        grid=(nbp + nbs,),
        in_specs=[
            pl.BlockSpec((br, d), lambda i: (jnp.minimum(i, nbp - 1), 0)),
            pl.BlockSpec((br, d), lambda i: (jnp.maximum(i - nbp, 0), 0)),
            pl.BlockSpec((1, d), lambda i: (0, 0)),
        ],
        out_specs=pl.BlockSpec((br, d), lambda i: (i, 0)),
        out_shape=jax.ShapeDtypeStruct((s + t, d), BF16),
        compiler_params=_cparams(("arbitrary",)),
        name="norm1",
    )(xp, xs, gain)


def _host_row_tile(m, n_col_tiles, n_side_steps, preferred):
    bm = preferred if m % preferred == 0 else min(ROW_TILE, m)
    while n_col_tiles * (m // bm) < n_side_steps:
        assert bm % 16 == 0
        bm //= 2
    assert m % bm == 0
    return bm


def _inproj_qkv_kernel(h_ref, w_ref, gain_ref, cos_ref, sa_ref, sb_ref, o_ref, wb_ref, *,
                       n_norm_tiles, head_dim, rot_half):
    n = pl.program_id(0)

    @pl.when(pl.program_id(1) == 0)
    def _():
        _cast_rows(w_ref, wb_ref, 256)

    z = jnp.dot(h_ref[...], wb_ref[...], preferred_element_type=F32)
    bn = z.shape[1]

    @pl.when(n < n_norm_tiles)
    def _():
        sw = min(SEG_WIDTH, bn)
        r = lax.broadcasted_iota(I32, (sw, sw), 0) // head_dim
        c = lax.broadcasted_iota(I32, (sw, sw), 1) // head_dim
        seg = (r == c).astype(BF16)
        zz = (z * z).astype(BF16)
        ssq = jnp.concatenate([jnp.dot(zz[:, c0:c0 + sw], seg, preferred_element_type=F32)
                               for c0 in range(0, bn, sw)], axis=1)
        y = z * lax.rsqrt(ssq * (1.0 / head_dim) + EPS) * gain_ref[...]
        reps = bn // cos_ref.shape[1]
        cosv = jnp.tile(cos_ref[...], (1, reps))
        sa = jnp.tile(sa_ref[...], (1, reps))
        sb = jnp.tile(sb_ref[...], (1, reps))
        y = y * cosv + pltpu.roll(y, bn - rot_half, 1) * sa + pltpu.roll(y, rot_half, 1) * sb
        o_ref[...] = y

    @pl.when(n >= n_norm_tiles)
    def _():
        o_ref[...] = z


def _inproj_qkv(h, w, col0, q_dim, kv_dim, gain_row, cos_t, sa_t, sb_t, head_dim, rot_half):
    m, d = h.shape
    ncols = q_dim + 2 * kv_dim
    bm, bn = min(ROW_TILE, m), kv_dim
    assert col0 % bn == 0 and q_dim % bn == 0 and bn % LANES == 0 and m % bm == 0
    off = col0 // bn
    n_norm_tiles = (q_dim + kv_dim) // bn
    tw = cos_t.shape[1]
    return pl.pallas_call(
        functools.partial(_inproj_qkv_kernel, n_norm_tiles=n_norm_tiles, head_dim=head_dim,
                          rot_half=rot_half),
        grid=(ncols // bn, m // bm),
        in_specs=[
            pl.BlockSpec((bm, d), lambda n, i: (i, 0)),
            pl.BlockSpec((d, bn), lambda n, i: (0, n + off)),
            pl.BlockSpec((1, bn), lambda n, i: (0, n)),
            pl.BlockSpec((bm, tw), lambda n, i: (i, 0)),
            pl.BlockSpec((bm, tw), lambda n, i: (i, 0)),
            pl.BlockSpec((bm, tw), lambda n, i: (i, 0)),
        ],
        out_specs=pl.BlockSpec((bm, bn), lambda n, i: (i, n)),
        out_shape=jax.ShapeDtypeStruct((m, ncols), F32),
        scratch_shapes=[pltpu.VMEM((d, bn), BF16)],
        compiler_params=_cparams(("arbitrary", "arbitrary")),
        name="inproj_qkv",
    )(h, w, gain_row, cos_t, sa_t, sb_t)


def _gmlp_rows(u_ref, va_ref, w_ref, b_ref, gain_ref, o_ref, vn_ref, r0, sub):
    groups, rows, _ = w_ref.shape
    gw = u_ref.shape[1] // groups
    rs = slice(r0, r0 + rows)
    ri = lax.broadcasted_iota(I32, (rows, rows), 0)
    ci = lax.broadcasted_iota(I32, (rows, rows), 1)
    mask = (ci <= ri) & ((ri // sub) == (ci // sub))
    vn = _rms(va_ref[rs, :].astype(F32), gain_ref[...])
    if vn_ref is not None:
        vn_ref[rs, :] = vn
    vb = vn.astype(BF16)
    for g in range(groups):
        sl = slice(g * gw, (g + 1) * gw)
        wg = jnp.where(mask, w_ref[g], 0.0).astype(BF16)
        s = jnp.dot(wg, vb[:, sl], preferred_element_type=F32) + b_ref[:, g:g + 1]
        o_ref[rs, sl] = (u_ref[rs, sl].astype(F32) * s).astype(o_ref.dtype)


def _gate_gmlp_kernel(h_ref, w_ref, u_ref, va_ref, wp_ref, ws_ref, bp_ref, bs_ref, gain_ref,
                      g_ref, oa_ref, vn_ref, wb_ref, *, steps_per_col, n_prompt_steps, n_side_steps,
                      sample_len):
    i = pl.program_id(1)
    s = pl.program_id(0) * steps_per_col + i
    chunk = wp_ref.shape[1]

    @pl.when(i == 0)
    def _():
        _cast_rows(w_ref, wb_ref, 256)

    def gate():
        z = jnp.dot(h_ref[...], wb_ref[...], preferred_element_type=F32)
        g_ref[...] = jax.nn.sigmoid(z).astype(g_ref.dtype)

    @pl.when(s < n_prompt_steps)
    def _():
        gate()
        for r0 in range(0, u_ref.shape[0], chunk):
            _gmlp_rows(u_ref, va_ref, wp_ref, bp_ref, gain_ref, oa_ref, None, r0, chunk)

    @pl.when((s >= n_prompt_steps) & (s < n_side_steps))
    def _():
        gate()
        for r0 in range(0, u_ref.shape[0], chunk):
            _gmlp_rows(u_ref, va_ref, ws_ref, bs_ref, gain_ref, oa_ref, vn_ref, r0, sample_len)

    @pl.when(s >= n_side_steps)
    def _():
        gate()


def _gate_gmlp(h, w, col0, ncols, ua, w_prompt, w_sample, b_prompt, b_sample, gain, n_prompt_rows,
               sample_len):
    m, d = h.shape
    d_a = ua.shape[1] // 2
    groups, chunk, _ = w_prompt.shape
    bn = min(COL_TILE, ncols)
    side = min(GMLP_ROWS, m - n_prompt_rows)
    assert col0 % bn == 0 and ncols % bn == 0
    assert n_prompt_rows % side == 0 and (m - n_prompt_rows) % side == 0 and side % chunk == 0
    nps, nss = n_prompt_rows // side, m // side
    bm = _host_row_tile(m, ncols // bn, nss, WIDE_ROW_TILE)
    off = col0 // bn
    spc = m // bm
    step = lambda n, i: n * spc + i
    side_blk = lambda n, i: jnp.minimum(step(n, i), nss - 1)
    full3 = lambda n, i: (0, 0, 0)
    full2 = lambda n, i: (0, 0)
    return pl.pallas_call(
        functools.partial(_gate_gmlp_kernel, steps_per_col=spc, n_prompt_steps=nps, n_side_steps=nss,
                          sample_len=sample_len),
        grid=(ncols // bn, spc),
        in_specs=[
            pl.BlockSpec((bm, d), lambda n, i: (i, 0)),
            pl.BlockSpec((d, bn), lambda n, i: (0, n + off)),
            pl.BlockSpec((side, d_a), lambda n, i: (side_blk(n, i), 0)),
            pl.BlockSpec((side, d_a), lambda n, i: (side_blk(n, i), 1)),
            pl.BlockSpec((groups, chunk, chunk), full3),
            pl.BlockSpec((groups, chunk, chunk), full3),
            pl.BlockSpec((chunk, groups), full2),
            pl.BlockSpec((chunk, groups), full2),
            pl.BlockSpec((1, d_a), full2),
        ],
        out_specs=[
            pl.BlockSpec((bm, bn), lambda n, i: (i, n)),
            pl.BlockSpec((side, d_a), lambda n, i: (side_blk(n, i), 0)),
            pl.BlockSpec((side, d_a), lambda n, i: (jnp.clip(step(n, i) - nps, 0, nss - nps - 1), 0)),
        ],
        out_shape=[
            jax.ShapeDtypeStruct((m, ncols), BF16),
            jax.ShapeDtypeStruct((m, d_a), BF16),
            jax.ShapeDtypeStruct((m - n_prompt_rows, d_a), F32),
        ],
        scratch_shapes=[pltpu.VMEM((d, bn), BF16)],
        compiler_params=_cparams(("arbitrary", "arbitrary"), FUSED_VMEM_LIMIT_BYTES),
        name="inproj_gate_gmlp",
    )(h, w, ua, ua, w_prompt, w_sample, b_prompt, b_sample, gain)


def _gelu_attn_kernel(h_ref, w_ref, bias_ref, q_ref, ka_ref, kb_ref, kc_ref, kd_ref, va_ref, vb_ref,
                      vc_ref, vd_ref, ck_ref, cv_ref, ua_ref, o_ref, wb_ref, s_ref, e_ref, *,
                      steps_per_col, n_prompt_steps, n_side_steps, tiles_per_kv, head_dim):
    i = pl.program_id(0) * steps_per_col + pl.program_id(1)

    @pl.when(pl.program_id(1) == 0)
    def _():
        _cast_rows(w_ref, wb_ref, 256)

    def project():
        z = jnp.dot(h_ref[...], wb_ref[...], preferred_element_type=F32)
        ua_ref[...] = jax.nn.gelu(z).astype(ua_ref.dtype)

    t = q_ref.shape[0] // 2
    kv_dim = kc_ref.shape[1]
    lkp = bias_ref.shape[1] // 2
    scale = head_dim ** -0.5
    nt = (((1,), (1,)), ((), ()))
    n_kv = kv_dim // head_dim
    tiles_of = lambda j: [j * tiles_per_kv + a for a in range(tiles_per_kv)]

    def run(halves):
        prepared = []
        for k_parts, v_parts, first_key_chunk in halves:
            k = jnp.concatenate(k_parts, axis=0)
            v = jnp.concatenate(v_parts, axis=0)
            chunk_bias = None
            if first_key_chunk is not None:
                col = lax.broadcasted_iota(I32, (1, 2 * lkp), 1)
                key_chunk = first_key_chunk + (col % lkp) // t
                chunk_bias = jnp.where(key_chunk >= 0, 0.0, -jnp.inf)
            prepared.append((k, v, chunk_bias))
        lk = prepared[0][0].shape[0]
        lower = lax.broadcasted_iota(I32, (lk, LANES), 1) < head_dim
        zpad = jnp.zeros((lkp - lk, LANES), F32)
        rr = lax.broadcasted_iota(I32, (2 * lkp, LANES), 0) < lkp
        rl = lax.broadcasted_iota(I32, (2 * lkp, LANES), 1) < head_dim
        ones_sel = (rr == rl).astype(BF16)

        def doubled(tile, rolled, x):
            lo_src, hi_src = (tile, rolled) if x == 0 else (rolled, tile)
            return jnp.concatenate([jnp.where(lower, lo_src, 0.0), zpad,
                                    jnp.where(lower, 0.0, hi_src), zpad], axis=0).astype(BF16)

        for h, (k, _, chunk_bias) in enumerate(prepared):
            rows_h = slice(h * t, (h + 1) * t)
            for b in range(kv_dim // LANES):
                kt = k[:, b * LANES:(b + 1) * LANES]
                kr = pltpu.roll(kt, head_dim, 1)
                for x in range(2):
                    j = 2 * b + x
                    q2 = jnp.concatenate([q_ref[rows_h, a * LANES:(a + 1) * LANES] for a in tiles_of(j)],
                                         axis=0)
                    q2 = (q2 * scale).astype(BF16)
                    rows = []
                    for a in tiles_of(j):
                        row = bias_ref[a:a + 1, :]
                        if chunk_bias is not None:
                            row = row + chunk_bias
                        rows.append(jnp.broadcast_to(row, (t, 2 * lkp)))
                    s_ref[h, j] = (lax.dot_general(q2, doubled(kt, kr, x), nt, preferred_element_type=F32)
                                   + jnp.concatenate(rows, axis=0))

        for h in range(len(prepared)):
            for j in range(n_kv):
                for hs in (slice(0, lkp), slice(lkp, 2 * lkp)):
                    sh = s_ref[h, j, :, hs]
                    e_ref[h, j, :, hs] = jnp.exp(sh - jnp.max(sh, axis=-1, keepdims=True)).astype(BF16)

        for h, (_, v, _) in enumerate(prepared):
            for b in range(kv_dim // LANES):
                vt = v[:, b * LANES:(b + 1) * LANES]
                vr = pltpu.roll(vt, head_dim, 1)
                for x in range(2):
                    j = 2 * b + x
                    rhs = jnp.concatenate([doubled(vt, vr, x), ones_sel], axis=1)
                    r = jnp.dot(e_ref[h, j], rhs, preferred_element_type=F32)
                    o = r[:, :LANES] / r[:, LANES:]
                    for n, a in enumerate(tiles_of(j)):
                        o_ref[h * t:(h + 1) * t, a * LANES:(a + 1) * LANES] = (
                            o[n * t:(n + 1) * t].astype(o_ref.dtype))

    @pl.when(i < n_prompt_steps)
    def _():
        project()
        ka, kb, kc, kd = ka_ref[...], kb_ref[...], kc_ref[...], kd_ref[...]
        va, vb, vc, vd = va_ref[...], vb_ref[...], vc_ref[...], vd_ref[...]
        run([([ka, kb, kc], [va, vb, vc], 2 * i - 2), ([kb, kc, kd], [vb, vc, vd], 2 * i - 1)])

    @pl.when((i >= n_prompt_steps) & (i < n_side_steps))
    def _():
        project()
        w = ck_ref.shape[0] // 2
        run([([ck_ref[:w, :], kc_ref[...]], [cv_ref[:w, :], vc_ref[...]], None),
             ([ck_ref[w:, :], kd_ref[...]], [cv_ref[w:, :], vd_ref[...]], None)])

    @pl.when(i >= n_side_steps)
    def _():
        project()


def _gelu_attention(h, w, col0, ncols, qkv, cache_k, cache_v, sinks, n_prompt_rows, q_dim, kv_dim,
                    head_dim, n_heads):
    m, d = h.shape
    t = CHUNK
    nc = n_prompt_rows // t
    n_kv_heads = kv_dim // head_dim
    window = cache_k.shape[0] // ((m - n_prompt_rows) // t)
    kcol, vcol = q_dim // kv_dim, q_dim // kv_dim + 1
    gqa = n_heads // n_kv_heads
    assert 2 * head_dim == LANES and gqa % 2 == 0 and kv_dim % LANES == 0

    lk = window + t
    lkp = -(-(lk + 1) // LANES) * LANES
    pair = sinks.reshape(n_heads // 2, 2, 1)
    bias = jnp.concatenate([jnp.zeros((n_heads // 2, 2, lk), F32), pair,
                            jnp.full((n_heads // 2, 2, lkp - lk - 1), -jnp.inf, F32)], axis=2)
    bias = bias.reshape(n_heads // 2, 2 * lkp)

    assert nc % 2 == 0 and (m // t - nc) % 2 == 0, "query chunks are processed in pairs"
    nps, nss = nc // 2, m // (2 * t)

    bn = min(COL_TILE, ncols)
    assert col0 % bn == 0 and ncols % bn == 0
    bm = _host_row_tile(m, ncols // bn, nss, ROW_TILE)
    spc = m // bm
    off = col0 // bn
    pair_of = lambda n, i: jnp.minimum(n * spc + i, nss - 1)

    def kv_spec(back, col):
        return pl.BlockSpec((t, kv_dim), lambda n, i: (jnp.maximum(2 * pair_of(n, i) + back, 0), col))

    cache_spec = pl.BlockSpec((2 * window, kv_dim), lambda n, i: (jnp.maximum(pair_of(n, i) - nps, 0), 0))
    scratch = (2, n_kv_heads, gqa // 2 * t, 2 * lkp)
    return pl.pallas_call(
        functools.partial(_gelu_attn_kernel, steps_per_col=spc, n_prompt_steps=nps, n_side_steps=nss,
                          tiles_per_kv=gqa // 2, head_dim=head_dim),
        grid=(ncols // bn, spc),
        in_specs=[
            pl.BlockSpec((bm, d), lambda n, i: (i, 0)),
            pl.BlockSpec((d, bn), lambda n, i: (0, n + off)),
            pl.BlockSpec((n_heads // 2, 2 * lkp), lambda n, i: (0, 0)),
            pl.BlockSpec((2 * t, q_dim), lambda n, i: (pair_of(n, i), 0)),
            kv_spec(-2, kcol), kv_spec(-1, kcol), kv_spec(0, kcol), kv_spec(1, kcol),
            kv_spec(-2, vcol), kv_spec(-1, vcol), kv_spec(0, vcol), kv_spec(1, vcol),
            cache_spec, cache_spec,
        ],
        out_specs=[
            pl.BlockSpec((bm, bn), lambda n, i: (i, n)),
            pl.BlockSpec((2 * t, q_dim), lambda n, i: (pair_of(n, i), 0)),
        ],
        out_shape=[
            jax.ShapeDtypeStruct((m, ncols), BF16),
            jax.ShapeDtypeStruct((m, q_dim), BF16),
        ],
        scratch_shapes=[pltpu.VMEM((d, bn), BF16), pltpu.VMEM(scratch, F32), pltpu.VMEM(scratch, BF16)],
        compiler_params=_cparams(("arbitrary", "arbitrary")),
        name="inproj_gelu_attention",
    )(h, w, bias, qkv, qkv, qkv, qkv, qkv, qkv, qkv, qkv, qkv, cache_k, cache_v)


def _merge_kernel(oa_ref, ob_ref, pa_ref, pb_ref, ga_ref, gb_ref, o_ref, pab_ref, pbb_ref):
    @pl.when(pl.program_id(1) == 0)
    def _():
        _cast_rows(pa_ref, pab_ref, 256)
        _cast_rows(pb_ref, pbb_ref, 256)

    a = jnp.dot(oa_ref[...], pab_ref[...], preferred_element_type=F32)
    b = jnp.dot(ob_ref[...], pbb_ref[...], preferred_element_type=F32)
    o_ref[...] = (ga_ref[...].astype(F32) * a + gb_ref[...].astype(F32) * b).astype(o_ref.dtype)


def _merge(o_a, o_b, p_a, p_b, gates):
    m, d_a = o_a.shape
    q_dim = o_b.shape[1]
    d = p_a.shape[1]
    bm, bn = min(ROW_TILE, m), min(COL_TILE, d)
    goff = d // bn
    return pl.pallas_call(
        _merge_kernel,
        grid=(d // bn, m // bm),
        in_specs=[
            pl.BlockSpec((bm, d_a), lambda n, i: (i, 0)),
            pl.BlockSpec((bm, q_dim), lambda n, i: (i, 0)),
            pl.BlockSpec((d_a, bn), lambda n, i: (0, n)),
            pl.BlockSpec((q_dim, bn), lambda n, i: (0, n)),
            pl.BlockSpec((bm, bn), lambda n, i: (i, n)),
            pl.BlockSpec((bm, bn), lambda n, i: (i, n + goff)),
        ],
        out_specs=pl.BlockSpec((bm, bn), lambda n, i: (i, n)),
        out_shape=jax.ShapeDtypeStruct((m, d), BF16),
        scratch_shapes=[pltpu.VMEM((d_a, bn), BF16), pltpu.VMEM((q_dim, bn), BF16)],
        compiler_params=_cparams(("arbitrary", "arbitrary")),
        name="merge_proj",
    )(o_a, o_b, p_a, p_b, gates, gates)


def _outproj_kernel(t_ref, w_ref, xp_ref, xs_ref, o_ref, wb_ref, *, n_prompt_tiles):
    i = pl.program_id(1)

    @pl.when(i == 0)
    def _():
        _cast_rows(w_ref, wb_ref, 256)

    z = jnp.dot(t_ref[...], wb_ref[...], preferred_element_type=F32)

    @pl.when(i < n_prompt_tiles)
    def _():
        o_ref[...] = xp_ref[...] + z

    @pl.when(i >= n_prompt_tiles)
    def _():
        o_ref[...] = xs_ref[...] + z


def _outproj(tm, w_out, xp, xs):
    m, d = tm.shape
    s, t = xp.shape[0], xs.shape[0]
    bm, bn = min(ROW_TILE, t), min(COL_TILE, d)
    npt = s // bm
    return pl.pallas_call(
        functools.partial(_outproj_kernel, n_prompt_tiles=npt),
        grid=(d // bn, m // bm),
        in_specs=[
            pl.BlockSpec((bm, d), lambda n, i: (i, 0)),
            pl.BlockSpec((d, bn), lambda n, i: (0, n)),
            pl.BlockSpec((bm, bn), lambda n, i: (jnp.minimum(i, npt - 1), n)),
            pl.BlockSpec((bm, bn), lambda n, i: (jnp.maximum(i - npt, 0), n)),
        ],
        out_specs=pl.BlockSpec((bm, bn), lambda n, i: (i, n)),
        out_shape=jax.ShapeDtypeStruct((m, d), F32),
        scratch_shapes=[pltpu.VMEM((d, bn), BF16)],
        compiler_params=_cparams(("arbitrary", "arbitrary")),
        name="out_proj",
    )(tm, w_out, xp, xs)


def _router_kernel(x_ref, g_ref, wr_ref, br_ref, route_ref, cnt_ref, hp_ref, wcat_ref, carry_ref, *,
                   n_groups, per_group):
    i = pl.program_id(0)

    @pl.when(i == 0)
    def _():
        w = wr_ref[...]
        hi = w.astype(BF16)
        wcat_ref[:, :LANES] = hi
        wcat_ref[:, LANES:] = (w - hi.astype(F32)).astype(BF16)
        carry_ref[...] = jnp.zeros_like(carry_ref)

    hn = _rms(x_ref[...], g_ref[...])
    hp_ref[...] = _pack_halves(hn)
    rows = hn.shape[0]
    hi = hn.astype(BF16)
    lo = (hn - hi.astype(F32)).astype(BF16)
    prod = jnp.dot(jnp.concatenate([hi, lo], axis=0), wcat_ref[...], preferred_element_type=F32)
    logits = prod[:rows, :LANES] + prod[:rows, LANES:] + prod[rows:, :LANES] + br_ref[...]
    lane = lax.broadcasted_iota(I32, logits.shape, 1)
    big = jnp.int32(LANES)

    lg = jnp.where(lane < n_groups, logits, -jnp.inf)
    mg = jnp.max(lg, axis=-1, keepdims=True)
    pg_sel = 1.0 / jnp.sum(jnp.exp(lg - mg), axis=-1, keepdims=True)
    gsel = jnp.min(jnp.where(lg == mg, lane, big), axis=-1, keepdims=True)

    first = n_groups + gsel * per_group
    emask = (lane >= first) & (lane < first + per_group)
    le = jnp.where(emask, logits, -jnp.inf)
    me = jnp.max(le, axis=-1, keepdims=True)
    ee = jnp.exp(le - me)
    pe = jnp.where(emask, ee / jnp.sum(ee, axis=-1, keepdims=True), -1.0)
    p1 = jnp.max(pe, axis=-1, keepdims=True)
    i1 = jnp.min(jnp.where(pe == p1, lane, big), axis=-1, keepdims=True)
    pe2 = jnp.where(lane == i1, -1.0, pe)
    p2 = jnp.max(pe2, axis=-1, keepdims=True)
    i2 = jnp.min(jnp.where(pe2 == p2, lane, big), axis=-1, keepdims=True)
    psum = p1 + p2
    w1 = p1 / psum * pg_sel
    w2 = p2 / psum * pg_sel
    e1 = i1 - n_groups
    e2 = i2 - n_groups

    oh1 = (lane == e1).astype(F32)
    oh2 = (lane == e2).astype(F32)
    ohs = oh1 + oh2
    ri = lax.broadcasted_iota(I32, (rows, rows), 0)
    ci = lax.broadcasted_iota(I32, (rows, rows), 1)
    below = (ci < ri).astype(BF16)
    before = jnp.dot(below, ohs.astype(BF16), preferred_element_type=F32) + carry_ref[...]
    r1 = jnp.sum(before * oh1, axis=-1, keepdims=True)
    r2 = jnp.sum(before * oh2, axis=-1, keepdims=True)
    carry_ref[...] = carry_ref[...] + jnp.sum(ohs, axis=0, keepdims=True)
    cnt_ref[...] = carry_ref[...]

    route = jnp.where(lane == 0, e1.astype(F32), 0.0)
    route = jnp.where(lane == 1, e2.astype(F32), route)
    route = jnp.where(lane == 2, w1, route)
    route = jnp.where(lane == 3, w2, route)
    route = jnp.where(lane == 4, r1, route)
    route = jnp.where(lane == 5, r2, route)
    route_ref[...] = route


def _router(x2, gain, wr, br, n_groups, per_group):
    m, d = x2.shape
    br_rows = _row_tile(ROUTE_ROWS, m)
    return pl.pallas_call(
        functools.partial(_router_kernel, n_groups=n_groups, per_group=per_group),
        grid=(m // br_rows,),
        in_specs=[
            pl.BlockSpec((br_rows, d), lambda i: (i, 0)),
            pl.BlockSpec((1, d), lambda i: (0, 0)),
            pl.BlockSpec((d, LANES), lambda i: (0, 0)),
            pl.BlockSpec((1, LANES), lambda i: (0, 0)),
        ],
        out_specs=[
            pl.BlockSpec((br_rows, LANES), lambda i: (i, 0)),
            pl.BlockSpec((1, LANES), lambda i: (0, 0)),
            pl.BlockSpec((br_rows, d // 2), lambda i: (i, 0)),
        ],
        out_shape=[
            jax.ShapeDtypeStruct((m, LANES), F32),
            jax.ShapeDtypeStruct((1, LANES), F32),
            jax.ShapeDtypeStruct((m, d // 2), U32),
        ],
        scratch_shapes=[pltpu.VMEM((d, 2 * LANES), BF16), pltpu.VMEM((1, LANES), F32)],
        compiler_params=_cparams(("arbitrary",)),
        name="norm2_router",
    )(x2, gain, wr, br)


def _pack_pair(first, second):
    hi = lax.bitcast_convert_type(first.astype(BF16).astype(F32), U32)
    lo = lax.bitcast_convert_type(second.astype(BF16).astype(F32), U32)
    return hi | (lo >> 16)


def _pack_halves(x):
    half = x.shape[1] // 2
    return _pack_pair(x[:, :half], x[:, half:])


def _unpack_halves(p):
    first = lax.bitcast_convert_type(p & jnp.uint32(0xFFFF0000), F32)
    second = lax.bitcast_convert_type(p << 16, F32)
    return first, second


def _dispatch_kernel(dest_ref, hp_ref, xs_ref, buf_ref, sem, *, n_tiles):
    i = pl.program_id(0)
    tm = hp_ref.shape[0]
    n_tokens = n_tiles * tm
    slot = i % 2

    def wait_rows(s):
        for k in range(TOP_K):
            pltpu.make_async_copy(buf_ref.at[s], xs_ref.at[pl.ds(0, tm)], sem.at[s]).wait()

    buf_ref[slot] = hp_ref[...]

    def start(r, c):
        for k in range(TOP_K):
            d = dest_ref[k * n_tokens + i * tm + r]
            pltpu.make_async_copy(buf_ref.at[slot, pl.ds(r, 1)], xs_ref.at[pl.ds(d, 1)],
                                  sem.at[slot]).start()
        return c

    lax.fori_loop(0, tm, start, 0, unroll=DMA_UNROLL)

    @pl.when(i > 0)
    def _():
        wait_rows(1 - slot)

    @pl.when(i == n_tiles - 1)
    def _():
        wait_rows(slot)


def _dispatch(dest_flat, hp, n_rows):
    m, half = hp.shape
    tm = _row_tile(DISPATCH_ROWS, m)
    return pl.pallas_call(
        functools.partial(_dispatch_kernel, n_tiles=m // tm),
        grid_spec=pltpu.PrefetchScalarGridSpec(
            num_scalar_prefetch=1,
            grid=(m // tm,),
            in_specs=[pl.BlockSpec((tm, half), lambda i, dest: (i, 0))],
            out_specs=pl.BlockSpec(memory_space=pl.ANY),
            scratch_shapes=[pltpu.VMEM((2, tm, half), U32), pltpu.SemaphoreType.DMA((2,))],
        ),
        out_shape=jax.ShapeDtypeStruct((n_rows, half), U32),
        compiler_params=_cparams(("arbitrary",)),
        name="moe_dispatch",
    )(dest_flat, hp)


def _moe_kernel(be_ref, nrows_ref, xs_ref, wg_ref, wu_ref, wd_ref, y_ref, x1_ref, x2_ref, pa_ref, pb_ref):
    i = pl.program_id(0)
    j = pl.program_id(1)
    rows, half = xs_ref.shape

    def partial_proj(x_ref):
        x = x_ref[...]
        return (jnp.dot(x, wg_ref[0].astype(BF16), preferred_element_type=F32),
                jnp.dot(x, wu_ref[0].astype(BF16), preferred_element_type=F32))

    @pl.when(j == 0)
    def _():
        live = lax.broadcasted_iota(I32, (rows, 1), 0) < nrows_ref[i]
        x1, x2 = _unpack_halves(xs_ref[...])
        x1_ref[...] = jnp.where(live, x1, 0.0).astype(BF16)
        x2_ref[...] = jnp.where(live, x2, 0.0).astype(BF16)
        pa_ref[...], pb_ref[...] = partial_proj(x1_ref)

    @pl.when(j == 1)
    def _():
        a, b = partial_proj(x2_ref)
        act = (jax.nn.silu(pa_ref[...] + a) * (pb_ref[...] + b)).astype(BF16)
        first = jnp.dot(act, wd_ref[0, :, :half].astype(BF16), preferred_element_type=F32)
        second = jnp.dot(act, wd_ref[0, :, half:].astype(BF16), preferred_element_type=F32)
        y_ref[...] = _pack_pair(first, second)


def _moe_experts(block_e, block_rows, n_occupied, xs, w_gate, w_up, w_down):
    half = xs.shape[1]
    d = 2 * half
    d_e = w_gate.shape[2]
    br = MOE_ROWS
    return pl.pallas_call(
        _moe_kernel,
        grid_spec=pltpu.PrefetchScalarGridSpec(
            num_scalar_prefetch=2,
            grid=(n_occupied, 2),
            in_specs=[
                pl.BlockSpec((br, half), lambda i, j, be, nr: (i, 0)),
                pl.BlockSpec((1, half, d_e), lambda i, j, be, nr: (be[i], j, 0)),
                pl.BlockSpec((1, half, d_e), lambda i, j, be, nr: (be[i], j, 0)),
                pl.BlockSpec((1, d_e, d), lambda i, j, be, nr: (be[i], 0, 0)),
            ],
            out_specs=pl.BlockSpec((br, half), lambda i, j, be, nr: (i, 0)),
            scratch_shapes=[pltpu.VMEM((br, half), BF16), pltpu.VMEM((br, half), BF16),
                            pltpu.VMEM((br, d_e), F32), pltpu.VMEM((br, d_e), F32)],
        ),
        out_shape=jax.ShapeDtypeStruct(xs.shape, U32),
        compiler_params=_cparams(("arbitrary", "arbitrary")),
        name="moe_experts",
    )(block_e, block_rows, xs, w_gate, w_up, w_down)


def _combine_kernel(dest_ref, x_ref, route_ref, y_ref, op_ref, os_ref, buf_ref, sem, *,
                    n_prompt_tiles, n_tiles):
    i = pl.program_id(0)
    tm = x_ref.shape[0]
    n_tokens = n_tiles * tm
    slot = i % 2

    def gather(tile, to_slot):
        def start(r, c):
            for k in range(TOP_K):
                d = dest_ref[k * n_tokens + tile * tm + r]
                pltpu.make_async_copy(y_ref.at[pl.ds(d, 1)], buf_ref.at[to_slot, k, pl.ds(r, 1)],
                                      sem.at[to_slot]).start()
            return c

        lax.fori_loop(0, tm, start, 0, unroll=DMA_UNROLL)

    @pl.when(i == 0)
    def _():
        gather(0, 0)

    @pl.when(i + 1 < n_tiles)
    def _():
        gather(i + 1, 1 - slot)

    for k in range(TOP_K):
        pltpu.make_async_copy(y_ref.at[pl.ds(0, tm)], buf_ref.at[slot, k], sem.at[slot]).wait()

    half = x_ref.shape[1] // 2
    first, second = x_ref[:, :half], x_ref[:, half:]
    for k in range(TOP_K):
        y1, y2 = _unpack_halves(buf_ref[slot, k])
        w = route_ref[:, 2 + k:3 + k]
        first = first + w * y1
        second = second + w * y2
    out = jnp.concatenate([first, second], axis=1)

    @pl.when(i < n_prompt_tiles)
    def _():
        op_ref[...] = out

    @pl.when(i >= n_prompt_tiles)
    def _():
        os_ref[...] = out


def _combine(dest_flat, x2, route, y, n_prompt_rows):
    m, d = x2.shape
    tm = _row_tile(COMBINE_ROWS, n_prompt_rows, m - n_prompt_rows)
    npt = n_prompt_rows // tm
    return pl.pallas_call(
        functools.partial(_combine_kernel, n_prompt_tiles=npt, n_tiles=m // tm),
        grid_spec=pltpu.PrefetchScalarGridSpec(
            num_scalar_prefetch=1,
            grid=(m // tm,),
            in_specs=[
                pl.BlockSpec((tm, d), lambda i, dest: (i, 0)),
                pl.BlockSpec((tm, LANES), lambda i, dest: (i, 0)),
                pl.BlockSpec(memory_space=pl.ANY),
            ],
            out_specs=[
                pl.BlockSpec((tm, d), lambda i, dest: (jnp.minimum(i, npt - 1), 0)),
                pl.BlockSpec((tm, d), lambda i, dest: (jnp.maximum(i - npt, 0), 0)),
            ],
            scratch_shapes=[pltpu.VMEM((2, TOP_K, tm, d // 2), U32), pltpu.SemaphoreType.DMA((2,))],
        ),
        out_shape=[
            jax.ShapeDtypeStruct((n_prompt_rows, d), F32),
            jax.ShapeDtypeStruct((m - n_prompt_rows, d), F32),
        ],
        compiler_params=_cparams(("arbitrary",)),
        name="moe_combine",
    )(dest_flat, x2, route, y)


def _rope_tables(positions, head_dim):
    rot_dim = head_dim // 4
    half = rot_dim // 2
    inv_freq = jnp.power(ROPE_THETA, -jnp.arange(half, dtype=F32) * 2.0 / rot_dim)
    ang = positions.astype(F32)[:, None] * inv_freq[None, :]
    cos, sin = lax.optimization_barrier((jnp.cos(ang), jnp.sin(ang)))
    m = positions.shape[0]
    zeros = lambda n: jnp.zeros((m, n), F32)
    cos_h = jnp.concatenate([cos, cos, jnp.ones((m, head_dim - rot_dim), F32)], axis=1)
    sa_h = jnp.concatenate([-sin, zeros(head_dim - half)], axis=1)
    sb_h = jnp.concatenate([zeros(half), sin, zeros(head_dim - rot_dim)], axis=1)
    reps = LANES // head_dim
    return tuple(jnp.tile(a, (1, reps)) for a in (cos_h, sa_h, sb_h)), half


def _layer(xp, xs, cache_k, cache_v, norm1_g, w_in, gmlp_norm_g, w_s, b_s, q_norm_g, k_norm_g,
           sinks, p_a, p_b, w_out, norm2_g, w_rg, b_rg, w_re, b_re, w_gate_e, w_up_e, w_down_e):
    s, d = xp.shape
    dec_rows = xs.shape[0]
    dec_batch = cache_k.shape[0]
    t = dec_rows // dec_batch
    m = s + dec_rows
    d_a = gmlp_norm_g.shape[0]
    head_dim = q_norm_g.shape[0]
    n_heads = sinks.shape[0]
    q_dim = n_heads * head_dim
    kv_dim = cache_k.shape[2] * cache_k.shape[3]
    n_groups, per_group = w_re.shape[1], w_re.shape[2]
    n_experts = n_groups * per_group

    h = _norm1(xp, xs, norm1_g[None, :])
    positions = jnp.concatenate(
        [jnp.arange(s, dtype=I32), jnp.tile(PAST_LEN + jnp.arange(t, dtype=I32), dec_batch)])
    (cos_t, sa_t, sb_t), rot_half = _rope_tables(positions, head_dim)
    gain_row = jnp.concatenate([jnp.tile(q_norm_g, n_heads), jnp.tile(k_norm_g, kv_dim // head_dim),
                                jnp.ones((kv_dim,), F32)])[None, :]
    qkv = _inproj_qkv(h, w_in, 2 * d_a, q_dim, kv_dim, gain_row, cos_t, sa_t, sb_t, head_dim, rot_half)

    ua, o_b = _gelu_attention(h, w_in, 0, 2 * d_a, qkv, cache_k.reshape(-1, kv_dim),
                              cache_v.reshape(-1, kv_dim), sinks[None, :], s, q_dim, kv_dim, head_dim,
                              n_heads)
    rows = w_s.shape[1]
    reps = rows // t
    w_sample = jnp.tile(w_s[:, :t, :t], (1, reps, reps))
    b_sample = jnp.tile(b_s[:, :t], (1, reps))
    gates, o_a, vn_s = _gate_gmlp(h, w_in, 2 * d_a + q_dim + 2 * kv_dim, 2 * d, ua, w_s, w_sample,
                                  b_s.T, b_sample.T, gmlp_norm_g[None, :], s, t)

    merged = _merge(o_a, o_b, p_a, p_b, gates)
    x2 = _outproj(merged, w_out, xp, xs)

    pad = LANES - n_groups - n_experts
    wr = jnp.concatenate([w_rg, w_re.reshape(d, n_experts), jnp.zeros((d, pad), F32)], axis=1)
    br = jnp.concatenate([b_rg, b_re.reshape(n_experts), jnp.zeros((pad,), F32)])[None, :]
    route, counts, h2_packed = _router(x2, norm2_g[None, :], wr, br, n_groups, per_group)
    e_idx = route[:, 0:TOP_K].T.astype(I32)
    rank = route[:, 4:4 + TOP_K].T.astype(I32)
    counts = counts[0, :n_experts].astype(I32)

    blk = MOE_ROWS
    n_blocks = -(-(m * TOP_K) // blk) + n_experts
    padded = ((counts + blk - 1) // blk) * blk
    pend = jnp.cumsum(padded)
    pstart = pend - padded
    hit = e_idx[:, :, None] == jnp.arange(n_experts, dtype=I32)
    dest = (jnp.sum(jnp.where(hit, pstart, 0), axis=-1) + rank).reshape(-1).astype(I32)
    n_occupied = (pend[-1] // blk).astype(I32)
    first_row = jnp.arange(n_blocks, dtype=I32) * blk
    block_e = jnp.minimum(jnp.sum((pend[None, :] <= first_row[:, None]).astype(I32), axis=1), n_experts - 1)
    block_rows = jnp.clip(counts[block_e] - (first_row - pstart[block_e]), 0, blk).astype(I32)

    xs_sorted = _dispatch(dest, h2_packed, n_blocks * blk)
    y_sorted = _moe_experts(block_e, block_rows, n_occupied, xs_sorted, w_gate_e, w_up_e, w_down_e)
    yp, ys = _combine(dest, x2, route, y_sorted, s)
    return yp, ys, qkv, vn_s


def kernel(x_prompt, x_sample, cache_k_win, cache_v_win, norm1_g, w_in, gmlp_norm_g, w_s, b_s,
           q_norm_g, k_norm_g, sinks, p_a, p_b, w_out, norm2_g, w_rg, b_rg, w_re, b_re,
           w_gate_e, w_up_e, w_down_e):
    depth = norm1_g.shape[0]
    assert depth == 1, "weights of one layer are expected"
    batch, s, d = x_prompt.shape
    assert batch == 1
    dec_batch, t, _ = x_sample.shape
    head_dim = q_norm_g.shape[-1]
    n_kv = cache_k_win.shape[3]
    q_dim = sinks.shape[-1] * head_dim
    kv_dim = n_kv * head_dim
    keep = min(cache_k_win.shape[2], s)

    l = 0
    yp, ys, qkv, vn_s = _layer(
        x_prompt.reshape(s, d), x_sample.reshape(dec_batch * t, d), cache_k_win[l], cache_v_win[l],
        norm1_g[l], w_in[l], gmlp_norm_g[l], w_s[l], b_s[l], q_norm_g[l], k_norm_g[l], sinks[l],
        p_a[l], p_b[l], w_out[l], norm2_g[l], w_rg[l], b_rg[l], w_re[l], b_re[l],
        w_gate_e[l], w_up_e[l], w_down_e[l])

    k_all = qkv[:, q_dim:q_dim + kv_dim]
    v_all = qkv[:, q_dim + kv_dim:]
    k_win_p = k_all[s - keep:s].reshape(1, batch, keep, n_kv, head_dim)
    v_win_p = v_all[s - keep:s].reshape(1, batch, keep, n_kv, head_dim)
    k_new_s = k_all[s:].reshape(1, dec_batch, t, n_kv, head_dim)
    v_new_s = v_all[s:].reshape(1, dec_batch, t, n_kv, head_dim)
    gv_s = vn_s.reshape(1, dec_batch, t, -1)
    return (yp.reshape(batch, s, d), ys.reshape(dec_batch, t, d), k_win_p, v_win_p, k_new_s, v_new_s, gv_s)
```

```python
import functools

import jax
import jax.numpy as jnp
from jax import lax
from jax.experimental import pallas as pl
from jax.experimental.pallas import tpu as pltpu

F32 = jnp.float32
BF16 = jnp.bfloat16
I32 = jnp.int32
U32 = jnp.uint32

EPS = 1e-6
PAST_LEN = 1024
CHUNK = 64
ROPE_THETA = 500000.0
TOP_K = 2
LANES = 128

VMEM_LIMIT_BYTES = 56 * 1024 * 1024
FUSED_VMEM_LIMIT_BYTES = 58 * 1024 * 1024

ROW_TILE = 1024
WIDE_ROW_TILE = 1536
COL_TILE = 512
SEG_WIDTH = 256
GMLP_ROWS = 128
NORM_ROWS = 512
ROUTE_ROWS = 512
MOE_ROWS = 320
COMBINE_ROWS = 256
DISPATCH_ROWS = 512
DMA_UNROLL = 8


def _cparams(sem, vmem_limit_bytes=VMEM_LIMIT_BYTES):
    return pltpu.CompilerParams(dimension_semantics=sem, vmem_limit_bytes=vmem_limit_bytes)


def _cast_rows(src_ref, dst_ref, rows):
    n = src_ref.shape[0] // rows

    def body(r, c):
        sl = pl.ds(pl.multiple_of(r * rows, rows), rows)
        dst_ref[sl, :] = src_ref[sl, :].astype(dst_ref.dtype)
        return c

    lax.fori_loop(0, n, body, 0)


def _row_tile(preferred, *row_counts):
    tile = preferred
    while any(n % tile for n in row_counts):
        tile //= 2
    assert tile >= 8
    return tile


def _rms(x, gain):
    ms = jnp.mean(x * x, axis=-1, keepdims=True)
    return x * lax.rsqrt(ms + EPS) * gain


def _norm1_kernel(xp_ref, xs_ref, g_ref, h_ref, *, n_prompt_blocks):
    i = pl.program_id(0)

    @pl.when(i < n_prompt_blocks)
    def _():
        h_ref[...] = _rms(xp_ref[...], g_ref[...]).astype(h_ref.dtype)

    @pl.when(i >= n_prompt_blocks)
    def _():
        h_ref[...] = _rms(xs_ref[...], g_ref[...]).astype(h_ref.dtype)


def _norm1(xp, xs, gain):
    s, d = xp.shape
    t = xs.shape[0]
    br = _row_tile(NORM_ROWS, s, t)
    nbp, nbs = s // br, t // br
    return pl.pallas_call(
        functools.partial(_norm1_kernel, n_prompt_blocks=nbp),
        grid=(nbp + nbs,),
        in_specs=[
            pl.BlockSpec((br, d), lambda i: (jnp.minimum(i, nbp - 1), 0)),
            pl.BlockSpec((br, d), lambda i: (jnp.maximum(i - nbp, 0), 0)),
            pl.BlockSpec((1, d), lambda i: (0, 0)),
        ],
        out_specs=pl.BlockSpec((br, d), lambda i: (i, 0)),
        out_shape=jax.ShapeDtypeStruct((s + t, d), BF16),
        compiler_params=_cparams(("arbitrary",)),
        name="norm1",
    )(xp, xs, gain)


def _host_row_tile(m, n_col_tiles, n_side_steps, preferred):
    bm = preferred if m % preferred == 0 else min(ROW_TILE, m)
    while n_col_tiles * (m // bm) < n_side_steps:
        assert bm % 16 == 0
        bm //= 2
    assert m % bm == 0
    return bm


def _inproj_qkv_kernel(h_ref, w_ref, gain_ref, cos_ref, sa_ref, sb_ref, o_ref, wb_ref, *,
                       n_norm_tiles, head_dim, rot_half):
    n = pl.program_id(0)

    @pl.when(pl.program_id(1) == 0)
    def _():
        _cast_rows(w_ref, wb_ref, 256)

    z = jnp.dot(h_ref[...], wb_ref[...], preferred_element_type=F32)
    bn = z.shape[1]

    @pl.when(n < n_norm_tiles)
    def _():
        sw = min(SEG_WIDTH, bn)
        r = lax.broadcasted_iota(I32, (sw, sw), 0) // head_dim
        c = lax.broadcasted_iota(I32, (sw, sw), 1) // head_dim
        seg = (r == c).astype(BF16)
        zz = (z * z).astype(BF16)
        ssq = jnp.concatenate([jnp.dot(zz[:, c0:c0 + sw], seg, preferred_element_type=F32)
                               for c0 in range(0, bn, sw)], axis=1)
        y = z * lax.rsqrt(ssq * (1.0 / head_dim) + EPS) * gain_ref[...]
        reps = bn // cos_ref.shape[1]
        cosv = jnp.tile(cos_ref[...], (1, reps))
        sa = jnp.tile(sa_ref[...], (1, reps))
        sb = jnp.tile(sb_ref[...], (1, reps))
        y = y * cosv + pltpu.roll(y, bn - rot_half, 1) * sa + pltpu.roll(y, rot_half, 1) * sb
        o_ref[...] = y

    @pl.when(n >= n_norm_tiles)
    def _():
        o_ref[...] = z


def _inproj_qkv(h, w, col0, q_dim, kv_dim, gain_row, cos_t, sa_t, sb_t, head_dim, rot_half):
    m, d = h.shape
    ncols = q_dim + 2 * kv_dim
    bm, bn = min(ROW_TILE, m), kv_dim
    assert col0 % bn == 0 and q_dim % bn == 0 and bn % LANES == 0 and m % bm == 0
    off = col0 // bn
    n_norm_tiles = (q_dim + kv_dim) // bn
    tw = cos_t.shape[1]
    return pl.pallas_call(
        functools.partial(_inproj_qkv_kernel, n_norm_tiles=n_norm_tiles, head_dim=head_dim,
                          rot_half=rot_half),
        grid=(ncols // bn, m // bm),
        in_specs=[
            pl.BlockSpec((bm, d), lambda n, i: (i, 0)),
            pl.BlockSpec((d, bn), lambda n, i: (0, n + off)),
            pl.BlockSpec((1, bn), lambda n, i: (0, n)),
            pl.BlockSpec((bm, tw), lambda n, i: (i, 0)),
            pl.BlockSpec((bm, tw), lambda n, i: (i, 0)),
            pl.BlockSpec((bm, tw), lambda n, i: (i, 0)),
        ],
        out_specs=pl.BlockSpec((bm, bn), lambda n, i: (i, n)),
        out_shape=jax.ShapeDtypeStruct((m, ncols), F32),
        scratch_shapes=[pltpu.VMEM((d, bn), BF16)],
        compiler_params=_cparams(("arbitrary", "arbitrary")),
        name="inproj_qkv",
    )(h, w, gain_row, cos_t, sa_t, sb_t)


def _gmlp_rows(u_ref, va_ref, w_ref, b_ref, gain_ref, o_ref, vn_ref, r0, sub):
    groups, rows, _ = w_ref.shape
    gw = u_ref.shape[1] // groups
    rs = slice(r0, r0 + rows)
    ri = lax.broadcasted_iota(I32, (rows, rows), 0)
    ci = lax.broadcasted_iota(I32, (rows, rows), 1)
    mask = (ci <= ri) & ((ri // sub) == (ci // sub))
    vn = _rms(va_ref[rs, :].astype(F32), gain_ref[...])
    if vn_ref is not None:
        vn_ref[rs, :] = vn
    vb = vn.astype(BF16)
    for g in range(groups):
        sl = slice(g * gw, (g + 1) * gw)
        wg = jnp.where(mask, w_ref[g], 0.0).astype(BF16)
        s = jnp.dot(wg, vb[:, sl], preferred_element_type=F32) + b_ref[:, g:g + 1]
        o_ref[rs, sl] = (u_ref[rs, sl].astype(F32) * s).astype(o_ref.dtype)


def _gate_gmlp_kernel(h_ref, w_ref, u_ref, va_ref, wp_ref, ws_ref, bp_ref, bs_ref, gain_ref,
                      g_ref, oa_ref, vn_ref, wb_ref, *, steps_per_col, n_prompt_steps, n_side_steps,
                      sample_len):
    i = pl.program_id(1)
    s = pl.program_id(0) * steps_per_col + i
    chunk = wp_ref.shape[1]

    @pl.when(i == 0)
    def _():
        _cast_rows(w_ref, wb_ref, 256)

    def gate():
        z = jnp.dot(h_ref[...], wb_ref[...], preferred_element_type=F32)
        g_ref[...] = jax.nn.sigmoid(z).astype(g_ref.dtype)

    @pl.when(s < n_prompt_steps)
    def _():
        gate()
        for r0 in range(0, u_ref.shape[0], chunk):
            _gmlp_rows(u_ref, va_ref, wp_ref, bp_ref, gain_ref, oa_ref, None, r0, chunk)

    @pl.when((s >= n_prompt_steps) & (s < n_side_steps))
    def _():
        gate()
        for r0 in range(0, u_ref.shape[0], chunk):
            _gmlp_rows(u_ref, va_ref, ws_ref, bs_ref, gain_ref, oa_ref, vn_ref, r0, sample_len)

    @pl.when(s >= n_side_steps)
    def _():
        gate()


def _gate_gmlp(h, w, col0, ncols, ua, w_prompt, w_sample, b_prompt, b_sample, gain, n_prompt_rows,
               sample_len):
    m, d = h.shape
    d_a = ua.shape[1] // 2
    groups, chunk, _ = w_prompt.shape
    bn = min(COL_TILE, ncols)
    side = min(GMLP_ROWS, m - n_prompt_rows)
    assert col0 % bn == 0 and ncols % bn == 0
    assert n_prompt_rows % side == 0 and (m - n_prompt_rows) % side == 0 and side % chunk == 0
    nps, nss = n_prompt_rows // side, m // side
    bm = _host_row_tile(m, ncols // bn, nss, WIDE_ROW_TILE)
    off = col0 // bn
    spc = m // bm
    step = lambda n, i: n * spc + i
    side_blk = lambda n, i: jnp.minimum(step(n, i), nss - 1)
    full3 = lambda n, i: (0, 0, 0)
    full2 = lambda n, i: (0, 0)
    return pl.pallas_call(
        functools.partial(_gate_gmlp_kernel, steps_per_col=spc, n_prompt_steps=nps, n_side_steps=nss,
                          sample_len=sample_len),
        grid=(ncols // bn, spc),
        in_specs=[
            pl.BlockSpec((bm, d), lambda n, i: (i, 0)),
            pl.BlockSpec((d, bn), lambda n, i: (0, n + off)),
            pl.BlockSpec((side, d_a), lambda n, i: (side_blk(n, i), 0)),
            pl.BlockSpec((side, d_a), lambda n, i: (side_blk(n, i), 1)),
            pl.BlockSpec((groups, chunk, chunk), full3),
            pl.BlockSpec((groups, chunk, chunk), full3),
            pl.BlockSpec((chunk, groups), full2),
            pl.BlockSpec((chunk, groups), full2),
            pl.BlockSpec((1, d_a), full2),
        ],
        out_specs=[
            pl.BlockSpec((bm, bn), lambda n, i: (i, n)),
            pl.BlockSpec((side, d_a), lambda n, i: (side_blk(n, i), 0)),
            pl.BlockSpec((side, d_a), lambda n, i: (jnp.clip(step(n, i) - nps, 0, nss - nps - 1), 0)),
        ],
        out_shape=[
            jax.ShapeDtypeStruct((m, ncols), BF16),
            jax.ShapeDtypeStruct((m, d_a), BF16),
            jax.ShapeDtypeStruct((m - n_prompt_rows, d_a), F32),
        ],
        scratch_shapes=[pltpu.VMEM((d, bn), BF16)],
        compiler_params=_cparams(("arbitrary", "arbitrary"), FUSED_VMEM_LIMIT_BYTES),
        name="inproj_gate_gmlp",
    )(h, w, ua, ua, w_prompt, w_sample, b_prompt, b_sample, gain)


def _gelu_attn_kernel(h_ref, w_ref, bias_ref, q_ref, ka_ref, kb_ref, kc_ref, kd_ref, va_ref, vb_ref,
                      vc_ref, vd_ref, ck_ref, cv_ref, ua_ref, o_ref, wb_ref, s_ref, e_ref, *,
                      steps_per_col, n_prompt_steps, n_side_steps, tiles_per_kv, head_dim):
    i = pl.program_id(0) * steps_per_col + pl.program_id(1)

    @pl.when(pl.program_id(1) == 0)
    def _():
        _cast_rows(w_ref, wb_ref, 256)

    def project():
        z = jnp.dot(h_ref[...], wb_ref[...], preferred_element_type=F32)
        ua_ref[...] = jax.nn.gelu(z).astype(ua_ref.dtype)

    t = q_ref.shape[0] // 2
    kv_dim = kc_ref.shape[1]
    lkp = bias_ref.shape[1] // 2
    scale = head_dim ** -0.5
    nt = (((1,), (1,)), ((), ()))
    n_kv = kv_dim // head_dim
    tiles_of = lambda j: [j * tiles_per_kv + a for a in range(tiles_per_kv)]

    def run(halves):
        prepared = []
        for k_parts, v_parts, first_key_chunk in halves:
            k = jnp.concatenate(k_parts, axis=0)
            v = jnp.concatenate(v_parts, axis=0)
            chunk_bias = None
            if first_key_chunk is not None:
                col = lax.broadcasted_iota(I32, (1, 2 * lkp), 1)
                key_chunk = first_key_chunk + (col % lkp) // t
                chunk_bias = jnp.where(key_chunk >= 0, 0.0, -jnp.inf)
            prepared.append((k, v, chunk_bias))
        lk = prepared[0][0].shape[0]
        lower = lax.broadcasted_iota(I32, (lk, LANES), 1) < head_dim
        zpad = jnp.zeros((lkp - lk, LANES), F32)
        rr = lax.broadcasted_iota(I32, (2 * lkp, LANES), 0) < lkp
        rl = lax.broadcasted_iota(I32, (2 * lkp, LANES), 1) < head_dim
        ones_sel = (rr == rl).astype(BF16)

        def doubled(tile, rolled, x):
            lo_src, hi_src = (tile, rolled) if x == 0 else (rolled, tile)
            return jnp.concatenate([jnp.where(lower, lo_src, 0.0), zpad,
                                    jnp.where(lower, 0.0, hi_src), zpad], axis=0).astype(BF16)

        for h, (k, _, chunk_bias) in enumerate(prepared):
            rows_h = slice(h * t, (h + 1) * t)
            for b in range(kv_dim // LANES):
                kt = k[:, b * LANES:(b + 1) * LANES]
                kr = pltpu.roll(kt, head_dim, 1)
                for x in range(2):
                    j = 2 * b + x
                    q2 = jnp.concatenate([q_ref[rows_h, a * LANES:(a + 1) * LANES] for a in tiles_of(j)],
                                         axis=0)
                    q2 = (q2 * scale).astype(BF16)
                    rows = []
                    for a in tiles_of(j):
                        row = bias_ref[a:a + 1, :]
                        if chunk_bias is not None:
                            row = row + chunk_bias
                        rows.append(jnp.broadcast_to(row, (t, 2 * lkp)))
                    s_ref[h, j] = (lax.dot_general(q2, doubled(kt, kr, x), nt, preferred_element_type=F32)
                                   + jnp.concatenate(rows, axis=0))

        for h in range(len(prepared)):
            for j in range(n_kv):
                for hs in (slice(0, lkp), slice(lkp, 2 * lkp)):
                    sh = s_ref[h, j, :, hs]
                    e_ref[h, j, :, hs] = jnp.exp(sh - jnp.max(sh, axis=-1, keepdims=True)).astype(BF16)

        for h, (_, v, _) in enumerate(prepared):
            for b in range(kv_dim // LANES):
                vt = v[:, b * LANES:(b + 1) * LANES]
                vr = pltpu.roll(vt, head_dim, 1)
                for x in range(2):
                    j = 2 * b + x
                    rhs = jnp.concatenate([doubled(vt, vr, x), ones_sel], axis=1)
                    r = jnp.dot(e_ref[h, j], rhs, preferred_element_type=F32)
                    o = r[:, :LANES] / r[:, LANES:]
                    for n, a in enumerate(tiles_of(j)):
                        o_ref[h * t:(h + 1) * t, a * LANES:(a + 1) * LANES] = (
                            o[n * t:(n + 1) * t].astype(o_ref.dtype))

    @pl.when(i < n_prompt_steps)
    def _():
        project()
        ka, kb, kc, kd = ka_ref[...], kb_ref[...], kc_ref[...], kd_ref[...]
        va, vb, vc, vd = va_ref[...], vb_ref[...], vc_ref[...], vd_ref[...]
        run([([ka, kb, kc], [va, vb, vc], 2 * i - 2), ([kb, kc, kd], [vb, vc, vd], 2 * i - 1)])

    @pl.when((i >= n_prompt_steps) & (i < n_side_steps))
    def _():
        project()
        w = ck_ref.shape[0] // 2
        run([([ck_ref[:w, :], kc_ref[...]], [cv_ref[:w, :], vc_ref[...]], None),
             ([ck_ref[w:, :], kd_ref[...]], [cv_ref[w:, :], vd_ref[...]], None)])

    @pl.when(i >= n_side_steps)
    def _():
        project()


def _gelu_attention(h, w, col0, ncols, qkv, cache_k, cache_v, sinks, n_prompt_rows, q_dim, kv_dim,
                    head_dim, n_heads):
    m, d = h.shape
    t = CHUNK
    nc = n_prompt_rows // t
    n_kv_heads = kv_dim // head_dim
    window = cache_k.shape[0] // ((m - n_prompt_rows) // t)
    kcol, vcol = q_dim // kv_dim, q_dim // kv_dim + 1
    gqa = n_heads // n_kv_heads
    assert 2 * head_dim == LANES and gqa % 2 == 0 and kv_dim % LANES == 0

    lk = window + t
    lkp = -(-(lk + 1) // LANES) * LANES
    pair = sinks.reshape(n_heads // 2, 2, 1)
    bias = jnp.concatenate([jnp.zeros((n_heads // 2, 2, lk), F32), pair,
                            jnp.full((n_heads // 2, 2, lkp - lk - 1), -jnp.inf, F32)], axis=2)
    bias = bias.reshape(n_heads // 2, 2 * lkp)

    assert nc % 2 == 0 and (m // t - nc) % 2 == 0, "query chunks are processed in pairs"
    nps, nss = nc // 2, m // (2 * t)

    bn = min(COL_TILE, ncols)
    assert col0 % bn == 0 and ncols % bn == 0
    bm = _host_row_tile(m, ncols // bn, nss, ROW_TILE)
    spc = m // bm
    off = col0 // bn
    pair_of = lambda n, i: jnp.minimum(n * spc + i, nss - 1)

    def kv_spec(back, col):
        return pl.BlockSpec((t, kv_dim), lambda n, i: (jnp.maximum(2 * pair_of(n, i) + back, 0), col))

    cache_spec = pl.BlockSpec((2 * window, kv_dim), lambda n, i: (jnp.maximum(pair_of(n, i) - nps, 0), 0))
    scratch = (2, n_kv_heads, gqa // 2 * t, 2 * lkp)
    return pl.pallas_call(
        functools.partial(_gelu_attn_kernel, steps_per_col=spc, n_prompt_steps=nps, n_side_steps=nss,
                          tiles_per_kv=gqa // 2, head_dim=head_dim),
        grid=(ncols // bn, spc),
        in_specs=[
            pl.BlockSpec((bm, d), lambda n, i: (i, 0)),
            pl.BlockSpec((d, bn), lambda n, i: (0, n + off)),
            pl.BlockSpec((n_heads // 2, 2 * lkp), lambda n, i: (0, 0)),
            pl.BlockSpec((2 * t, q_dim), lambda n, i: (pair_of(n, i), 0)),
            kv_spec(-2, kcol), kv_spec(-1, kcol), kv_spec(0, kcol), kv_spec(1, kcol),
            kv_spec(-2, vcol), kv_spec(-1, vcol), kv_spec(0, vcol), kv_spec(1, vcol),
            cache_spec, cache_spec,
        ],
        out_specs=[
            pl.BlockSpec((bm, bn), lambda n, i: (i, n)),
            pl.BlockSpec((2 * t, q_dim), lambda n, i: (pair_of(n, i), 0)),
        ],
        out_shape=[
            jax.ShapeDtypeStruct((m, ncols), BF16),
            jax.ShapeDtypeStruct((m, q_dim), BF16),
        ],
        scratch_shapes=[pltpu.VMEM((d, bn), BF16), pltpu.VMEM(scratch, F32), pltpu.VMEM(scratch, BF16)],
        compiler_params=_cparams(("arbitrary", "arbitrary")),
        name="inproj_gelu_attention",
    )(h, w, bias, qkv, qkv, qkv, qkv, qkv, qkv, qkv, qkv, qkv, cache_k, cache_v)


def _merge_kernel(oa_ref, ob_ref, pa_ref, pb_ref, ga_ref, gb_ref, o_ref, pab_ref, pbb_ref):
    @pl.when(pl.program_id(1) == 0)
    def _():
        _cast_rows(pa_ref, pab_ref, 256)
        _cast_rows(pb_ref, pbb_ref, 256)

    a = jnp.dot(oa_ref[...], pab_ref[...], preferred_element_type=F32)
    b = jnp.dot(ob_ref[...], pbb_ref[...], preferred_element_type=F32)
    o_ref[...] = (ga_ref[...].astype(F32) * a + gb_ref[...].astype(F32) * b).astype(o_ref.dtype)


def _merge(o_a, o_b, p_a, p_b, gates):
    m, d_a = o_a.shape
    q_dim = o_b.shape[1]
    d = p_a.shape[1]
    bm, bn = min(ROW_TILE, m), min(COL_TILE, d)
    goff = d // bn
    return pl.pallas_call(
        _merge_kernel,
        grid=(d // bn, m // bm),
        in_specs=[
            pl.BlockSpec((bm, d_a), lambda n, i: (i, 0)),
            pl.BlockSpec((bm, q_dim), lambda n, i: (i, 0)),
            pl.BlockSpec((d_a, bn), lambda n, i: (0, n)),
            pl.BlockSpec((q_dim, bn), lambda n, i: (0, n)),
            pl.BlockSpec((bm, bn), lambda n, i: (i, n)),
            pl.BlockSpec((bm, bn), lambda n, i: (i, n + goff)),
        ],
        out_specs=pl.BlockSpec((bm, bn), lambda n, i: (i, n)),
        out_shape=jax.ShapeDtypeStruct((m, d), BF16),
        scratch_shapes=[pltpu.VMEM((d_a, bn), BF16), pltpu.VMEM((q_dim, bn), BF16)],
        compiler_params=_cparams(("arbitrary", "arbitrary")),
        name="merge_proj",
    )(o_a, o_b, p_a, p_b, gates, gates)


def _outproj_kernel(t_ref, w_ref, xp_ref, xs_ref, o_ref, wb_ref, *, n_prompt_tiles):
    i = pl.program_id(1)

    @pl.when(i == 0)
    def _():
        _cast_rows(w_ref, wb_ref, 256)

    z = jnp.dot(t_ref[...], wb_ref[...], preferred_element_type=F32)

    @pl.when(i < n_prompt_tiles)
    def _():
        o_ref[...] = xp_ref[...] + z

    @pl.when(i >= n_prompt_tiles)
    def _():
        o_ref[...] = xs_ref[...] + z


def _outproj(tm, w_out, xp, xs):
    m, d = tm.shape
    s, t = xp.shape[0], xs.shape[0]
    bm, bn = min(ROW_TILE, t), min(COL_TILE, d)
    npt = s // bm
    return pl.pallas_call(
        functools.partial(_outproj_kernel, n_prompt_tiles=npt),
        grid=(d // bn, m // bm),
        in_specs=[
            pl.BlockSpec((bm, d), lambda n, i: (i, 0)),
            pl.BlockSpec((d, bn), lambda n, i: (0, n)),
            pl.BlockSpec((bm, bn), lambda n, i: (jnp.minimum(i, npt - 1), n)),
            pl.BlockSpec((bm, bn), lambda n, i: (jnp.maximum(i - npt, 0), n)),
        ],
        out_specs=pl.BlockSpec((bm, bn), lambda n, i: (i, n)),
        out_shape=jax.ShapeDtypeStruct((m, d), F32),
        scratch_shapes=[pltpu.VMEM((d, bn), BF16)],
        compiler_params=_cparams(("arbitrary", "arbitrary")),
        name="out_proj",
    )(tm, w_out, xp, xs)


def _router_kernel(x_ref, g_ref, wr_ref, br_ref, route_ref, cnt_ref, hp_ref, wcat_ref, carry_ref, *,
                   n_groups, per_group):
    i = pl.program_id(0)

    @pl.when(i == 0)
    def _():
        w = wr_ref[...]
        hi = w.astype(BF16)
        wcat_ref[:, :LANES] = hi
        wcat_ref[:, LANES:] = (w - hi.astype(F32)).astype(BF16)
        carry_ref[...] = jnp.zeros_like(carry_ref)

    hn = _rms(x_ref[...], g_ref[...])
    hp_ref[...] = _pack_halves(hn)
    rows = hn.shape[0]
    hi = hn.astype(BF16)
    lo = (hn - hi.astype(F32)).astype(BF16)
    prod = jnp.dot(jnp.concatenate([hi, lo], axis=0), wcat_ref[...], preferred_element_type=F32)
    logits = prod[:rows, :LANES] + prod[:rows, LANES:] + prod[rows:, :LANES] + br_ref[...]
    lane = lax.broadcasted_iota(I32, logits.shape, 1)
    big = jnp.int32(LANES)

    lg = jnp.where(lane < n_groups, logits, -jnp.inf)
    mg = jnp.max(lg, axis=-1, keepdims=True)
    pg_sel = 1.0 / jnp.sum(jnp.exp(lg - mg), axis=-1, keepdims=True)
    gsel = jnp.min(jnp.where(lg == mg, lane, big), axis=-1, keepdims=True)

    first = n_groups + gsel * per_group
    emask = (lane >= first) & (lane < first + per_group)
    le = jnp.where(emask, logits, -jnp.inf)
    me = jnp.max(le, axis=-1, keepdims=True)
    ee = jnp.exp(le - me)
    pe = jnp.where(emask, ee / jnp.sum(ee, axis=-1, keepdims=True), -1.0)
    p1 = jnp.max(pe, axis=-1, keepdims=True)
    i1 = jnp.min(jnp.where(pe == p1, lane, big), axis=-1, keepdims=True)
    pe2 = jnp.where(lane == i1, -1.0, pe)
    p2 = jnp.max(pe2, axis=-1, keepdims=True)
    i2 = jnp.min(jnp.where(pe2 == p2, lane, big), axis=-1, keepdims=True)
    psum = p1 + p2
    w1 = p1 / psum * pg_sel
    w2 = p2 / psum * pg_sel
    e1 = i1 - n_groups
    e2 = i2 - n_groups

    oh1 = (lane == e1).astype(F32)
    oh2 = (lane == e2).astype(F32)
    ohs = oh1 + oh2
    ri = lax.broadcasted_iota(I32, (rows, rows), 0)
    ci = lax.broadcasted_iota(I32, (rows, rows), 1)
    below = (ci < ri).astype(BF16)
    before = jnp.dot(below, ohs.astype(BF16), preferred_element_type=F32) + carry_ref[...]
    r1 = jnp.sum(before * oh1, axis=-1, keepdims=True)
    r2 = jnp.sum(before * oh2, axis=-1, keepdims=True)
    carry_ref[...] = carry_ref[...] + jnp.sum(ohs, axis=0, keepdims=True)
    cnt_ref[...] = carry_ref[...]

    route = jnp.where(lane == 0, e1.astype(F32), 0.0)
    route = jnp.where(lane == 1, e2.astype(F32), route)
    route = jnp.where(lane == 2, w1, route)
    route = jnp.where(lane == 3, w2, route)
    route = jnp.where(lane == 4, r1, route)
    route = jnp.where(lane == 5, r2, route)
    route_ref[...] = route


def _router(x2, gain, wr, br, n_groups, per_group):
    m, d = x2.shape
    br_rows = _row_tile(ROUTE_ROWS, m)
    return pl.pallas_call(
        functools.partial(_router_kernel, n_groups=n_groups, per_group=per_group),
        grid=(m // br_rows,),
        in_specs=[
            pl.BlockSpec((br_rows, d), lambda i: (i, 0)),
            pl.BlockSpec((1, d), lambda i: (0, 0)),
            pl.BlockSpec((d, LANES), lambda i: (0, 0)),
            pl.BlockSpec((1, LANES), lambda i: (0, 0)),
        ],
        out_specs=[
            pl.BlockSpec((br_rows, LANES), lambda i: (i, 0)),
            pl.BlockSpec((1, LANES), lambda i: (0, 0)),
            pl.BlockSpec((br_rows, d // 2), lambda i: (i, 0)),
        ],
        out_shape=[
            jax.ShapeDtypeStruct((m, LANES), F32),
            jax.ShapeDtypeStruct((1, LANES), F32),
            jax.ShapeDtypeStruct((m, d // 2), U32),
        ],
        scratch_shapes=[pltpu.VMEM((d, 2 * LANES), BF16), pltpu.VMEM((1, LANES), F32)],
        compiler_params=_cparams(("arbitrary",)),
        name="norm2_router",
    )(x2, gain, wr, br)


def _pack_pair(first, second):
    hi = lax.bitcast_convert_type(first.astype(BF16).astype(F32), U32)
    lo = lax.bitcast_convert_type(second.astype(BF16).astype(F32), U32)
    return hi | (lo >> 16)


def _pack_halves(x):
    half = x.shape[1] // 2
    return _pack_pair(x[:, :half], x[:, half:])


def _unpack_halves(p):
    first = lax.bitcast_convert_type(p & jnp.uint32(0xFFFF0000), F32)
    second = lax.bitcast_convert_type(p << 16, F32)
    return first, second


def _dispatch_kernel(dest_ref, hp_ref, xs_ref, buf_ref, sem, *, n_tiles):
    i = pl.program_id(0)
    tm = hp_ref.shape[0]
    n_tokens = n_tiles * tm
    slot = i % 2

    def wait_rows(s):
        for k in range(TOP_K):
            pltpu.make_async_copy(buf_ref.at[s], xs_ref.at[pl.ds(0, tm)], sem.at[s]).wait()

    buf_ref[slot] = hp_ref[...]

    def start(r, c):
        for k in range(TOP_K):
            d = dest_ref[k * n_tokens + i * tm + r]
            pltpu.make_async_copy(buf_ref.at[slot, pl.ds(r, 1)], xs_ref.at[pl.ds(d, 1)],
                                  sem.at[slot]).start(priority=k % 2)
        return c

    lax.fori_loop(0, tm, start, 0, unroll=DMA_UNROLL)

    @pl.when(i > 0)
    def _():
        wait_rows(1 - slot)

    @pl.when(i == n_tiles - 1)
    def _():
        wait_rows(slot)


def _dispatch(dest_flat, hp, n_rows):
    m, half = hp.shape
    tm = _row_tile(DISPATCH_ROWS, m)
    return pl.pallas_call(
        functools.partial(_dispatch_kernel, n_tiles=m // tm),
        grid_spec=pltpu.PrefetchScalarGridSpec(
            num_scalar_prefetch=1,
            grid=(m // tm,),
            in_specs=[pl.BlockSpec((tm, half), lambda i, dest: (i, 0))],
            out_specs=pl.BlockSpec(memory_space=pl.ANY),
            scratch_shapes=[pltpu.VMEM((2, tm, half), U32), pltpu.SemaphoreType.DMA((2,))],
        ),
        out_shape=jax.ShapeDtypeStruct((n_rows, half), U32),
        compiler_params=_cparams(("arbitrary",)),
        name="moe_dispatch",
    )(dest_flat, hp)


def _moe_kernel(be_ref, nrows_ref, xs_ref, wg_ref, wu_ref, wd_ref, y_ref, x1_ref, x2_ref, pa_ref, pb_ref):
    i = pl.program_id(0)
    j = pl.program_id(1)
    rows, half = xs_ref.shape

    def partial_proj(x_ref):
        x = x_ref[...]
        return (jnp.dot(x, wg_ref[0].astype(BF16), preferred_element_type=F32),
                jnp.dot(x, wu_ref[0].astype(BF16), preferred_element_type=F32))

    @pl.when(j == 0)
    def _():
        live = lax.broadcasted_iota(I32, (rows, 1), 0) < nrows_ref[i]
        x1, x2 = _unpack_halves(xs_ref[...])
        x1_ref[...] = jnp.where(live, x1, 0.0).astype(BF16)
        x2_ref[...] = jnp.where(live, x2, 0.0).astype(BF16)
        pa_ref[...], pb_ref[...] = partial_proj(x1_ref)

    @pl.when(j == 1)
    def _():
        a, b = partial_proj(x2_ref)
        act = (jax.nn.silu(pa_ref[...] + a) * (pb_ref[...] + b)).astype(BF16)
        first = jnp.dot(act, wd_ref[0, :, :half].astype(BF16), preferred_element_type=F32)
        second = jnp.dot(act, wd_ref[0, :, half:].astype(BF16), preferred_element_type=F32)
        y_ref[...] = _pack_pair(first, second)


def _moe_experts(block_e, block_rows, n_occupied, xs, w_gate, w_up, w_down):
    half = xs.shape[1]
    d = 2 * half
    d_e = w_gate.shape[2]
    br = MOE_ROWS
    return pl.pallas_call(
        _moe_kernel,
        grid_spec=pltpu.PrefetchScalarGridSpec(
            num_scalar_prefetch=2,
            grid=(n_occupied, 2),
            in_specs=[
                pl.BlockSpec((br, half), lambda i, j, be, nr: (i, 0)),
                pl.BlockSpec((1, half, d_e), lambda i, j, be, nr: (be[i], j, 0)),
                pl.BlockSpec((1, half, d_e), lambda i, j, be, nr: (be[i], j, 0)),
                pl.BlockSpec((1, d_e, d), lambda i, j, be, nr: (be[i], 0, 0)),
            ],
            out_specs=pl.BlockSpec((br, half), lambda i, j, be, nr: (i, 0)),
            scratch_shapes=[pltpu.VMEM((br, half), BF16), pltpu.VMEM((br, half), BF16),
                            pltpu.VMEM((br, d_e), F32), pltpu.VMEM((br, d_e), F32)],
        ),
        out_shape=jax.ShapeDtypeStruct(xs.shape, U32),
        compiler_params=_cparams(("arbitrary", "arbitrary")),
        name="moe_experts",
    )(block_e, block_rows, xs, w_gate, w_up, w_down)


def _combine_kernel(dest_ref, x_ref, route_ref, y_ref, op_ref, os_ref, buf_ref, sem, *,
                    n_prompt_tiles, n_tiles):
    i = pl.program_id(0)
    tm = x_ref.shape[0]
    n_tokens = n_tiles * tm
    slot = i % 2

    def gather(tile, to_slot):
        def start(r, c):
            for k in range(TOP_K):
                d = dest_ref[k * n_tokens + tile * tm + r]
                pltpu.make_async_copy(y_ref.at[pl.ds(d, 1)], buf_ref.at[to_slot, k, pl.ds(r, 1)],
                                      sem.at[to_slot]).start()
            return c

        lax.fori_loop(0, tm, start, 0, unroll=DMA_UNROLL)

    @pl.when(i == 0)
    def _():
        gather(0, 0)

    @pl.when(i + 1 < n_tiles)
    def _():
        gather(i + 1, 1 - slot)

    for k in range(TOP_K):
        pltpu.make_async_copy(y_ref.at[pl.ds(0, tm)], buf_ref.at[slot, k], sem.at[slot]).wait()

    half = x_ref.shape[1] // 2
    first, second = x_ref[:, :half], x_ref[:, half:]
    for k in range(TOP_K):
        y1, y2 = _unpack_halves(buf_ref[slot, k])
        w = route_ref[:, 2 + k:3 + k]
        first = first + w * y1
        second = second + w * y2
    out = jnp.concatenate([first, second], axis=1)

    @pl.when(i < n_prompt_tiles)
    def _():
        op_ref[...] = out

    @pl.when(i >= n_prompt_tiles)
    def _():
        os_ref[...] = out


def _combine(dest_flat, x2, route, y, n_prompt_rows):
    m, d = x2.shape
    tm = _row_tile(COMBINE_ROWS, n_prompt_rows, m - n_prompt_rows)
    npt = n_prompt_rows // tm
    return pl.pallas_call(
        functools.partial(_combine_kernel, n_prompt_tiles=npt, n_tiles=m // tm),
        grid_spec=pltpu.PrefetchScalarGridSpec(
            num_scalar_prefetch=1,
            grid=(m // tm,),
            in_specs=[
                pl.BlockSpec((tm, d), lambda i, dest: (i, 0)),
                pl.BlockSpec((tm, LANES), lambda i, dest: (i, 0)),
                pl.BlockSpec(memory_space=pl.ANY),
            ],
            out_specs=[
                pl.BlockSpec((tm, d), lambda i, dest: (jnp.minimum(i, npt - 1), 0)),
                pl.BlockSpec((tm, d), lambda i, dest: (jnp.maximum(i - npt, 0), 0)),
            ],
            scratch_shapes=[pltpu.VMEM((2, TOP_K, tm, d // 2), U32), pltpu.SemaphoreType.DMA((2,))],
        ),
        out_shape=[
            jax.ShapeDtypeStruct((n_prompt_rows, d), F32),
            jax.ShapeDtypeStruct((m - n_prompt_rows, d), F32),
        ],
        compiler_params=_cparams(("arbitrary",)),
        name="moe_combine",
    )(dest_flat, x2, route, y)


def _rope_tables(positions, head_dim):
    rot_dim = head_dim // 4
    half = rot_dim // 2
    inv_freq = jnp.power(ROPE_THETA, -jnp.arange(half, dtype=F32) * 2.0 / rot_dim)
    ang = positions.astype(F32)[:, None] * inv_freq[None, :]
    cos, sin = lax.optimization_barrier((jnp.cos(ang), jnp.sin(ang)))
    m = positions.shape[0]
    zeros = lambda n: jnp.zeros((m, n), F32)
    cos_h = jnp.concatenate([cos, cos, jnp.ones((m, head_dim - rot_dim), F32)], axis=1)
    sa_h = jnp.concatenate([-sin, zeros(head_dim - half)], axis=1)
    sb_h = jnp.concatenate([zeros(half), sin, zeros(head_dim - rot_dim)], axis=1)
    reps = LANES // head_dim
    return tuple(jnp.tile(a, (1, reps)) for a in (cos_h, sa_h, sb_h)), half


def _layer(xp, xs, cache_k, cache_v, norm1_g, w_in, gmlp_norm_g, w_s, b_s, q_norm_g, k_norm_g,
           sinks, p_a, p_b, w_out, norm2_g, w_rg, b_rg, w_re, b_re, w_gate_e, w_up_e, w_down_e):
    s, d = xp.shape
    dec_rows = xs.shape[0]
    dec_batch = cache_k.shape[0]
    t = dec_rows // dec_batch
    m = s + dec_rows
    d_a = gmlp_norm_g.shape[0]
    head_dim = q_norm_g.shape[0]
    n_heads = sinks.shape[0]
    q_dim = n_heads * head_dim
    kv_dim = cache_k.shape[2] * cache_k.shape[3]
    n_groups, per_group = w_re.shape[1], w_re.shape[2]
    n_experts = n_groups * per_group

    h = _norm1(xp, xs, norm1_g[None, :])
    positions = jnp.concatenate(
        [jnp.arange(s, dtype=I32), jnp.tile(PAST_LEN + jnp.arange(t, dtype=I32), dec_batch)])
    (cos_t, sa_t, sb_t), rot_half = _rope_tables(positions, head_dim)
    gain_row = jnp.concatenate([jnp.tile(q_norm_g, n_heads), jnp.tile(k_norm_g, kv_dim // head_dim),
                                jnp.ones((kv_dim,), F32)])[None, :]
    qkv = _inproj_qkv(h, w_in, 2 * d_a, q_dim, kv_dim, gain_row, cos_t, sa_t, sb_t, head_dim, rot_half)

    ua, o_b = _gelu_attention(h, w_in, 0, 2 * d_a, qkv, cache_k.reshape(-1, kv_dim),
                              cache_v.reshape(-1, kv_dim), sinks[None, :], s, q_dim, kv_dim, head_dim,
                              n_heads)
    rows = w_s.shape[1]
    reps = rows // t
    w_sample = jnp.tile(w_s[:, :t, :t], (1, reps, reps))
    b_sample = jnp.tile(b_s[:, :t], (1, reps))
    gates, o_a, vn_s = _gate_gmlp(h, w_in, 2 * d_a + q_dim + 2 * kv_dim, 2 * d, ua, w_s, w_sample,
                                  b_s.T, b_sample.T, gmlp_norm_g[None, :], s, t)

    merged = _merge(o_a, o_b, p_a, p_b, gates)
    x2 = _outproj(merged, w_out, xp, xs)

    pad = LANES - n_groups - n_experts
    wr = jnp.concatenate([w_rg, w_re.reshape(d, n_experts), jnp.zeros((d, pad), F32)], axis=1)
    br = jnp.concatenate([b_rg, b_re.reshape(n_experts), jnp.zeros((pad,), F32)])[None, :]
    route, counts, h2_packed = _router(x2, norm2_g[None, :], wr, br, n_groups, per_group)
    e_idx = route[:, 0:TOP_K].T.astype(I32)
    rank = route[:, 4:4 + TOP_K].T.astype(I32)
    counts = counts[0, :n_experts].astype(I32)

    blk = MOE_ROWS
    n_blocks = -(-(m * TOP_K) // blk) + n_experts
    padded = ((counts + blk - 1) // blk) * blk
    pend = jnp.cumsum(padded)
    pstart = pend - padded
    hit = e_idx[:, :, None] == jnp.arange(n_experts, dtype=I32)
    dest = (jnp.sum(jnp.where(hit, pstart, 0), axis=-1) + rank).reshape(-1).astype(I32)
    n_occupied = (pend[-1] // blk).astype(I32)
    first_row = jnp.arange(n_blocks, dtype=I32) * blk
    block_e = jnp.minimum(jnp.sum((pend[None, :] <= first_row[:, None]).astype(I32), axis=1), n_experts - 1)
    block_rows = jnp.clip(counts[block_e] - (first_row - pstart[block_e]), 0, blk).astype(I32)

    xs_sorted = _dispatch(dest, h2_packed, n_blocks * blk)
    y_sorted = _moe_experts(block_e, block_rows, n_occupied, xs_sorted, w_gate_e, w_up_e, w_down_e)
    yp, ys = _combine(dest, x2, route, y_sorted, s)
    return yp, ys, qkv, vn_s


def kernel(x_prompt, x_sample, cache_k_win, cache_v_win, norm1_g, w_in, gmlp_norm_g, w_s, b_s,
           q_norm_g, k_norm_g, sinks, p_a, p_b, w_out, norm2_g, w_rg, b_rg, w_re, b_re,
           w_gate_e, w_up_e, w_down_e):
    depth = norm1_g.shape[0]
    assert depth == 1, "weights of one layer are expected"
    batch, s, d = x_prompt.shape
    assert batch == 1
    dec_batch, t, _ = x_sample.shape
    head_dim = q_norm_g.shape[-1]
    n_kv = cache_k_win.shape[3]
    q_dim = sinks.shape[-1] * head_dim
    kv_dim = n_kv * head_dim
    keep = min(cache_k_win.shape[2], s)

    l = 0
    yp, ys, qkv, vn_s = _layer(
        x_prompt.reshape(s, d), x_sample.reshape(dec_batch * t, d), cache_k_win[l], cache_v_win[l],
        norm1_g[l], w_in[l], gmlp_norm_g[l], w_s[l], b_s[l], q_norm_g[l], k_norm_g[l], sinks[l],
        p_a[l], p_b[l], w_out[l], norm2_g[l], w_rg[l], b_rg[l], w_re[l], b_re[l],
        w_gate_e[l], w_up_e[l], w_down_e[l])

    k_all = qkv[:, q_dim:q_dim + kv_dim]
    v_all = qkv[:, q_dim + kv_dim:]
    k_win_p = k_all[s - keep:s].reshape(1, batch, keep, n_kv, head_dim)
    v_win_p = v_all[s - keep:s].reshape(1, batch, keep, n_kv, head_dim)
    k_new_s = k_all[s:].reshape(1, dec_batch, t, n_kv, head_dim)
    v_new_s = v_all[s:].reshape(1, dec_batch, t, n_kv, head_dim)
    gv_s = vn_s.reshape(1, dec_batch, t, -1)
    return (yp.reshape(batch, s, d), ys.reshape(dec_batch, t, d), k_win_p, v_win_p, k_new_s, v_new_s, gv_s)
```

```python
import functools

import jax
import jax.numpy as jnp
from jax import lax
from jax.experimental import pallas as pl
from jax.experimental.pallas import tpu as pltpu

F32 = jnp.float32
BF16 = jnp.bfloat16
I32 = jnp.int32
U32 = jnp.uint32

EPS = 1e-6
PAST_LEN = 1024
CHUNK = 64
ROPE_THETA = 500000.0
TOP_K = 2
LANES = 128

VMEM_LIMIT_BYTES = 56 * 1024 * 1024
FUSED_VMEM_LIMIT_BYTES = 58 * 1024 * 1024

ROW_TILE = 1024
WIDE_ROW_TILE = 1536
COL_TILE = 512
SEG_WIDTH = 256
GMLP_ROWS = 128
NORM_ROWS = 512
ROUTE_ROWS = 512
MOE_ROWS = 320
COMBINE_ROWS = 256
DISPATCH_ROWS = 512
DMA_UNROLL = 8


def _cparams(sem, vmem_limit_bytes=VMEM_LIMIT_BYTES):
    return pltpu.CompilerParams(dimension_semantics=sem, vmem_limit_bytes=vmem_limit_bytes)


def _cast_rows(src_ref, dst_ref, rows):
    n = src_ref.shape[0] // rows

    def body(r, c):
        sl = pl.ds(pl.multiple_of(r * rows, rows), rows)
        dst_ref[sl, :] = src_ref[sl, :].astype(dst_ref.dtype)
        return c

    lax.fori_loop(0, n, body, 0)


def _row_tile(preferred, *row_counts):
    tile = preferred
    while any(n % tile for n in row_counts):
        tile //= 2
    assert tile >= 8
    return tile


def _rms(x, gain):
    ms = jnp.mean(x * x, axis=-1, keepdims=True)
    return x * lax.rsqrt(ms + EPS) * gain


def _norm1_kernel(xp_ref, xs_ref, g_ref, h_ref, *, n_prompt_blocks):
    i = pl.program_id(0)

    @pl.when(i < n_prompt_blocks)
    def _():
        h_ref[...] = _rms(xp_ref[...], g_ref[...]).astype(h_ref.dtype)

    @pl.when(i >= n_prompt_blocks)
    def _():
        h_ref[...] = _rms(xs_ref[...], g_ref[...]).astype(h_ref.dtype)


def _norm1(xp, xs, gain):
    s, d = xp.shape
    t = xs.shape[0]
    br = _row_tile(NORM_ROWS, s, t)
    nbp, nbs = s // br, t // br
    return pl.pallas_call(
        functools.partial(_norm1_kernel, n_prompt_blocks=nbp),
        grid=(nbp + nbs,),
        in_specs=[
            pl.BlockSpec((br, d), lambda i: (jnp.minimum(i, nbp - 1), 0)),
            pl.BlockSpec((br, d), lambda i: (jnp.maximum(i - nbp, 0), 0)),
            pl.BlockSpec((1, d), lambda i: (0, 0)),
        ],
        out_specs=pl.BlockSpec((br, d), lambda i: (i, 0)),
        out_shape=jax.ShapeDtypeStruct((s + t, d), BF16),
        compiler_params=_cparams(("arbitrary",)),
        name="norm1",
    )(xp, xs, gain)


def _host_row_tile(m, n_col_tiles, n_side_steps, preferred):
    bm = preferred if m % preferred == 0 else min(ROW_TILE, m)
    while n_col_tiles * (m // bm) < n_side_steps:
        assert bm % 16 == 0
        bm //= 2
    assert m % bm == 0
    return bm


def _inproj_qkv_kernel(h_ref, w_ref, gain_ref, cos_ref, sa_ref, sb_ref, o_ref, wb_ref, *,
                       n_norm_tiles, head_dim, rot_half):
    n = pl.program_id(0)

    @pl.when(pl.program_id(1) == 0)
    def _():
        _cast_rows(w_ref, wb_ref, 256)

    z = jnp.dot(h_ref[...], wb_ref[...], preferred_element_type=F32)
    bn = z.shape[1]

    @pl.when(n < n_norm_tiles)
    def _():
        sw = min(SEG_WIDTH, bn)
        r = lax.broadcasted_iota(I32, (sw, sw), 0) // head_dim
        c = lax.broadcasted_iota(I32, (sw, sw), 1) // head_dim
        seg = (r == c).astype(BF16)
        zz = (z * z).astype(BF16)
        ssq = jnp.concatenate([jnp.dot(zz[:, c0:c0 + sw], seg, preferred_element_type=F32)
                               for c0 in range(0, bn, sw)], axis=1)
        y = z * lax.rsqrt(ssq * (1.0 / head_dim) + EPS) * gain_ref[...]
        reps = bn // cos_ref.shape[1]
        cosv = jnp.tile(cos_ref[...], (1, reps))
        sa = jnp.tile(sa_ref[...], (1, reps))
        sb = jnp.tile(sb_ref[...], (1, reps))
        y = y * cosv + pltpu.roll(y, bn - rot_half, 1) * sa + pltpu.roll(y, rot_half, 1) * sb
        o_ref[...] = y

    @pl.when(n >= n_norm_tiles)
    def _():
        o_ref[...] = z


def _inproj_qkv(h, w, col0, q_dim, kv_dim, gain_row, cos_t, sa_t, sb_t, head_dim, rot_half):
    m, d = h.shape
    ncols = q_dim + 2 * kv_dim
    bm, bn = min(ROW_TILE, m), kv_dim
    assert col0 % bn == 0 and q_dim % bn == 0 and bn % LANES == 0 and m % bm == 0
    off = col0 // bn
    n_norm_tiles = (q_dim + kv_dim) // bn
    tw = cos_t.shape[1]
    return pl.pallas_call(
        functools.partial(_inproj_qkv_kernel, n_norm_tiles=n_norm_tiles, head_dim=head_dim,
                          rot_half=rot_half),
        grid=(ncols // bn, m // bm),
        in_specs=[
            pl.BlockSpec((bm, d), lambda n, i: (i, 0)),
            pl.BlockSpec((d, bn), lambda n, i: (0, n + off)),
            pl.BlockSpec((1, bn), lambda n, i: (0, n)),
            pl.BlockSpec((bm, tw), lambda n, i: (i, 0)),
            pl.BlockSpec((bm, tw), lambda n, i: (i, 0)),
            pl.BlockSpec((bm, tw), lambda n, i: (i, 0)),
        ],
        out_specs=pl.BlockSpec((bm, bn), lambda n, i: (i, n)),
        out_shape=jax.ShapeDtypeStruct((m, ncols), F32),
        scratch_shapes=[pltpu.VMEM((d, bn), BF16)],
        compiler_params=_cparams(("arbitrary", "arbitrary")),
        name="inproj_qkv",
    )(h, w, gain_row, cos_t, sa_t, sb_t)


def _gmlp_rows(u_ref, va_ref, w_ref, b_ref, gain_ref, o_ref, vn_ref, r0, sub):
    groups, rows, _ = w_ref.shape
    gw = u_ref.shape[1] // groups
    rs = slice(r0, r0 + rows)
    ri = lax.broadcasted_iota(I32, (rows, rows), 0)
    ci = lax.broadcasted_iota(I32, (rows, rows), 1)
    mask = (ci <= ri) & ((ri // sub) == (ci // sub))
    vn = _rms(va_ref[rs, :].astype(F32), gain_ref[...])
    if vn_ref is not None:
        vn_ref[rs, :] = vn
    vb = vn.astype(BF16)
    for g in range(groups):
        sl = slice(g * gw, (g + 1) * gw)
        wg = jnp.where(mask, w_ref[g], 0.0).astype(BF16)
        s = jnp.dot(wg, vb[:, sl], preferred_element_type=F32) + b_ref[:, g:g + 1]
        o_ref[rs, sl] = (u_ref[rs, sl].astype(F32) * s).astype(o_ref.dtype)


def _gate_gmlp_kernel(h_ref, w_ref, u_ref, va_ref, wp_ref, ws_ref, bp_ref, bs_ref, gain_ref,
                      g_ref, oa_ref, vn_ref, wb_ref, *, steps_per_col, n_prompt_steps, n_side_steps,
                      sample_len):
    i = pl.program_id(1)
    s = pl.program_id(0) * steps_per_col + i
    chunk = wp_ref.shape[1]

    @pl.when(i == 0)
    def _():
        _cast_rows(w_ref, wb_ref, 256)

    def gate():
        z = jnp.dot(h_ref[...], wb_ref[...], preferred_element_type=F32)
        g_ref[...] = jax.nn.sigmoid(z).astype(g_ref.dtype)

    @pl.when(s < n_prompt_steps)
    def _():
        gate()
        for r0 in range(0, u_ref.shape[0], chunk):
            _gmlp_rows(u_ref, va_ref, wp_ref, bp_ref, gain_ref, oa_ref, None, r0, chunk)

    @pl.when((s >= n_prompt_steps) & (s < n_side_steps))
    def _():
        gate()
        for r0 in range(0, u_ref.shape[0], chunk):
            _gmlp_rows(u_ref, va_ref, ws_ref, bs_ref, gain_ref, oa_ref, vn_ref, r0, sample_len)

    @pl.when(s >= n_side_steps)
    def _():
        gate()


def _gate_gmlp(h, w, col0, ncols, ua, w_prompt, w_sample, b_prompt, b_sample, gain, n_prompt_rows,
               sample_len):
    m, d = h.shape
    d_a = ua.shape[1] // 2
    groups, chunk, _ = w_prompt.shape
    bn = min(COL_TILE, ncols)
    side = min(GMLP_ROWS, m - n_prompt_rows)
    assert col0 % bn == 0 and ncols % bn == 0
    assert n_prompt_rows % side == 0 and (m - n_prompt_rows) % side == 0 and side % chunk == 0
    nps, nss = n_prompt_rows // side, m // side
    bm = _host_row_tile(m, ncols // bn, nss, WIDE_ROW_TILE)
    off = col0 // bn
    spc = m // bm
    step = lambda n, i: n * spc + i
    side_blk = lambda n, i: jnp.minimum(step(n, i), nss - 1)
    full3 = lambda n, i: (0, 0, 0)
    full2 = lambda n, i: (0, 0)
    return pl.pallas_call(
        functools.partial(_gate_gmlp_kernel, steps_per_col=spc, n_prompt_steps=nps, n_side_steps=nss,
                          sample_len=sample_len),
        grid=(ncols // bn, spc),
        in_specs=[
            pl.BlockSpec((bm, d), lambda n, i: (i, 0)),
            pl.BlockSpec((d, bn), lambda n, i: (0, n + off)),
            pl.BlockSpec((side, d_a), lambda n, i: (side_blk(n, i), 0)),
            pl.BlockSpec((side, d_a), lambda n, i: (side_blk(n, i), 1)),
            pl.BlockSpec((groups, chunk, chunk), full3),
            pl.BlockSpec((groups, chunk, chunk), full3),
            pl.BlockSpec((chunk, groups), full2),
            pl.BlockSpec((chunk, groups), full2),
            pl.BlockSpec((1, d_a), full2),
        ],
        out_specs=[
            pl.BlockSpec((bm, bn), lambda n, i: (i, n)),
            pl.BlockSpec((side, d_a), lambda n, i: (side_blk(n, i), 0)),
            pl.BlockSpec((side, d_a), lambda n, i: (jnp.clip(step(n, i) - nps, 0, nss - nps - 1), 0)),
        ],
        out_shape=[
            jax.ShapeDtypeStruct((m, ncols), BF16),
            jax.ShapeDtypeStruct((m, d_a), BF16),
            jax.ShapeDtypeStruct((m - n_prompt_rows, d_a), F32),
        ],
        scratch_shapes=[pltpu.VMEM((d, bn), BF16)],
        compiler_params=_cparams(("arbitrary", "arbitrary"), FUSED_VMEM_LIMIT_BYTES),
        name="inproj_gate_gmlp",
    )(h, w, ua, ua, w_prompt, w_sample, b_prompt, b_sample, gain)


def _gelu_attn_kernel(h_ref, w_ref, bias_ref, q_ref, ka_ref, kb_ref, kc_ref, kd_ref, va_ref, vb_ref,
                      vc_ref, vd_ref, ck_ref, cv_ref, ua_ref, o_ref, wb_ref, s_ref, e_ref, *,
                      steps_per_col, n_prompt_steps, n_side_steps, tiles_per_kv, head_dim):
    i = pl.program_id(0) * steps_per_col + pl.program_id(1)

    @pl.when(pl.program_id(1) == 0)
    def _():
        _cast_rows(w_ref, wb_ref, 256)

    def project():
        z = jnp.dot(h_ref[...], wb_ref[...], preferred_element_type=F32)
        ua_ref[...] = jax.nn.gelu(z).astype(ua_ref.dtype)

    t = q_ref.shape[0] // 2
    kv_dim = kc_ref.shape[1]
    lkp = bias_ref.shape[1] // 2
    scale = head_dim ** -0.5
    nt = (((1,), (1,)), ((), ()))
    n_kv = kv_dim // head_dim
    tiles_of = lambda j: [j * tiles_per_kv + a for a in range(tiles_per_kv)]

    def run(halves):
        prepared = []
        for k_parts, v_parts, first_key_chunk in halves:
            k = jnp.concatenate(k_parts, axis=0)
            v = jnp.concatenate(v_parts, axis=0)
            chunk_bias = None
            if first_key_chunk is not None:
                col = lax.broadcasted_iota(I32, (1, 2 * lkp), 1)
                key_chunk = first_key_chunk + (col % lkp) // t
                chunk_bias = jnp.where(key_chunk >= 0, 0.0, -jnp.inf)
            prepared.append((k, v, chunk_bias))
        lk = prepared[0][0].shape[0]
        lower = lax.broadcasted_iota(I32, (lk, LANES), 1) < head_dim
        zpad = jnp.zeros((lkp - lk, LANES), F32)
        rr = lax.broadcasted_iota(I32, (2 * lkp, LANES), 0) < lkp
        rl = lax.broadcasted_iota(I32, (2 * lkp, LANES), 1) < head_dim
        ones_sel = (rr == rl).astype(BF16)

        def doubled(tile, rolled, x):
            lo_src, hi_src = (tile, rolled) if x == 0 else (rolled, tile)
            return jnp.concatenate([jnp.where(lower, lo_src, 0.0), zpad,
                                    jnp.where(lower, 0.0, hi_src), zpad], axis=0).astype(BF16)

        for h, (k, _, chunk_bias) in enumerate(prepared):
            rows_h = slice(h * t, (h + 1) * t)
            for b in range(kv_dim // LANES):
                kt = k[:, b * LANES:(b + 1) * LANES]
                kr = pltpu.roll(kt, head_dim, 1)
                for x in range(2):
                    j = 2 * b + x
                    q2 = jnp.concatenate([q_ref[rows_h, a * LANES:(a + 1) * LANES] for a in tiles_of(j)],
                                         axis=0)
                    q2 = (q2 * scale).astype(BF16)
                    rows = []
                    for a in tiles_of(j):
                        row = bias_ref[a:a + 1, :]
                        if chunk_bias is not None:
                            row = row + chunk_bias
                        rows.append(jnp.broadcast_to(row, (t, 2 * lkp)))
                    s_ref[h, j] = (lax.dot_general(q2, doubled(kt, kr, x), nt, preferred_element_type=F32)
                                   + jnp.concatenate(rows, axis=0))

        for h in range(len(prepared)):
            for j in range(n_kv):
                for hs in (slice(0, lkp), slice(lkp, 2 * lkp)):
                    sh = s_ref[h, j, :, hs]
                    e_ref[h, j, :, hs] = jnp.exp(sh - jnp.max(sh, axis=-1, keepdims=True)).astype(BF16)

        for h, (_, v, _) in enumerate(prepared):
            for b in range(kv_dim // LANES):
                vt = v[:, b * LANES:(b + 1) * LANES]
                vr = pltpu.roll(vt, head_dim, 1)
                for x in range(2):
                    j = 2 * b + x
                    rhs = jnp.concatenate([doubled(vt, vr, x), ones_sel], axis=1)
                    r = jnp.dot(e_ref[h, j], rhs, preferred_element_type=F32)
                    o = r[:, :LANES] / r[:, LANES:]
                    for n, a in enumerate(tiles_of(j)):
                        o_ref[h * t:(h + 1) * t, a * LANES:(a + 1) * LANES] = (
                            o[n * t:(n + 1) * t].astype(o_ref.dtype))

    @pl.when(i < n_prompt_steps)
    def _():
        project()
        ka, kb, kc, kd = ka_ref[...], kb_ref[...], kc_ref[...], kd_ref[...]
        va, vb, vc, vd = va_ref[...], vb_ref[...], vc_ref[...], vd_ref[...]
        run([([ka, kb, kc], [va, vb, vc], 2 * i - 2), ([kb, kc, kd], [vb, vc, vd], 2 * i - 1)])

    @pl.when((i >= n_prompt_steps) & (i < n_side_steps))
    def _():
        project()
        w = ck_ref.shape[0] // 2
        run([([ck_ref[:w, :], kc_ref[...]], [cv_ref[:w, :], vc_ref[...]], None),
             ([ck_ref[w:, :], kd_ref[...]], [cv_ref[w:, :], vd_ref[...]], None)])

    @pl.when(i >= n_side_steps)
    def _():
        project()


def _gelu_attention(h, w, col0, ncols, qkv, cache_k, cache_v, sinks, n_prompt_rows, q_dim, kv_dim,
                    head_dim, n_heads):
    m, d = h.shape
    t = CHUNK
    nc = n_prompt_rows // t
    n_kv_heads = kv_dim // head_dim
    window = cache_k.shape[0] // ((m - n_prompt_rows) // t)
    kcol, vcol = q_dim // kv_dim, q_dim // kv_dim + 1
    gqa = n_heads // n_kv_heads
    assert 2 * head_dim == LANES and gqa % 2 == 0 and kv_dim % LANES == 0

    lk = window + t
    lkp = -(-(lk + 1) // LANES) * LANES
    pair = sinks.reshape(n_heads // 2, 2, 1)
    bias = jnp.concatenate([jnp.zeros((n_heads // 2, 2, lk), F32), pair,
                            jnp.full((n_heads // 2, 2, lkp - lk - 1), -jnp.inf, F32)], axis=2)
    bias = bias.reshape(n_heads // 2, 2 * lkp)

    assert nc % 2 == 0 and (m // t - nc) % 2 == 0, "query chunks are processed in pairs"
    nps, nss = nc // 2, m // (2 * t)

    bn = min(COL_TILE, ncols)
    assert col0 % bn == 0 and ncols % bn == 0
    bm = _host_row_tile(m, ncols // bn, nss, ROW_TILE)
    spc = m // bm
    off = col0 // bn
    pair_of = lambda n, i: jnp.minimum(n * spc + i, nss - 1)

    def kv_spec(back, col):
        return pl.BlockSpec((t, kv_dim), lambda n, i: (jnp.maximum(2 * pair_of(n, i) + back, 0), col))

    cache_spec = pl.BlockSpec((2 * window, kv_dim), lambda n, i: (jnp.maximum(pair_of(n, i) - nps, 0), 0))
    scratch = (2, n_kv_heads, gqa // 2 * t, 2 * lkp)
    return pl.pallas_call(
        functools.partial(_gelu_attn_kernel, steps_per_col=spc, n_prompt_steps=nps, n_side_steps=nss,
                          tiles_per_kv=gqa // 2, head_dim=head_dim),
        grid=(ncols // bn, spc),
        in_specs=[
            pl.BlockSpec((bm, d), lambda n, i: (i, 0)),
            pl.BlockSpec((d, bn), lambda n, i: (0, n + off)),
            pl.BlockSpec((n_heads // 2, 2 * lkp), lambda n, i: (0, 0)),
            pl.BlockSpec((2 * t, q_dim), lambda n, i: (pair_of(n, i), 0)),
            kv_spec(-2, kcol), kv_spec(-1, kcol), kv_spec(0, kcol), kv_spec(1, kcol),
            kv_spec(-2, vcol), kv_spec(-1, vcol), kv_spec(0, vcol), kv_spec(1, vcol),
            cache_spec, cache_spec,
        ],
        out_specs=[
            pl.BlockSpec((bm, bn), lambda n, i: (i, n)),
            pl.BlockSpec((2 * t, q_dim), lambda n, i: (pair_of(n, i), 0)),
        ],
        out_shape=[
            jax.ShapeDtypeStruct((m, ncols), BF16),
            jax.ShapeDtypeStruct((m, q_dim), BF16),
        ],
        scratch_shapes=[pltpu.VMEM((d, bn), BF16), pltpu.VMEM(scratch, F32), pltpu.VMEM(scratch, BF16)],
        compiler_params=_cparams(("arbitrary", "arbitrary")),
        name="inproj_gelu_attention",
    )(h, w, bias, qkv, qkv, qkv, qkv, qkv, qkv, qkv, qkv, qkv, cache_k, cache_v)


def _merge_kernel(oa_ref, ob_ref, pa_ref, pb_ref, ga_ref, gb_ref, o_ref, pab_ref, pbb_ref):
    @pl.when(pl.program_id(1) == 0)
    def _():
        _cast_rows(pa_ref, pab_ref, 256)
        _cast_rows(pb_ref, pbb_ref, 256)

    a = jnp.dot(oa_ref[...], pab_ref[...], preferred_element_type=F32)
    b = jnp.dot(ob_ref[...], pbb_ref[...], preferred_element_type=F32)
    o_ref[...] = (ga_ref[...].astype(F32) * a + gb_ref[...].astype(F32) * b).astype(o_ref.dtype)


def _merge(o_a, o_b, p_a, p_b, gates):
    m, d_a = o_a.shape
    q_dim = o_b.shape[1]
    d = p_a.shape[1]
    bm, bn = min(ROW_TILE, m), min(COL_TILE, d)
    goff = d // bn
    return pl.pallas_call(
        _merge_kernel,
        grid=(d // bn, m // bm),
        in_specs=[
            pl.BlockSpec((bm, d_a), lambda n, i: (i, 0)),
            pl.BlockSpec((bm, q_dim), lambda n, i: (i, 0)),
            pl.BlockSpec((d_a, bn), lambda n, i: (0, n)),
            pl.BlockSpec((q_dim, bn), lambda n, i: (0, n)),
            pl.BlockSpec((bm, bn), lambda n, i: (i, n)),
            pl.BlockSpec((bm, bn), lambda n, i: (i, n + goff)),
        ],
        out_specs=pl.BlockSpec((bm, bn), lambda n, i: (i, n)),
        out_shape=jax.ShapeDtypeStruct((m, d), BF16),
        scratch_shapes=[pltpu.VMEM((d_a, bn), BF16), pltpu.VMEM((q_dim, bn), BF16)],
        compiler_params=_cparams(("arbitrary", "arbitrary")),
        name="merge_proj",
    )(o_a, o_b, p_a, p_b, gates, gates)


def _outproj_kernel(t_ref, w_ref, xp_ref, xs_ref, o_ref, wb_ref, *, n_prompt_tiles):
    i = pl.program_id(1)

    @pl.when(i == 0)
    def _():
        _cast_rows(w_ref, wb_ref, 256)

    z = jnp.dot(t_ref[...], wb_ref[...], preferred_element_type=F32)

    @pl.when(i < n_prompt_tiles)
    def _():
        o_ref[...] = xp_ref[...] + z

    @pl.when(i >= n_prompt_tiles)
    def _():
        o_ref[...] = xs_ref[...] + z


def _outproj(tm, w_out, xp, xs):
    m, d = tm.shape
    s, t = xp.shape[0], xs.shape[0]
    bm, bn = min(ROW_TILE, t), min(COL_TILE, d)
    npt = s // bm
    return pl.pallas_call(
        functools.partial(_outproj_kernel, n_prompt_tiles=npt),
        grid=(d // bn, m // bm),
        in_specs=[
            pl.BlockSpec((bm, d), lambda n, i: (i, 0)),
            pl.BlockSpec((d, bn), lambda n, i: (0, n)),
            pl.BlockSpec((bm, bn), lambda n, i: (jnp.minimum(i, npt - 1), n)),
            pl.BlockSpec((bm, bn), lambda n, i: (jnp.maximum(i - npt, 0), n)),
        ],
        out_specs=pl.BlockSpec((bm, bn), lambda n, i: (i, n)),
        out_shape=jax.ShapeDtypeStruct((m, d), F32),
        scratch_shapes=[pltpu.VMEM((d, bn), BF16)],
        compiler_params=_cparams(("arbitrary", "arbitrary")),
        name="out_proj",
    )(tm, w_out, xp, xs)


def _router_kernel(x_ref, g_ref, wr_ref, br_ref, route_ref, cnt_ref, hp_ref, wcat_ref, carry_ref, *,
                   n_groups, per_group):
    i = pl.program_id(0)

    @pl.when(i == 0)
    def _():
        w = wr_ref[...]
        hi = w.astype(BF16)
        wcat_ref[:, :LANES] = hi
        wcat_ref[:, LANES:] = (w - hi.astype(F32)).astype(BF16)
        carry_ref[...] = jnp.zeros_like(carry_ref)

    hn = _rms(x_ref[...], g_ref[...])
    hp_ref[...] = _pack_halves(hn)
    rows = hn.shape[0]
    hi = hn.astype(BF16)
    lo = (hn - hi.astype(F32)).astype(BF16)
    prod = jnp.dot(jnp.concatenate([hi, lo], axis=0), wcat_ref[...], preferred_element_type=F32)
    logits = prod[:rows, :LANES] + prod[:rows, LANES:] + prod[rows:, :LANES] + br_ref[...]
    lane = lax.broadcasted_iota(I32, logits.shape, 1)
    big = jnp.int32(LANES)

    lg = jnp.where(lane < n_groups, logits, -jnp.inf)
    mg = jnp.max(lg, axis=-1, keepdims=True)
    pg_sel = 1.0 / jnp.sum(jnp.exp(lg - mg), axis=-1, keepdims=True)
    gsel = jnp.min(jnp.where(lg == mg, lane, big), axis=-1, keepdims=True)

    first = n_groups + gsel * per_group
    emask = (lane >= first) & (lane < first + per_group)
    le = jnp.where(emask, logits, -jnp.inf)
    me = jnp.max(le, axis=-1, keepdims=True)
    ee = jnp.exp(le - me)
    pe = jnp.where(emask, ee / jnp.sum(ee, axis=-1, keepdims=True), -1.0)
    p1 = jnp.max(pe, axis=-1, keepdims=True)
    i1 = jnp.min(jnp.where(pe == p1, lane, big), axis=-1, keepdims=True)
    pe2 = jnp.where(lane == i1, -1.0, pe)
    p2 = jnp.max(pe2, axis=-1, keepdims=True)
    i2 = jnp.min(jnp.where(pe2 == p2, lane, big), axis=-1, keepdims=True)
    psum = p1 + p2
    w1 = p1 / psum * pg_sel
    w2 = p2 / psum * pg_sel
    e1 = i1 - n_groups
    e2 = i2 - n_groups

    oh1 = (lane == e1).astype(F32)
    oh2 = (lane == e2).astype(F32)
    ohs = oh1 + oh2
    ri = lax.broadcasted_iota(I32, (rows, rows), 0)
    ci = lax.broadcasted_iota(I32, (rows, rows), 1)
    below = (ci < ri).astype(BF16)
    before = jnp.dot(below, ohs.astype(BF16), preferred_element_type=F32) + carry_ref[...]
    r1 = jnp.sum(before * oh1, axis=-1, keepdims=True)
    r2 = jnp.sum(before * oh2, axis=-1, keepdims=True)
    carry_ref[...] = carry_ref[...] + jnp.sum(ohs, axis=0, keepdims=True)
    cnt_ref[...] = carry_ref[...]

    route = jnp.where(lane == 0, e1.astype(F32), 0.0)
    route = jnp.where(lane == 1, e2.astype(F32), route)
    route = jnp.where(lane == 2, w1, route)
    route = jnp.where(lane == 3, w2, route)
    route = jnp.where(lane == 4, r1, route)
    route = jnp.where(lane == 5, r2, route)
    route_ref[...] = route


def _router(x2, gain, wr, br, n_groups, per_group):
    m, d = x2.shape
    br_rows = _row_tile(ROUTE_ROWS, m)
    return pl.pallas_call(
        functools.partial(_router_kernel, n_groups=n_groups, per_group=per_group),
        grid=(m // br_rows,),
        in_specs=[
            pl.BlockSpec((br_rows, d), lambda i: (i, 0)),
            pl.BlockSpec((1, d), lambda i: (0, 0)),
            pl.BlockSpec((d, LANES), lambda i: (0, 0)),
            pl.BlockSpec((1, LANES), lambda i: (0, 0)),
        ],
        out_specs=[
            pl.BlockSpec((br_rows, LANES), lambda i: (i, 0)),
            pl.BlockSpec((1, LANES), lambda i: (0, 0)),
            pl.BlockSpec((br_rows, d // 2), lambda i: (i, 0)),
        ],
        out_shape=[
            jax.ShapeDtypeStruct((m, LANES), F32),
            jax.ShapeDtypeStruct((1, LANES), F32),
            jax.ShapeDtypeStruct((m, d // 2), U32),
        ],
        scratch_shapes=[pltpu.VMEM((d, 2 * LANES), BF16), pltpu.VMEM((1, LANES), F32)],
        compiler_params=_cparams(("arbitrary",)),
        name="norm2_router",
    )(x2, gain, wr, br)


def _pack_pair(first, second):
    hi = lax.bitcast_convert_type(first.astype(BF16).astype(F32), U32)
    lo = lax.bitcast_convert_type(second.astype(BF16).astype(F32), U32)
    return hi | (lo >> 16)


def _pack_halves(x):
    half = x.shape[1] // 2
    return _pack_pair(x[:, :half], x[:, half:])


def _unpack_halves(p):
    first = lax.bitcast_convert_type(p & jnp.uint32(0xFFFF0000), F32)
    second = lax.bitcast_convert_type(p << 16, F32)
    return first, second


def _dispatch_kernel(dest_ref, hp_ref, xs_ref, buf_ref, sem, *, n_tiles):
    i = pl.program_id(0)
    tm = hp_ref.shape[0]
    n_tokens = n_tiles * tm
    slot = i % 2

    def wait_rows(s):
        for k in range(TOP_K):
            pltpu.make_async_copy(buf_ref.at[s], xs_ref.at[pl.ds(0, tm)], sem.at[s]).wait()

    buf_ref[slot] = hp_ref[...]

    def start(r, c):
        for k in range(TOP_K):
            d = dest_ref[k * n_tokens + i * tm + r]
            pltpu.make_async_copy(buf_ref.at[slot, pl.ds(r, 1)], xs_ref.at[pl.ds(d, 1)],
                                  sem.at[slot]).start()
        return c

    lax.fori_loop(0, tm, start, 0, unroll=DMA_UNROLL)

    @pl.when(i > 0)
    def _():
        wait_rows(1 - slot)

    @pl.when(i == n_tiles - 1)
    def _():
        wait_rows(slot)


def _dispatch(dest_flat, hp, n_rows):
    m, half = hp.shape
    tm = _row_tile(DISPATCH_ROWS, m)
    return pl.pallas_call(
        functools.partial(_dispatch_kernel, n_tiles=m // tm),
        grid_spec=pltpu.PrefetchScalarGridSpec(
            num_scalar_prefetch=1,
            grid=(m // tm,),
            in_specs=[pl.BlockSpec((tm, half), lambda i, dest: (i, 0))],
            out_specs=pl.BlockSpec(memory_space=pl.ANY),
            scratch_shapes=[pltpu.VMEM((2, tm, half), U32), pltpu.SemaphoreType.DMA((2,))],
        ),
        out_shape=jax.ShapeDtypeStruct((n_rows, half), U32),
        compiler_params=_cparams(("arbitrary",)),
        name="moe_dispatch",
    )(dest_flat, hp)


def _moe_kernel(be_ref, nrows_ref, xs_ref, wg_ref, wu_ref, wd_ref, y_ref, x1_ref, x2_ref, pa_ref, pb_ref):
    i = pl.program_id(0)
    j = pl.program_id(1)
    rows, half = xs_ref.shape

    def partial_proj(x_ref):
        x = x_ref[...]
        return (jnp.dot(x, wg_ref[0].astype(BF16), preferred_element_type=F32),
                jnp.dot(x, wu_ref[0].astype(BF16), preferred_element_type=F32))

    @pl.when(j == 0)
    def _():
        live = lax.broadcasted_iota(I32, (rows, 1), 0) < nrows_ref[i]
        x1, x2 = _unpack_halves(xs_ref[...])
        x1_ref[...] = jnp.where(live, x1, 0.0).astype(BF16)
        x2_ref[...] = jnp.where(live, x2, 0.0).astype(BF16)
        pa_ref[...], pb_ref[...] = partial_proj(x1_ref)

    @pl.when(j == 1)
    def _():
        a, b = partial_proj(x2_ref)
        act = (jax.nn.silu(pa_ref[...] + a) * (pb_ref[...] + b)).astype(BF16)
        first = jnp.dot(act, wd_ref[0, :, :half].astype(BF16), preferred_element_type=F32)
        second = jnp.dot(act, wd_ref[0, :, half:].astype(BF16), preferred_element_type=F32)
        y_ref[...] = _pack_pair(first, second)


def _moe_experts(block_e, block_rows, n_occupied, xs, w_gate, w_up, w_down):
    half = xs.shape[1]
    d = 2 * half
    d_e = w_gate.shape[2]
    br = MOE_ROWS
    return pl.pallas_call(
        _moe_kernel,
        grid_spec=pltpu.PrefetchScalarGridSpec(
            num_scalar_prefetch=2,
            grid=(n_occupied, 2),
            in_specs=[
                pl.BlockSpec((br, half), lambda i, j, be, nr: (i, 0)),
                pl.BlockSpec((1, half, d_e), lambda i, j, be, nr: (be[i], j, 0)),
                pl.BlockSpec((1, half, d_e), lambda i, j, be, nr: (be[i], j, 0)),
                pl.BlockSpec((1, d_e, d), lambda i, j, be, nr: (be[i], 0, 0)),
            ],
            out_specs=pl.BlockSpec((br, half), lambda i, j, be, nr: (i, 0)),
            scratch_shapes=[pltpu.VMEM((br, half), BF16), pltpu.VMEM((br, half), BF16),
                            pltpu.VMEM((br, d_e), F32), pltpu.VMEM((br, d_e), F32)],
        ),
        out_shape=jax.ShapeDtypeStruct(xs.shape, U32),
        compiler_params=_cparams(("arbitrary", "arbitrary")),
        name="moe_experts",
    )(block_e, block_rows, xs, w_gate, w_up, w_down)


def _combine_kernel(dest_ref, x_ref, route_ref, y_ref, op_ref, os_ref, buf_ref, sem, *,
                    n_prompt_tiles, n_tiles):
    i = pl.program_id(0)
    tm = x_ref.shape[0]
    n_tokens = n_tiles * tm
    slot = i % 2

    def gather(tile, to_slot):
        def start(r, c):
            for k in range(TOP_K):
                d = dest_ref[k * n_tokens + tile * tm + r]
                pltpu.make_async_copy(y_ref.at[pl.ds(d, 1)], buf_ref.at[to_slot, k, pl.ds(r, 1)],
                                      sem.at[to_slot]).start(priority=k % 2)
            return c

        lax.fori_loop(0, tm, start, 0, unroll=DMA_UNROLL)

    @pl.when(i == 0)
    def _():
        gather(0, 0)

    @pl.when(i + 1 < n_tiles)
    def _():
        gather(i + 1, 1 - slot)

    for k in range(TOP_K):
        pltpu.make_async_copy(y_ref.at[pl.ds(0, tm)], buf_ref.at[slot, k], sem.at[slot]).wait()

    half = x_ref.shape[1] // 2
    first, second = x_ref[:, :half], x_ref[:, half:]
    for k in range(TOP_K):
        y1, y2 = _unpack_halves(buf_ref[slot, k])
        w = route_ref[:, 2 + k:3 + k]
        first = first + w * y1
        second = second + w * y2
    out = jnp.concatenate([first, second], axis=1)

    @pl.when(i < n_prompt_tiles)
    def _():
        op_ref[...] = out

    @pl.when(i >= n_prompt_tiles)
    def _():
        os_ref[...] = out


def _combine(dest_flat, x2, route, y, n_prompt_rows):
    m, d = x2.shape
    tm = _row_tile(COMBINE_ROWS, n_prompt_rows, m - n_prompt_rows)
    npt = n_prompt_rows // tm
    return pl.pallas_call(
        functools.partial(_combine_kernel, n_prompt_tiles=npt, n_tiles=m // tm),
        grid_spec=pltpu.PrefetchScalarGridSpec(
            num_scalar_prefetch=1,
            grid=(m // tm,),
            in_specs=[
                pl.BlockSpec((tm, d), lambda i, dest: (i, 0)),
                pl.BlockSpec((tm, LANES), lambda i, dest: (i, 0)),
                pl.BlockSpec(memory_space=pl.ANY),
            ],
            out_specs=[
                pl.BlockSpec((tm, d), lambda i, dest: (jnp.minimum(i, npt - 1), 0)),
                pl.BlockSpec((tm, d), lambda i, dest: (jnp.maximum(i - npt, 0), 0)),
            ],
            scratch_shapes=[pltpu.VMEM((2, TOP_K, tm, d // 2), U32), pltpu.SemaphoreType.DMA((2,))],
        ),
        out_shape=[
            jax.ShapeDtypeStruct((n_prompt_rows, d), F32),
            jax.ShapeDtypeStruct((m - n_prompt_rows, d), F32),
        ],
        compiler_params=_cparams(("arbitrary",)),
        name="moe_combine",
    )(dest_flat, x2, route, y)


def _rope_tables(positions, head_dim):
    rot_dim = head_dim // 4
    half = rot_dim // 2
    inv_freq = jnp.power(ROPE_THETA, -jnp.arange(half, dtype=F32) * 2.0 / rot_dim)
    ang = positions.astype(F32)[:, None] * inv_freq[None, :]
    cos, sin = lax.optimization_barrier((jnp.cos(ang), jnp.sin(ang)))
    m = positions.shape[0]
    zeros = lambda n: jnp.zeros((m, n), F32)
    cos_h = jnp.concatenate([cos, cos, jnp.ones((m, head_dim - rot_dim), F32)], axis=1)
    sa_h = jnp.concatenate([-sin, zeros(head_dim - half)], axis=1)
    sb_h = jnp.concatenate([zeros(half), sin, zeros(head_dim - rot_dim)], axis=1)
    reps = LANES // head_dim
    return tuple(jnp.tile(a, (1, reps)) for a in (cos_h, sa_h, sb_h)), half


def _layer(xp, xs, cache_k, cache_v, norm1_g, w_in, gmlp_norm_g, w_s, b_s, q_norm_g, k_norm_g,
           sinks, p_a, p_b, w_out, norm2_g, w_rg, b_rg, w_re, b_re, w_gate_e, w_up_e, w_down_e):
    s, d = xp.shape
    dec_rows = xs.shape[0]
    dec_batch = cache_k.shape[0]
    t = dec_rows // dec_batch
    m = s + dec_rows
    d_a = gmlp_norm_g.shape[0]
    head_dim = q_norm_g.shape[0]
    n_heads = sinks.shape[0]
    q_dim = n_heads * head_dim
    kv_dim = cache_k.shape[2] * cache_k.shape[3]
    n_groups, per_group = w_re.shape[1], w_re.shape[2]
    n_experts = n_groups * per_group

    h = _norm1(xp, xs, norm1_g[None, :])
    positions = jnp.concatenate(
        [jnp.arange(s, dtype=I32), jnp.tile(PAST_LEN + jnp.arange(t, dtype=I32), dec_batch)])
    (cos_t, sa_t, sb_t), rot_half = _rope_tables(positions, head_dim)
    gain_row = jnp.concatenate([jnp.tile(q_norm_g, n_heads), jnp.tile(k_norm_g, kv_dim // head_dim),
                                jnp.ones((kv_dim,), F32)])[None, :]
    qkv = _inproj_qkv(h, w_in, 2 * d_a, q_dim, kv_dim, gain_row, cos_t, sa_t, sb_t, head_dim, rot_half)

    ua, o_b = _gelu_attention(h, w_in, 0, 2 * d_a, qkv, cache_k.reshape(-1, kv_dim),
                              cache_v.reshape(-1, kv_dim), sinks[None, :], s, q_dim, kv_dim, head_dim,
                              n_heads)
    rows = w_s.shape[1]
    reps = rows // t
    w_sample = jnp.tile(w_s[:, :t, :t], (1, reps, reps))
    b_sample = jnp.tile(b_s[:, :t], (1, reps))
    gates, o_a, vn_s = _gate_gmlp(h, w_in, 2 * d_a + q_dim + 2 * kv_dim, 2 * d, ua, w_s, w_sample,
                                  b_s.T, b_sample.T, gmlp_norm_g[None, :], s, t)

    merged = _merge(o_a, o_b, p_a, p_b, gates)
    x2 = _outproj(merged, w_out, xp, xs)

    pad = LANES - n_groups - n_experts
    wr = jnp.concatenate([w_rg, w_re.reshape(d, n_experts), jnp.zeros((d, pad), F32)], axis=1)
    br = jnp.concatenate([b_rg, b_re.reshape(n_experts), jnp.zeros((pad,), F32)])[None, :]
    route, counts, h2_packed = _router(x2, norm2_g[None, :], wr, br, n_groups, per_group)
    e_idx = route[:, 0:TOP_K].T.astype(I32)
    rank = route[:, 4:4 + TOP_K].T.astype(I32)
    counts = counts[0, :n_experts].astype(I32)

    blk = MOE_ROWS
    n_blocks = -(-(m * TOP_K) // blk) + n_experts
    padded = ((counts + blk - 1) // blk) * blk
    pend = jnp.cumsum(padded)
    pstart = pend - padded
    hit = e_idx[:, :, None] == jnp.arange(n_experts, dtype=I32)
    dest = (jnp.sum(jnp.where(hit, pstart, 0), axis=-1) + rank).reshape(-1).astype(I32)
    n_occupied = (pend[-1] // blk).astype(I32)
    first_row = jnp.arange(n_blocks, dtype=I32) * blk
    block_e = jnp.minimum(jnp.sum((pend[None, :] <= first_row[:, None]).astype(I32), axis=1), n_experts - 1)
    block_rows = jnp.clip(counts[block_e] - (first_row - pstart[block_e]), 0, blk).astype(I32)

    xs_sorted = _dispatch(dest, h2_packed, n_blocks * blk)
    y_sorted = _moe_experts(block_e, block_rows, n_occupied, xs_sorted, w_gate_e, w_up_e, w_down_e)
    yp, ys = _combine(dest, x2, route, y_sorted, s)
    return yp, ys, qkv, vn_s


def kernel(x_prompt, x_sample, cache_k_win, cache_v_win, norm1_g, w_in, gmlp_norm_g, w_s, b_s,
           q_norm_g, k_norm_g, sinks, p_a, p_b, w_out, norm2_g, w_rg, b_rg, w_re, b_re,
           w_gate_e, w_up_e, w_down_e):
    depth = norm1_g.shape[0]
    assert depth == 1, "weights of one layer are expected"
    batch, s, d = x_prompt.shape
    assert batch == 1
    dec_batch, t, _ = x_sample.shape
    head_dim = q_norm_g.shape[-1]
    n_kv = cache_k_win.shape[3]
    q_dim = sinks.shape[-1] * head_dim
    kv_dim = n_kv * head_dim
    keep = min(cache_k_win.shape[2], s)

    l = 0
    yp, ys, qkv, vn_s = _layer(
        x_prompt.reshape(s, d), x_sample.reshape(dec_batch * t, d), cache_k_win[l], cache_v_win[l],
        norm1_g[l], w_in[l], gmlp_norm_g[l], w_s[l], b_s[l], q_norm_g[l], k_norm_g[l], sinks[l],
        p_a[l], p_b[l], w_out[l], norm2_g[l], w_rg[l], b_rg[l], w_re[l], b_re[l],
        w_gate_e[l], w_up_e[l], w_down_e[l])

    k_all = qkv[:, q_dim:q_dim + kv_dim]
    v_all = qkv[:, q_dim + kv_dim:]
    k_win_p = k_all[s - keep:s].reshape(1, batch, keep, n_kv, head_dim)
    v_win_p = v_all[s - keep:s].reshape(1, batch, keep, n_kv, head_dim)
    k_new_s = k_all[s:].reshape(1, dec_batch, t, n_kv, head_dim)
    v_new_s = v_all[s:].reshape(1, dec_batch, t, n_kv, head_dim)
    gv_s = vn_s.reshape(1, dec_batch, t, -1)
    return (yp.reshape(batch, s, d), ys.reshape(dec_batch, t, d), k_win_p, v_win_p, k_new_s, v_new_s, gv_s)
```
